```python
import jax
import jax.numpy as jnp
from jax import lax
import numpy as np

D_MODEL = 1024
BATCH = 8
SEQ = 4096
DEPTH = 2

CTX_LEN = 256
GRID_W = 64
N_MOD = 6
RMS_EPS = 1e-6
ROPE_BASE = 10000.0
HEAD_DIM = 64
F32 = jnp.float32

RWKV_HEADS = 4
RWKV_DIM = RWKV_HEADS * HEAD_DIM
DECAY_LORA = 64
AAA_LORA = 64
GATE_LORA = 128
RWKV_GN_EPS = 64e-5

MLA_HEADS = 8
MLA_NOPE = 64
MLA_ROPE = 32
MLA_QK = MLA_NOPE + MLA_ROPE
MLA_V = 64
MLA_Q_RANK = 256
MLA_KV_RANK = 128
MLA_DIM = MLA_HEADS * MLA_V
Q_BLOCK = 128

RET_HEADS = 4
RET_KEY = 64
RET_VAL = 64
RET_DIM = RET_HEADS * RET_VAL
RET_CHUNK = 64

N_GROUPS = 4
EXPERTS_PER_GROUP = 8
N_EXPERTS = N_GROUPS * EXPERTS_PER_GROUP
TOP_K = 2
EXPERT_HIDDEN = 512

IN_WIDTHS = (RWKV_DIM, RWKV_DIM, RWKV_DIM, 2 * DECAY_LORA, 2 * AAA_LORA, GATE_LORA,
             MLA_Q_RANK, MLA_KV_RANK, MLA_ROPE,
             RET_HEADS * RET_KEY, RET_HEADS * RET_KEY, RET_DIM, RET_DIM)
D_IN = sum(IN_WIDTHS)

kernel_name = 'hybrid_rwkv7_mla_retention_hmoe_dit'


def rmsnorm(x, w):
    xf = x.astype(F32)
    y = xf * lax.rsqrt(jnp.mean(xf * xf, axis=-1, keepdims=True) + RMS_EPS)
    return (y * w.astype(F32)).astype(x.dtype)


def split_columns(u):
    parts, start = [], 0
    for width in IN_WIDTHS:
        parts.append(u[..., start:start + width])
        start += width
    return parts


def axial_rope_table(n_tokens, rot_dim):
    rows = n_tokens // GRID_W
    row = jnp.repeat(jnp.arange(rows, dtype=F32), GRID_W)
    col = jnp.tile(jnp.arange(GRID_W, dtype=F32), rows)
    n_freq = rot_dim // 4
    inv_freq = ROPE_BASE ** (-jnp.arange(n_freq, dtype=F32) / n_freq)
    ang = jnp.concatenate([row[:, None] * inv_freq, col[:, None] * inv_freq], axis=-1)
    return jnp.cos(ang), jnp.sin(ang)


def apply_rope(x, table):
    if table is None:
        return x
    cos, sin = table
    cos = cos[None, :, None, :]
    sin = sin[None, :, None, :]
    xf = x.astype(F32)
    half = x.shape[-1] // 2
    x1, x2 = xf[..., :half], xf[..., half:]
    return jnp.concatenate([x1 * cos - x2 * sin, x1 * sin + x2 * cos], axis=-1).astype(x.dtype)


def centred_conv3(u, w):
    up = jnp.pad(u, ((0, 0), (1, 1), (0, 0)))
    return up[:, :-2] * w[0] + up[:, 1:-1] * w[1] + up[:, 2:] * w[2]


def rwkv_prepare(parts, p):
    r, k, v, w_dn, a_dn, g_dn = parts
    B, L, _ = r.shape
    heads = lambda t: t.astype(F32).reshape(B, L, RWKV_HEADS, HEAD_DIM)
    r, k, v = jnp.split(centred_conv3(jnp.concatenate([r, k, v], axis=-1), p['rwkv_conv']), 3, axis=-1)
    kk = heads(k * p['rwkv_k_k'])
    kk = kk / jnp.maximum(jnp.linalg.norm(kk, axis=-1, keepdims=True), 1e-12)
    g = jax.nn.sigmoid(g_dn) @ p['rwkv_g_up']
    per_dir = []
    for d in range(2):
        w_lora = jnp.tanh(w_dn[..., d * DECAY_LORA:(d + 1) * DECAY_LORA]) @ p['rwkv_w_up'][d]
        log_w = -jax.nn.softplus(-(p['rwkv_w0'][d] + w_lora).astype(F32)) - 0.5
        decay = jnp.exp(-jnp.exp(log_w))
        a = jax.nn.sigmoid((p['rwkv_a0'][d] + a_dn[..., d * AAA_LORA:(d + 1) * AAA_LORA] @ p['rwkv_a_up'][d]).astype(F32))
        k_mod = k.astype(F32) * (1.0 + (a - 1.0) * p['rwkv_k_a'].astype(F32))
        per_dir.append((heads(decay), heads(k_mod), heads(a)))
    return heads(r), heads(k), heads(v), kk, g, per_dir


def rwkv7_scan(r, decay, k, v, kk, a, state0, reverse, with_outputs):
    def step(S, inp):
        r_t, w_t, k_t, v_t, kk_t, a_t = inp
        sa = jnp.einsum('bhvk,bhk->bhv', S, kk_t)
        S = S * w_t[:, :, None, :] - sa[..., None] * (kk_t * a_t)[:, :, None, :] + v_t[..., None] * k_t[:, :, None, :]
        y = jnp.einsum('bhvk,bhk->bhv', S, r_t) if with_outputs else None
        return S, y
    xs = tuple(jnp.moveaxis(t, 1, 0) for t in (r, decay, k, v, kk, a))
    S, ys = lax.scan(step, state0, xs, reverse=reverse)
    return S, (jnp.moveaxis(ys, 0, 1) if with_outputs else None)


def rwkv_output(y, prep, p):
    r, k, v, kk, g, _ = prep
    B, L, H, N = y.shape
    mu = jnp.mean(y, axis=-1, keepdims=True)
    var = jnp.mean(jnp.square(y - mu), axis=-1, keepdims=True)
    y = ((y - mu) * lax.rsqrt(var + RWKV_GN_EPS)).reshape(B, L, RWKV_DIM)
    y = y * p['rwkv_ln_w'].astype(F32) + p['rwkv_ln_b'].astype(F32)
    bonus = jnp.sum(r * k * p['rwkv_r_k'].astype(F32), axis=-1, keepdims=True) * v
    y = (y + bonus.reshape(B, L, RWKV_DIM)) * g.astype(F32)
    return y.astype(g.dtype)


def rwkv_mixer(prep_ctx, prep_lat, p, with_ctx):
    B = prep_lat[0].shape[0]
    zero = jnp.zeros((B, RWKV_HEADS, HEAD_DIM, HEAD_DIM), F32)

    def run(prep, d, S0, with_out):
        r, k, v, kk, g, per_dir = prep
        decay, k_mod, a = per_dir[d]
        return rwkv7_scan(r, decay, k_mod, v, kk, a, S0, d == 1, with_out)

    S_cf, y_cf = run(prep_ctx, 0, zero, with_ctx)
    S_cb, y_cb = run(prep_ctx, 1, zero, with_ctx)
    _, y_lf = run(prep_lat, 0, S_cf, True)
    _, y_lb = run(prep_lat, 1, S_cb, True)
    y_lat = rwkv_output(y_lf + y_lb, prep_lat, p)
    y_ctx = rwkv_output(y_cf + y_cb, prep_ctx, p) if with_ctx else None
    return y_ctx, y_lat


def mla_prepare(parts, p, rope):
    q_dn, kv_dn, k_rope = parts
    B, L, _ = q_dn.shape
    q = (rmsnorm(q_dn, p['mla_q_norm']) @ p['mla_w_uq']).reshape(B, L, MLA_HEADS, MLA_QK)
    kv = (rmsnorm(kv_dn, p['mla_kv_norm']) @ p['mla_w_ukv']).reshape(B, L, MLA_HEADS, MLA_NOPE + MLA_V)
    q = jnp.concatenate([q[..., :MLA_NOPE], apply_rope(q[..., MLA_NOPE:], rope)], axis=-1)
    k_rope = apply_rope(k_rope[:, :, None, :], rope)
    k = jnp.concatenate([kv[..., :MLA_NOPE], jnp.broadcast_to(k_rope, (B, L, MLA_HEADS, MLA_ROPE))], axis=-1)
    return q, k, kv[..., MLA_NOPE:]


def attend_blocks(q, k, v):
    B, Lq, H, dq = q.shape
    nb = Lq // Q_BLOCK
    qb = jnp.moveaxis(q.reshape(B, nb, Q_BLOCK, H, dq), 1, 0)
    scale = dq ** -0.5

    def one(q_blk):
        s = jnp.einsum('bqhd,bkhd->bhqk', q_blk, k, preferred_element_type=F32) * scale
        prob = jax.nn.softmax(s, axis=-1).astype(v.dtype)
        return jnp.einsum('bhqk,bkhd->bqhd', prob, v)

    out = lax.map(one, qb)
    return jnp.moveaxis(out, 0, 1).reshape(B, Lq, H * v.shape[-1])


def retention_prepare(parts, rope):
    q, k, v, g = parts
    B, L, _ = q.shape
    q = apply_rope(q.reshape(B, L, RET_HEADS, RET_KEY), rope).astype(F32)
    k = apply_rope(k.reshape(B, L, RET_HEADS, RET_KEY), rope).astype(F32) * (RET_KEY ** -0.5)
    v = v.reshape(B, L, RET_HEADS, RET_VAL).astype(F32)
    return q, k, v, g


def retention_chunkwise(q, k, v, log_gamma, R0, with_outputs):
    B, L, H, dk = q.shape
    dv = v.shape[-1]
    C = RET_CHUNK
    n = L // C
    idx = jnp.arange(C, dtype=F32)
    rel = idx[:, None] - idx[None, :]
    inner = jnp.where(rel[None] >= 0, jnp.exp(log_gamma[:, None, None] * jnp.maximum(rel, 0.0)[None]), 0.0)
    cross = jnp.exp(log_gamma[None, :] * (idx[:, None] + 1.0))
    tail = jnp.exp(log_gamma[None, :] * (C - 1.0 - idx)[:, None])
    chunk_decay = jnp.exp(log_gamma * C)
    to_chunks = lambda t: jnp.moveaxis(t.reshape(B, n, C, H, t.shape[-1]), 1, 0)

    def step(R, inp):
        qc, kc, vc = inp
        R_new = R * chunk_decay[None, :, None, None] + jnp.einsum('bjhd,jh,bjhe->bhde', kc, tail, vc)
        if with_outputs:
            s = jnp.einsum('bihd,bjhd->bhij', qc, kc) * inner[None]
            o = jnp.einsum('bhij,bjhe->bihe', s, vc) + jnp.einsum('bihd,bhde->bihe', qc, R) * cross[None, :, :, None]
        else:
            o = None
        return R_new, o

    R, o = lax.scan(step, R0, (to_chunks(q), to_chunks(k), to_chunks(v)))
    if with_outputs:
        o = jnp.moveaxis(o, 0, 1).reshape(B, L, H, dv)
    return R, o


def retention_output(y, g):
    B, L = y.shape[0], y.shape[1]
    y = y * lax.rsqrt(jnp.mean(y * y, axis=-1, keepdims=True) + RMS_EPS)
    return (jax.nn.silu(g.astype(F32)) * y.reshape(B, L, RET_DIM)).astype(g.dtype)


def retention_mixer(prep_ctx, prep_lat, p, with_ctx):
    B = prep_lat[0].shape[0]
    zero = jnp.zeros((B, RET_HEADS, RET_KEY, RET_VAL), F32)
    log_gamma = jax.nn.log_sigmoid(p['ret_decay'].astype(F32))
    flip = lambda t: jnp.flip(t, axis=1)

    def run(prep, d, R0, with_out):
        q, k, v, _ = prep
        if d == 1:
            q, k, v = flip(q), flip(k), flip(v)
        R, o = retention_chunkwise(q, k, v, log_gamma[d], R0, with_out)
        if d == 1 and with_out:
            o = flip(o)
        return R, o

    R_cf, o_cf = run(prep_ctx, 0, zero, with_ctx)
    R_cb, o_cb = run(prep_ctx, 1, zero, with_ctx)
    _, o_lf = run(prep_lat, 0, R_cf, True)
    _, o_lb = run(prep_lat, 1, R_cb, True)
    y_lat = retention_output(o_lf + o_lb, prep_lat[3])
    y_ctx = retention_output(o_cf + o_cb, prep_ctx[3]) if with_ctx else None
    return y_ctx, y_lat


def merge_branches(h, y_a, y_b, y_c, p):
    gates = jax.nn.sigmoid((h @ p['w_branch_gate'] + p['b_branch_gate']).astype(F32)).astype(h.dtype)
    g_a, g_b, g_c = jnp.split(gates, 3, axis=-1)
    m = g_a * (y_a @ p['w_branch_a']) + g_b * (y_b @ p['w_branch_b']) + g_c * (y_c @ p['w_branch_c'])
    return m @ p['w_out']


def hier_moe(h, p):
    n_tok = h.shape[0]
    grp_prob = jax.nn.softmax((h @ p['moe_w_group']).astype(F32) + p['moe_b_group'].astype(F32), axis=-1)
    grp_p, grp_idx = lax.top_k(grp_prob, 1)
    exp_logits = ((h @ p['moe_w_expert']).astype(F32) + p['moe_b_expert'].astype(F32)).reshape(n_tok, N_GROUPS, EXPERTS_PER_GROUP)
    in_group = jnp.einsum('nge,ng->ne', exp_logits, jax.nn.one_hot(grp_idx[:, 0], N_GROUPS, dtype=F32))
    exp_p, exp_idx = lax.top_k(jax.nn.softmax(in_group, axis=-1), TOP_K)
    weights = grp_p * exp_p / jnp.sum(exp_p, axis=-1, keepdims=True)
    expert_id = grp_idx * EXPERTS_PER_GROUP + exp_idx
    combine = jnp.einsum('nk,nke->ne', weights, jax.nn.one_hot(expert_id, N_EXPERTS, dtype=F32)).astype(h.dtype)
    y = jnp.zeros_like(h)
    for e in range(N_EXPERTS):
        hid = jax.nn.silu(h @ p['moe_w_gate'][e]) * (h @ p['moe_w_up'][e])
        y = y + (hid @ p['moe_w_down'][e]) * combine[:, e:e + 1]
    return y


def hybrid_layer(x_ctx, x_lat, mod_ctx, mod_lat, p, ropes, with_ctx):
    sh1c, sc1c, g1c, sh2c, sc2c, g2c = mod_ctx
    sh1l, sc1l, g1l, sh2l, sc2l, g2l = mod_lat
    h_ctx = rmsnorm(x_ctx, p['norm1_w']) * (1 + sc1c) + sh1c
    h_lat = rmsnorm(x_lat, p['norm1_w']) * (1 + sc1l) + sh1l
    pc = split_columns(h_ctx @ p['w_in'])
    pl = split_columns(h_lat @ p['w_in'])
    ya_ctx, ya_lat = rwkv_mixer(rwkv_prepare(pc[0:6], p), rwkv_prepare(pl[0:6], p), p, with_ctx)
    q_c, k_c, v_c = mla_prepare(pc[6:9], p, None)
    q_l, k_l, v_l = mla_prepare(pl[6:9], p, ropes[0])
    yb_lat = attend_blocks(q_l, jnp.concatenate([k_c, k_l], axis=1), jnp.concatenate([v_c, v_l], axis=1))
    yc_ctx, yc_lat = retention_mixer(retention_prepare(pc[9:13], None), retention_prepare(pl[9:13], ropes[1]), p, with_ctx)
    x_lat = x_lat + g1l * merge_branches(h_lat, ya_lat, yb_lat, yc_lat, p)
    h2_lat = rmsnorm(x_lat, p['norm2_w']) * (1 + sc2l) + sh2l
    if with_ctx:
        yb_ctx = attend_blocks(q_c, k_c, v_c)
        x_ctx = x_ctx + g1c * merge_branches(h_ctx, ya_ctx, yb_ctx, yc_ctx, p)
        h2_ctx = rmsnorm(x_ctx, p['norm2_w']) * (1 + sc2c) + sh2c
        n_ctx = h2_ctx.shape[0] * h2_ctx.shape[1]
        tokens = jnp.concatenate([h2_ctx.reshape(-1, D_MODEL), h2_lat.reshape(-1, D_MODEL)], axis=0)
        f = hier_moe(tokens, p)
        x_ctx = x_ctx + g2c * f[:n_ctx].reshape(x_ctx.shape)
        x_lat = x_lat + g2l * f[n_ctx:].reshape(x_lat.shape)
    else:
        x_lat = x_lat + g2l * hier_moe(h2_lat.reshape(-1, D_MODEL), p).reshape(x_lat.shape)
    return x_ctx, x_lat


def setup_inputs(seed: int = 0) -> dict:
    key = jax.random.key(seed)
    ks = iter(jax.random.split(key, 48))
    nrm = lambda shape, scale: jax.random.normal(next(ks), shape, F32) * scale
    D = D_MODEL
    head_idx = jnp.arange(RET_HEADS, dtype=F32)
    ret_logit = jnp.log(2.0 ** (5.0 + head_idx) - 1.0)
    return {
        'x': nrm((BATCH, SEQ, D), 1.0),
        'c': nrm((BATCH, D), 1.0),
        'ctx': nrm((BATCH, CTX_LEN, D), 1.0),
        'c_ctx': nrm((D,), 1.0),
        'w_mod': nrm((DEPTH, D, N_MOD * D), 0.5 * D ** -0.5),
        'b_mod': nrm((DEPTH, N_MOD * D), 0.02),
        'norm1_w': 1.0 + nrm((DEPTH, D), 0.02),
        'w_in': nrm((DEPTH, D, D_IN), D ** -0.5),
        'rwkv_conv': nrm((DEPTH, 3, 3 * RWKV_DIM), 0.3) + jax.nn.one_hot(1, 3, dtype=F32)[None, :, None],
        'rwkv_w0': jax.random.uniform(next(ks), (DEPTH, 2, RWKV_DIM), F32, -6.0, 1.0),
        'rwkv_w_up': nrm((DEPTH, 2, DECAY_LORA, RWKV_DIM), 0.5 * DECAY_LORA ** -0.5),
        'rwkv_a0': nrm((DEPTH, 2, RWKV_DIM), 0.5),
        'rwkv_a_up': nrm((DEPTH, 2, AAA_LORA, RWKV_DIM), 0.5 * AAA_LORA ** -0.5),
        'rwkv_g_up': nrm((DEPTH, GATE_LORA, RWKV_DIM), GATE_LORA ** -0.5),
        'rwkv_k_k': 0.85 + nrm((DEPTH, RWKV_DIM), 0.05),
        'rwkv_k_a': 1.0 + nrm((DEPTH, RWKV_DIM), 0.05),
        'rwkv_r_k': nrm((DEPTH, RWKV_HEADS, HEAD_DIM), 0.1),
        'rwkv_ln_w': 1.0 + nrm((DEPTH, RWKV_DIM), 0.02),
        'rwkv_ln_b': nrm((DEPTH, RWKV_DIM), 0.02),
        'mla_q_norm': 1.0 + nrm((DEPTH, MLA_Q_RANK), 0.02),
        'mla_w_uq': nrm((DEPTH, MLA_Q_RANK, MLA_HEADS * MLA_QK), MLA_Q_RANK ** -0.5),
        'mla_kv_norm': 1.0 + nrm((DEPTH, MLA_KV_RANK), 0.02),
        'mla_w_ukv': nrm((DEPTH, MLA_KV_RANK, MLA_HEADS * (MLA_NOPE + MLA_V)), MLA_KV_RANK ** -0.5),
        'ret_decay': ret_logit + nrm((DEPTH, 2, RET_HEADS), 0.1),
        'w_branch_a': nrm((DEPTH, RWKV_DIM, D), RWKV_DIM ** -0.5),
        'w_branch_b': nrm((DEPTH, MLA_DIM, D), MLA_DIM ** -0.5),
        'w_branch_c': nrm((DEPTH, RET_DIM, D), RET_DIM ** -0.5),
        'w_branch_gate': nrm((DEPTH, D, 3 * D), D ** -0.5),
        'b_branch_gate': nrm((DEPTH, 3 * D), 0.02),
        'w_out': nrm((DEPTH, D, D), D ** -0.5),
        'norm2_w': 1.0 + nrm((DEPTH, D), 0.02),
        'moe_w_group': nrm((DEPTH, D, N_GROUPS), D ** -0.5),
        'moe_b_group': nrm((DEPTH, N_GROUPS), 0.01),
        'moe_w_expert': nrm((DEPTH, D, N_EXPERTS), D ** -0.5),
        'moe_b_expert': nrm((DEPTH, N_EXPERTS), 0.01),
        'moe_w_gate': nrm((DEPTH, N_EXPERTS, D, EXPERT_HIDDEN), D ** -0.5),
        'moe_w_up': nrm((DEPTH, N_EXPERTS, D, EXPERT_HIDDEN), D ** -0.5),
        'moe_w_down': nrm((DEPTH, N_EXPERTS, EXPERT_HIDDEN, D), EXPERT_HIDDEN ** -0.5),
        'final_norm_w': 1.0 + nrm((D,), 0.02),
    }


def reference(x, c, ctx, c_ctx, w_mod, b_mod, norm1_w, w_in, rwkv_conv, rwkv_w0, rwkv_w_up, rwkv_a0,
              rwkv_a_up, rwkv_g_up, rwkv_k_k, rwkv_k_a, rwkv_r_k, rwkv_ln_w, rwkv_ln_b, mla_q_norm,
              mla_w_uq, mla_kv_norm, mla_w_ukv, ret_decay, w_branch_a, w_branch_b, w_branch_c,
              w_branch_gate, b_branch_gate, w_out, norm2_w, moe_w_group, moe_b_group, moe_w_expert,
              moe_b_expert, moe_w_gate, moe_w_up, moe_w_down, final_norm_w):
    n_lat = x.shape[1]
    ropes = (axial_rope_table(n_lat, MLA_ROPE), axial_rope_table(n_lat, RET_KEY))
    x_ctx, x_lat = ctx, x
    for l in range(DEPTH):
        p = {
            'norm1_w': norm1_w[l], 'w_in': w_in[l], 'rwkv_conv': rwkv_conv[l], 'rwkv_w0': rwkv_w0[l],
            'rwkv_w_up': rwkv_w_up[l], 'rwkv_a0': rwkv_a0[l], 'rwkv_a_up': rwkv_a_up[l],
            'rwkv_g_up': rwkv_g_up[l], 'rwkv_k_k': rwkv_k_k[l], 'rwkv_k_a': rwkv_k_a[l],
            'rwkv_r_k': rwkv_r_k[l], 'rwkv_ln_w': rwkv_ln_w[l], 'rwkv_ln_b': rwkv_ln_b[l],
            'mla_q_norm': mla_q_norm[l], 'mla_w_uq': mla_w_uq[l], 'mla_kv_norm': mla_kv_norm[l],
            'mla_w_ukv': mla_w_ukv[l], 'ret_decay': ret_decay[l], 'w_branch_a': w_branch_a[l],
            'w_branch_b': w_branch_b[l], 'w_branch_c': w_branch_c[l], 'w_branch_gate': w_branch_gate[l],
            'b_branch_gate': b_branch_gate[l], 'w_out': w_out[l], 'norm2_w': norm2_w[l],
            'moe_w_group': moe_w_group[l], 'moe_b_group': moe_b_group[l],
            'moe_w_expert': moe_w_expert[l], 'moe_b_expert': moe_b_expert[l],
            'moe_w_gate': moe_w_gate[l], 'moe_w_up': moe_w_up[l], 'moe_w_down': moe_w_down[l],
        }
        mod_lat = jnp.split((jax.nn.silu(c) @ w_mod[l] + b_mod[l])[:, None, :], N_MOD, axis=-1)
        mod_ctx = jnp.split((jax.nn.silu(c_ctx) @ w_mod[l] + b_mod[l])[None, None, :], N_MOD, axis=-1)
        x_ctx, x_lat = hybrid_layer(x_ctx, x_lat, mod_ctx, mod_lat, p, ropes, l < DEPTH - 1)
    return rmsnorm(x_lat, final_norm_w)
```

```python
import functools

import jax
import jax.numpy as jnp
from jax import lax
from jax.experimental import pallas as pl
from jax.experimental.pallas import tpu as pltpu

D_MODEL = 1024
CTX_LEN = 256
GRID_W = 64
N_MOD = 6
RMS_EPS = 1e-6
ROPE_BASE = 10000.0
HEAD_DIM = 64
F32 = jnp.float32
BF16 = jnp.bfloat16

RWKV_HEADS = 4
RWKV_DIM = RWKV_HEADS * HEAD_DIM
DECAY_LORA = 64
AAA_LORA = 64
GATE_LORA = 128
RWKV_GN_EPS = 64e-5

MLA_HEADS = 8
MLA_NOPE = 64
MLA_ROPE = 32
MLA_QK = MLA_NOPE + MLA_ROPE
MLA_V = 64
MLA_Q_RANK = 256
MLA_KV_RANK = 128
MLA_DIM = MLA_HEADS * MLA_V

RET_HEADS = 4
RET_KEY = 64
RET_VAL = 64
RET_DIM = RET_HEADS * RET_VAL

N_GROUPS = 4
EXPERTS_PER_GROUP = 8
N_EXPERTS = N_GROUPS * EXPERTS_PER_GROUP
TOP_K = 2
EXPERT_HIDDEN = 512

U_RWKV = 3 * RWKV_DIM + 2 * DECAY_LORA + 2 * AAA_LORA + GATE_LORA
U_MLA_RAW = MLA_Q_RANK + MLA_KV_RANK + MLA_ROPE
U_MLA = 512
U_RET = 4 * RET_DIM

ROW_TILE = 256
RWKV_CHUNK = 64
RET_CHUNK = 256
ROUTER_LANES = 128
MOE_TILE = 256
VMEM_LIMIT = 56 * 1024 * 1024


def _dot(a, b):
    return jnp.dot(a, b, preferred_element_type=F32)


def _dot_nt(a, b):
    return lax.dot_general(a, b, (((1,), (1,)), ((), ())), preferred_element_type=F32)


def _inproj_kernel(x_ref, mod_ref, nw_ref, wa_ref, wb_ref, wc_ref, h_ref, ua_ref, ub_ref, uc_ref):
    x = x_ref[0]
    y = x * lax.rsqrt(jnp.mean(x * x, axis=-1, keepdims=True) + RMS_EPS) * nw_ref[...]
    h = y * (1.0 + mod_ref[0, 0, 1:2, :]) + mod_ref[0, 0, 0:1, :]
    hb = h.astype(BF16)
    h_ref[0] = hb
    ua_ref[0] = _dot(hb, wa_ref[...])
    ub_ref[0] = _dot(hb, wb_ref[...])
    uc_ref[0] = _dot(hb, wc_ref[...])


def _inproj(x_all, mods, norm_w, w_a, w_b, w_c):
    B, T, D = x_all.shape
    nt = T // ROW_TILE
    tok = lambda w: pl.BlockSpec((1, ROW_TILE, w), lambda b, j: (b, j, 0))
    full = lambda a: pl.BlockSpec(a.shape, lambda b, j: (0,) * a.ndim)
    return pl.pallas_call(
        _inproj_kernel,
        grid=(B, nt),
        in_specs=[tok(D),
                  pl.BlockSpec((1, 1, 8, D), lambda b, j: (b, jnp.minimum(j, 1), 0, 0)),
                  full(norm_w), full(w_a), full(w_b), full(w_c)],
        out_specs=[tok(D), tok(U_RWKV), tok(U_MLA), tok(U_RET)],
        out_shape=[jax.ShapeDtypeStruct((B, T, D), BF16),
                   jax.ShapeDtypeStruct((B, T, U_RWKV), F32),
                   jax.ShapeDtypeStruct((B, T, U_MLA), F32),
                   jax.ShapeDtypeStruct((B, T, U_RET), F32)],
        compiler_params=pltpu.CompilerParams(
            dimension_semantics=("parallel", "parallel"), vmem_limit_bytes=VMEM_LIMIT),
        name="inproj",
    )(x_all, mods, norm_w, w_a, w_b, w_c)


def _rwkv_kernel(kap_ref, w_ref, beta_ref, r_ref, kaph_ref, kch_ref, rh_ref, kend_ref, vT_ref,
                 yT_ref, p_ref, bvt_ref, yq_ref, yp_ref):
    T = RWKV_CHUNK
    n_dir, n_head = kap_ref.shape[1], kap_ref.shape[2]
    chains = [(d, h) for d in range(n_dir) for h in range(n_head)]

    @pl.when(pl.program_id(1) == 0)
    def _():
        p_ref[...] = jnp.zeros_like(p_ref)

    row_t = lax.broadcasted_iota(jnp.int32, (T, T), 0)
    col_j = lax.broadcasted_iota(jnp.int32, (T, T), 1)
    for ch, (d, h) in enumerate(chains):
        kch = kch_ref[0, d, h]
        vT = vT_ref[0, d, h, 0]
        b_m = jnp.where(col_j < row_t, _dot_nt(kaph_ref[0, d, h], kch), 0.0)
        e_m = jnp.where(col_j <= row_t, _dot_nt(rh_ref[0, d, h], kch), 0.0)
        bvt_ref[ch] = _dot_nt(vT, b_m)
        yq_ref[ch] = _dot_nt(vT, e_m)

    lane = lax.broadcasted_iota(jnp.int32, (HEAD_DIM, T), 1)

    def step(t, carry):
        at_t = lane == t
        for ch, (d, h) in enumerate(chains):
            p = p_ref[ch]
            kap = kap_ref[0, d, h, pl.ds(t, 1), :]
            sa = jnp.sum(p * kap + jnp.where(at_t, bvt_ref[ch], 0.0), axis=-1, keepdims=True)
            p_new = p * w_ref[0, d, h, pl.ds(t, 1), :] - sa * beta_ref[0, d, h, pl.ds(t, 1), :]
            p_ref[ch] = p_new
            yp = jnp.sum(p_new * r_ref[0, d, h, pl.ds(t, 1), :], axis=-1, keepdims=True)
            yp_ref[ch] = jnp.where(at_t, yp, yp_ref[ch])
        return carry

    lax.fori_loop(0, T, step, 0)

    for ch, (d, h) in enumerate(chains):
        yT_ref[0, d, h, 0] = yp_ref[ch] + yq_ref[ch]
        p_ref[ch] = p_ref[ch] + _dot(vT_ref[0, d, h, 0], kend_ref[0, d, h])


def _rwkv_scan(kap, w, beta, r, kaph, kch, rh, kend, vT):
    B, n_dir, H, L, N = kap.shape
    nc = L // RWKV_CHUNK
    seq = pl.BlockSpec((1, n_dir, H, RWKV_CHUNK, N), lambda b, c: (b, 0, 0, c, 0))
    tr = pl.BlockSpec((1, n_dir, H, 1, N, RWKV_CHUNK), lambda b, c: (b, 0, 0, c, 0, 0))
    n_chain = n_dir * H
    return pl.pallas_call(
        _rwkv_kernel,
        grid=(B, nc),
        in_specs=[seq] * 8 + [tr],
        out_specs=tr,
        out_shape=jax.ShapeDtypeStruct((B, n_dir, H, nc, N, RWKV_CHUNK), F32),
        scratch_shapes=[pltpu.VMEM((n_chain, N, N), F32),
                        pltpu.VMEM((n_chain, N, RWKV_CHUNK), F32),
                        pltpu.VMEM((n_chain, N, RWKV_CHUNK), F32),
                        pltpu.VMEM((n_chain, N, RWKV_CHUNK), F32)],
        compiler_params=pltpu.CompilerParams(
            dimension_semantics=("parallel", "arbitrary"), vmem_limit_bytes=VMEM_LIMIT),
        name="rwkv_scan",
    )(kap, w, beta, r, kaph, kch, rh, kend, vT)


def _attn_kernel(q_ref, k_ref, v_ref, o_ref, *, scale):
    outs = []
    for h in range(q_ref.shape[1]):
        s = _dot_nt(q_ref[0, h], k_ref[0, h]) * scale
        p = jnp.exp(s - jnp.max(s, axis=-1, keepdims=True))
        l = jnp.sum(p, axis=-1, keepdims=True)
        outs.append(_dot(p.astype(BF16), v_ref[0, h]) / l)
    o_ref[0] = jnp.concatenate(outs, axis=-1).astype(o_ref.dtype)


def _attention(q, k, v):
    B, H, Lq, dq = q.shape
    Lk, dv = k.shape[2], v.shape[3]
    return pl.pallas_call(
        functools.partial(_attn_kernel, scale=dq ** -0.5),
        grid=(B, Lq // ROW_TILE),
        in_specs=[pl.BlockSpec((1, H, ROW_TILE, dq), lambda b, i: (b, 0, i, 0)),
                  pl.BlockSpec((1, H, Lk, dq), lambda b, i: (b, 0, 0, 0)),
                  pl.BlockSpec((1, H, Lk, dv), lambda b, i: (b, 0, 0, 0))],
        out_specs=pl.BlockSpec((1, ROW_TILE, H * dv), lambda b, i: (b, i, 0)),
        out_shape=jax.ShapeDtypeStruct((B, Lq, H * dv), BF16),
        compiler_params=pltpu.CompilerParams(
            dimension_semantics=("parallel", "parallel"), vmem_limit_bytes=VMEM_LIMIT),
        name="mla_attention",
    )(q, k, v)


def _ret_kernel(q_ref, k_ref, kT_ref, v_ref, inner_ref, cross_ref, tailT_ref, cdec_ref, o_ref, r_ref):
    @pl.when(pl.program_id(3) == 0)
    def _():
        r_ref[...] = jnp.zeros_like(r_ref)

    q = q_ref[0, 0, 0]
    v = v_ref[0, 0, 0].astype(BF16)
    state = r_ref[...]
    s = _dot_nt(q.astype(BF16), k_ref[0, 0, 0].astype(BF16)) * inner_ref[0, 0]
    o = _dot(s.astype(BF16), v) + _dot(q.astype(BF16), state.astype(BF16)) * cross_ref[0, 0]
    o_ref[0, 0, 0] = o
    r_ref[...] = state * cdec_ref[0, 0] + _dot((kT_ref[0, 0, 0] * tailT_ref[0, 0]).astype(BF16), v)


def _retention_scan(q, k, kT, v, inner, cross, tailT, cdec):
    n_dir, B, H, L, N = q.shape
    C = RET_CHUNK
    seq = pl.BlockSpec((1, 1, 1, C, N), lambda d, b, h, c: (d, b, h, c, 0))
    per_head = lambda a: pl.BlockSpec((1, 1) + a.shape[2:], lambda d, b, h, c: (d, h, 0, 0))
    return pl.pallas_call(
        _ret_kernel,
        grid=(n_dir, B, H, L // C),
        in_specs=[seq, seq, pl.BlockSpec((1, 1, 1, N, C), lambda d, b, h, c: (d, b, h, 0, c)), seq,
                  per_head(inner), per_head(cross), per_head(tailT), per_head(cdec)],
        out_specs=seq,
        out_shape=jax.ShapeDtypeStruct((n_dir, B, H, L, N), F32),
        scratch_shapes=[pltpu.VMEM((N, N), F32)],
        compiler_params=pltpu.CompilerParams(
            dimension_semantics=("parallel", "parallel", "parallel", "arbitrary"),
            vmem_limit_bytes=VMEM_LIMIT),
        name="retention_scan",
    )(q, k, kT, v, inner, cross, tailT, cdec)


def _merge_kernel(x_ref, h_ref, ya_ref, yb_ref, yc_ref, mod_ref, wg_ref, bg_ref, wa_ref, wb_ref, wc_ref,
                  wo_ref, n2_ref, wr_ref, br_ref, xo_ref, h2_ref, lg_ref):
    D = D_MODEL
    gates = jax.nn.sigmoid(_dot(h_ref[0], wg_ref[...]) + bg_ref[...])
    m = (gates[:, :D] * _dot(ya_ref[0], wa_ref[...])
         + gates[:, D:2 * D] * _dot(yb_ref[0], wb_ref[...])
         + gates[:, 2 * D:] * _dot(yc_ref[0], wc_ref[...]))
    xn = x_ref[0] + mod_ref[0, 0, 2:3, :] * _dot(m.astype(BF16), wo_ref[...])
    xo_ref[0] = xn
    y = xn * lax.rsqrt(jnp.mean(xn * xn, axis=-1, keepdims=True) + RMS_EPS) * n2_ref[...]
    h2 = y * (1.0 + mod_ref[0, 0, 4:5, :]) + mod_ref[0, 0, 3:4, :]
    h2_ref[0] = h2.astype(BF16)
    lg_ref[0] = jnp.dot(h2, wr_ref[...], preferred_element_type=F32,
                        precision=lax.Precision.HIGHEST) + br_ref[...]


def _merge(x_all, h, ya, yb, yc, mods, wg, bg, wa, wb, wc, wo, n2, wr, br, tile_off):
    B, T, D = x_all.shape
    nt = T // ROW_TILE - tile_off
    tok = lambda w: pl.BlockSpec((1, ROW_TILE, w), lambda b, j: (b, j + tile_off, 0))
    out = lambda w: pl.BlockSpec((1, ROW_TILE, w), lambda b, j: (b, j, 0))
    full = lambda a: pl.BlockSpec(a.shape, lambda b, j: (0,) * a.ndim)
    return pl.pallas_call(
        _merge_kernel,
        grid=(B, nt),
        in_specs=[tok(D), tok(D), tok(RWKV_DIM), tok(MLA_DIM), tok(RET_DIM),
                  pl.BlockSpec((1, 1, 8, D), lambda b, j: (b, jnp.minimum(j + tile_off, 1), 0, 0)),
                  full(wg), full(bg), full(wa), full(wb), full(wc), full(wo), full(n2), full(wr), full(br)],
        out_specs=[out(D), out(D), out(ROUTER_LANES)],
        out_shape=[jax.ShapeDtypeStruct((B, nt * ROW_TILE, D), F32),
                   jax.ShapeDtypeStruct((B, nt * ROW_TILE, D), BF16),
                   jax.ShapeDtypeStruct((B, nt * ROW_TILE, ROUTER_LANES), F32)],
        compiler_params=pltpu.CompilerParams(
            dimension_semantics=("parallel", "parallel"), vmem_limit_bytes=VMEM_LIMIT),
        name="merge",
    )(x_all, h, ya, yb, yc, mods, wg, bg, wa, wb, wc, wo, n2, wr, br)


def _moe_kernel(te_ref, x_ref, cw_ref, wg_ref, wu_ref, wd_ref, o_ref):
    del te_ref
    x = x_ref[...]
    hid = jax.nn.silu(_dot(x, wg_ref[0])) * _dot(x, wu_ref[0])
    o_ref[...] = _dot(hid.astype(BF16), wd_ref[0]) * cw_ref[...]


def _moe_experts(tile_expert, x_sorted, cw_sorted, w_gate, w_up, w_down):
    n_rows, D = x_sorted.shape
    n_tiles = n_rows // MOE_TILE
    grid_spec = pltpu.PrefetchScalarGridSpec(
        num_scalar_prefetch=1,
        grid=(n_tiles,),
        in_specs=[pl.BlockSpec((MOE_TILE, D), lambda i, te: (i, 0)),
                  pl.BlockSpec((MOE_TILE, 1), lambda i, te: (i, 0)),
                  pl.BlockSpec((1, D, EXPERT_HIDDEN), lambda i, te: (te[i], 0, 0)),
                  pl.BlockSpec((1, D, EXPERT_HIDDEN), lambda i, te: (te[i], 0, 0)),
                  pl.BlockSpec((1, EXPERT_HIDDEN, D), lambda i, te: (te[i], 0, 0))],
        out_specs=pl.BlockSpec((MOE_TILE, D), lambda i, te: (i, 0)),
    )
    return pl.pallas_call(
        _moe_kernel,
        grid_spec=grid_spec,
        out_shape=jax.ShapeDtypeStruct((n_rows, D), F32),
        compiler_params=pltpu.CompilerParams(
            dimension_semantics=("arbitrary",), vmem_limit_bytes=VMEM_LIMIT),
        name="moe_experts",
    )(tile_expert, x_sorted, cw_sorted, w_gate, w_up, w_down)


def _final_norm_kernel(x_ref, w_ref, o_ref):
    x = x_ref[...]
    o_ref[...] = x * lax.rsqrt(jnp.mean(x * x, axis=-1, keepdims=True) + RMS_EPS) * w_ref[...]


def _final_norm(x, w):
    n_rows, D = x.shape
    tile = 512
    return pl.pallas_call(
        _final_norm_kernel,
        grid=(n_rows // tile,),
        in_specs=[pl.BlockSpec((tile, D), lambda i: (i, 0)), pl.BlockSpec((1, D), lambda i: (0, 0))],
        out_specs=pl.BlockSpec((tile, D), lambda i: (i, 0)),
        out_shape=jax.ShapeDtypeStruct((n_rows, D), F32),
        compiler_params=pltpu.CompilerParams(dimension_semantics=("parallel",)),
        name="final_norm",
    )(x, w)


def _rmsnorm(x, w):
    return x * lax.rsqrt(jnp.mean(x * x, axis=-1, keepdims=True) + RMS_EPS) * w


def _rope_table(n_tokens, rot_dim):
    rows = n_tokens // GRID_W
    row = jnp.repeat(jnp.arange(rows, dtype=F32), GRID_W)
    col = jnp.tile(jnp.arange(GRID_W, dtype=F32), rows)
    n_freq = rot_dim // 4
    inv_freq = ROPE_BASE ** (-jnp.arange(n_freq, dtype=F32) / n_freq)
    ang = jnp.concatenate([row[:, None] * inv_freq, col[:, None] * inv_freq], axis=-1)
    return jnp.cos(ang), jnp.sin(ang)


def _rope(x, table):
    cos, sin = table
    cos = cos[None, :, None, :]
    sin = sin[None, :, None, :]
    half = x.shape[-1] // 2
    x1, x2 = x[..., :half], x[..., half:]
    return jnp.concatenate([x1 * cos - x2 * sin, x1 * sin + x2 * cos], axis=-1)


def _conv3(u, w):
    up = jnp.pad(u, ((0, 0), (1, 1), (0, 0)))
    return up[:, :-2] * w[0] + up[:, 1:-1] * w[1] + up[:, 2:] * w[2]


def _to_heads(t, n_heads):
    B, L, _ = t.shape
    return jnp.transpose(t.reshape(B, L, n_heads, -1), (0, 2, 1, 3))


def _scan_order(seg_ctx, seg_lat):
    fwd = jnp.concatenate([seg_ctx, seg_lat], axis=1)
    bwd = jnp.concatenate([jnp.flip(seg_ctx, axis=1), jnp.flip(seg_lat, axis=1)], axis=1)
    return jnp.stack([fwd, bwd], axis=1)


def _from_scan_order(y):
    fwd, bwd = y[:, 0], y[:, 1]
    bwd = jnp.concatenate([jnp.flip(bwd[:, :CTX_LEN], axis=1), jnp.flip(bwd[:, CTX_LEN:], axis=1)], axis=1)
    return fwd + bwd


def _rwkv_segment(u, p):
    R = RWKV_DIM
    w_dn = u[..., 3 * R:3 * R + 2 * DECAY_LORA]
    a_dn = u[..., 3 * R + 2 * DECAY_LORA:3 * R + 2 * DECAY_LORA + 2 * AAA_LORA]
    g_dn = u[..., 3 * R + 2 * DECAY_LORA + 2 * AAA_LORA:]
    rkv = _conv3(u[..., :3 * R], p['rwkv_conv'])
    r, k, v = rkv[..., :R], rkv[..., R:2 * R], rkv[..., 2 * R:]
    B, Lx, _ = r.shape
    kk = (k * p['rwkv_k_k']).reshape(B, Lx, RWKV_HEADS, HEAD_DIM)
    kk = (kk / jnp.maximum(jnp.linalg.norm(kk, axis=-1, keepdims=True), 1e-12)).reshape(B, Lx, R)
    g = jax.nn.sigmoid(g_dn) @ p['rwkv_g_up']
    out = {'r': r, 'k': k, 'v': v, 'kk': kk, 'g': g, 'logw': [], 'kmod': [], 'beta': []}
    for d in range(2):
        w_lora = jnp.tanh(w_dn[..., d * DECAY_LORA:(d + 1) * DECAY_LORA]) @ p['rwkv_w_up'][d]
        log_w = -jax.nn.softplus(-(p['rwkv_w0'][d] + w_lora)) - 0.5
        a = jax.nn.sigmoid(p['rwkv_a0'][d] + a_dn[..., d * AAA_LORA:(d + 1) * AAA_LORA] @ p['rwkv_a_up'][d])
        out['logw'].append(-jnp.exp(log_w))
        out['kmod'].append(k * (1.0 + (a - 1.0) * p['rwkv_k_a']))
        out['beta'].append(kk * a)
    return out


def _rwkv_branch(u_rwkv, p):
    B = u_rwkv.shape[0]
    sc = _rwkv_segment(u_rwkv[:, :CTX_LEN], p)
    sl = _rwkv_segment(u_rwkv[:, CTX_LEN:], p)
    L = u_rwkv.shape[1]
    T = RWKV_CHUNK
    nc = L // T

    def streams(name, per_dir):
        if per_dir:
            fwd = jnp.concatenate([sc[name][0], sl[name][0]], axis=1)
            bwd = jnp.concatenate([jnp.flip(sc[name][1], axis=1), jnp.flip(sl[name][1], axis=1)], axis=1)
            y = jnp.stack([fwd, bwd], axis=1)
        else:
            y = _scan_order(sc[name], sl[name])
        return jnp.transpose(y.reshape(B, 2, L, RWKV_HEADS, HEAD_DIM), (0, 1, 3, 2, 4))

    r, kap, v = streams('r', False), streams('kk', False), streams('v', False)
    logw, kmod, beta = streams('logw', True), streams('kmod', True), streams('beta', True)
    chunked = lambda t: t.reshape(B, 2, RWKV_HEADS, nc, T, HEAD_DIM)
    unchunk = lambda t: t.reshape(B, 2, RWKV_HEADS, L, HEAD_DIM)
    cs = jnp.cumsum(chunked(logw), axis=4)
    cs_prev = cs - chunked(logw)
    kaph = unchunk(chunked(kap) * jnp.exp(cs_prev))
    kch = unchunk(chunked(kmod) * jnp.exp(-cs))
    rh = unchunk(chunked(r) * jnp.exp(cs))
    kend = unchunk(chunked(kmod) * jnp.exp(cs[:, :, :, :, -1:, :] - cs))
    vT = jnp.swapaxes(chunked(v), 4, 5)
    yT = _rwkv_scan(kap, jnp.exp(logw), beta, r, kaph, kch, rh, kend, vT)
    y = jnp.swapaxes(yT, 4, 5).reshape(B, 2, RWKV_HEADS, L, HEAD_DIM)
    y = jnp.transpose(y, (0, 1, 3, 2, 4)).reshape(B, 2, L, RWKV_DIM)
    y = _from_scan_order(y).reshape(B, L, RWKV_HEADS, HEAD_DIM)

    cat = lambda name: jnp.concatenate([sc[name], sl[name]], axis=1)
    r4, k4, v4 = (cat(n).reshape(B, L, RWKV_HEADS, HEAD_DIM) for n in ('r', 'k', 'v'))
    g = cat('g')
    mu = jnp.mean(y, axis=-1, keepdims=True)
    var = jnp.mean(jnp.square(y - mu), axis=-1, keepdims=True)
    y = ((y - mu) * lax.rsqrt(var + RWKV_GN_EPS)).reshape(B, L, RWKV_DIM)
    y = y * p['rwkv_ln_w'] + p['rwkv_ln_b']
    bonus = jnp.sum(r4 * k4 * p['rwkv_r_k'], axis=-1, keepdims=True) * v4
    return ((y + bonus.reshape(B, L, RWKV_DIM)) * g).astype(BF16)


def _mla_segment(u, p, rope):
    B, Lx, _ = u.shape
    q_dn = u[..., :MLA_Q_RANK]
    kv_dn = u[..., MLA_Q_RANK:MLA_Q_RANK + MLA_KV_RANK]
    k_rope = u[..., MLA_Q_RANK + MLA_KV_RANK:U_MLA_RAW]
    q = (_rmsnorm(q_dn, p['mla_q_norm']) @ p['mla_w_uq']).reshape(B, Lx, MLA_HEADS, MLA_QK)
    kv = (_rmsnorm(kv_dn, p['mla_kv_norm']) @ p['mla_w_ukv']).reshape(B, Lx, MLA_HEADS, MLA_NOPE + MLA_V)
    q_r, k_r = q[..., MLA_NOPE:], k_rope[:, :, None, :]
    if rope is not None:
        q_r, k_r = _rope(q_r, rope), _rope(k_r, rope)
    q = jnp.concatenate([q[..., :MLA_NOPE], q_r], axis=-1)
    k = jnp.concatenate([kv[..., :MLA_NOPE], jnp.broadcast_to(k_r, (B, Lx, MLA_HEADS, MLA_ROPE))], axis=-1)
    hm = lambda t: jnp.transpose(t, (0, 2, 1, 3)).astype(BF16)
    return hm(q), hm(k), hm(kv[..., MLA_NOPE:])


def _mla_branch(u_mla, p, rope, with_ctx):
    q_c, k_c, v_c = _mla_segment(u_mla[:, :CTX_LEN], p, None)
    q_l, k_l, v_l = _mla_segment(u_mla[:, CTX_LEN:], p, rope)
    y_lat = _attention(q_l, jnp.concatenate([k_c, k_l], axis=2), jnp.concatenate([v_c, v_l], axis=2))
    if with_ctx:
        y_ctx = _attention(q_c, k_c, v_c)
    else:
        y_ctx = jnp.zeros((u_mla.shape[0], CTX_LEN, MLA_DIM), BF16)
    return jnp.concatenate([y_ctx, y_lat], axis=1)


def _retention_branch(u_ret, p, rope):
    B, L, _ = u_ret.shape
    C = RET_CHUNK
    seg = lambda t, i: t[..., i * RET_DIM:(i + 1) * RET_DIM]
    heads = lambda t: t.reshape(t.shape[0], t.shape[1], RET_HEADS, RET_KEY)
    u_c, u_l = u_ret[:, :CTX_LEN], u_ret[:, CTX_LEN:]
    flat = lambda t: t.reshape(t.shape[0], t.shape[1], RET_DIM)
    q_c, q_l = seg(u_c, 0), flat(_rope(heads(seg(u_l, 0)), rope))
    k_c, k_l = seg(u_c, 1) * (RET_KEY ** -0.5), flat(_rope(heads(seg(u_l, 1)), rope)) * (RET_KEY ** -0.5)
    v_c, v_l = seg(u_c, 2), seg(u_l, 2)
    to_streams = lambda c, l: jnp.transpose(
        _scan_order(c, l).reshape(B, 2, L, RET_HEADS, RET_KEY), (1, 0, 3, 2, 4))
    q, k, v = to_streams(q_c, q_l), to_streams(k_c, k_l), to_streams(v_c, v_l)
    log_gamma = jax.nn.log_sigmoid(p['ret_decay'])
    idx = jnp.arange(C, dtype=F32)
    rel = idx[:, None] - idx[None, :]
    lg = log_gamma[:, :, None, None]
    inner = jnp.where(rel[None, None] >= 0, jnp.exp(lg * jnp.maximum(rel, 0.0)[None, None]), 0.0)
    cross = jnp.broadcast_to(jnp.exp(lg * (idx[:, None] + 1.0)[None, None]), (2, RET_HEADS, C, RET_VAL))
    tailT = jnp.broadcast_to(jnp.exp(lg * (C - 1.0 - idx)[None, None, None, :]), (2, RET_HEADS, RET_KEY, C))
    cdec = jnp.broadcast_to(jnp.exp(lg * C), (2, RET_HEADS, RET_KEY, RET_VAL))
    o = _retention_scan(q, k, jnp.swapaxes(k, 3, 4), v, inner, cross, tailT, cdec)
    o = jnp.transpose(o, (1, 0, 3, 2, 4)).reshape(B, 2, L, RET_DIM)
    y = _from_scan_order(o).reshape(B, L, RET_HEADS, RET_VAL)
    y = y * lax.rsqrt(jnp.mean(y * y, axis=-1, keepdims=True) + RMS_EPS)
    return (jax.nn.silu(seg(u_ret, 3)) * y.reshape(B, L, RET_DIM)).astype(BF16)


def _route(logits, p):
    n_tok = logits.shape[0]
    grp_prob = jax.nn.softmax(logits[:, :N_GROUPS], axis=-1)
    grp_p, grp_idx = lax.top_k(grp_prob, 1)
    exp_logits = logits[:, N_GROUPS:N_GROUPS + N_EXPERTS].reshape(n_tok, N_GROUPS, EXPERTS_PER_GROUP)
    in_group = jnp.take_along_axis(exp_logits, grp_idx[:, :, None], axis=1)[:, 0]
    exp_p, exp_idx = lax.top_k(jax.nn.softmax(in_group, axis=-1), TOP_K)
    weights = grp_p * exp_p / jnp.sum(exp_p, axis=-1, keepdims=True)
    return grp_idx * EXPERTS_PER_GROUP + exp_idx, weights


def _moe(h2, logits, p):
    n_tok, D = h2.shape
    expert_id, weights = _route(logits, p)
    n_assign = n_tok * TOP_K
    flat_e = expert_id.reshape(n_assign).astype(jnp.int32)
    flat_w = weights.reshape(n_assign)
    flat_tok = jnp.arange(n_assign, dtype=jnp.int32) // TOP_K
    counts = jnp.zeros((N_EXPERTS,), jnp.int32).at[flat_e].add(1)
    padded = ((counts + MOE_TILE - 1) // MOE_TILE) * MOE_TILE
    pad_start = jnp.cumsum(padded) - padded
    raw_start = jnp.cumsum(counts) - counts
    order = jnp.argsort(flat_e, stable=True)
    sorted_e = flat_e[order]
    pos_sorted = pad_start[sorted_e] + (jnp.arange(n_assign, dtype=jnp.int32) - raw_start[sorted_e])
    n_tiles = n_assign // MOE_TILE + N_EXPERTS
    n_rows = n_tiles * MOE_TILE
    src_tok = jnp.zeros((n_rows,), jnp.int32).at[pos_sorted].set(flat_tok[order])
    cw = jnp.zeros((n_rows,), F32).at[pos_sorted].set(flat_w[order])
    pos = jnp.zeros((n_assign,), jnp.int32).at[order].set(pos_sorted).reshape(n_tok, TOP_K)
    tile_start = jnp.arange(n_tiles, dtype=jnp.int32) * MOE_TILE
    tile_expert = jnp.minimum(
        jnp.sum(tile_start[:, None] >= jnp.cumsum(padded)[None, :], axis=1), N_EXPERTS - 1).astype(jnp.int32)
    out = _moe_experts(tile_expert, h2[src_tok], cw[:, None], p['moe_w_gate'], p['moe_w_up'], p['moe_w_down'])
    return out[pos[:, 0]] + out[pos[:, 1]]


def _permute_w_in(w_in):
    w_a = w_in[:, :U_RWKV]
    w_b = jnp.pad(w_in[:, U_RWKV:U_RWKV + U_MLA_RAW], ((0, 0), (0, U_MLA - U_MLA_RAW)))
    w_c = w_in[:, U_RWKV + U_MLA_RAW:]
    return w_a.astype(BF16), w_b.astype(BF16), w_c.astype(BF16)


def kernel(x, c, ctx, c_ctx, w_mod, b_mod, norm1_w, w_in, rwkv_conv, rwkv_w0, rwkv_w_up, rwkv_a0, rwkv_a_up, rwkv_g_up, rwkv_k_k, rwkv_k_a, rwkv_r_k, rwkv_ln_w, rwkv_ln_b, mla_q_norm, mla_w_uq, mla_kv_norm, mla_w_ukv, ret_decay, w_branch_a, w_branch_b, w_branch_c, w_branch_gate, b_branch_gate, w_out, norm2_w, moe_w_group, moe_b_group, moe_w_expert, moe_b_expert, moe_w_gate, moe_w_up, moe_w_down, final_norm_w):
    B, n_lat, D = x.shape
    depth = w_mod.shape[0]
    ropes = (_rope_table(n_lat, MLA_ROPE), _rope_table(n_lat, RET_KEY))
    x_all = jnp.concatenate([ctx, x], axis=1)
    L = x_all.shape[1]
    for l in range(depth):
        with_ctx = l < depth - 1
        p = {
            'rwkv_conv': rwkv_conv[l], 'rwkv_w0': rwkv_w0[l], 'rwkv_w_up': rwkv_w_up[l],
            'rwkv_a0': rwkv_a0[l], 'rwkv_a_up': rwkv_a_up[l], 'rwkv_g_up': rwkv_g_up[l],
            'rwkv_k_k': rwkv_k_k[l], 'rwkv_k_a': rwkv_k_a[l], 'rwkv_r_k': rwkv_r_k[l].reshape(RWKV_HEADS, HEAD_DIM),
            'rwkv_ln_w': rwkv_ln_w[l], 'rwkv_ln_b': rwkv_ln_b[l],
            'mla_q_norm': mla_q_norm[l], 'mla_w_uq': mla_w_uq[l], 'mla_kv_norm': mla_kv_norm[l],
            'mla_w_ukv': mla_w_ukv[l], 'ret_decay': ret_decay[l],
            'moe_w_gate': moe_w_gate[l].astype(BF16), 'moe_w_up': moe_w_up[l].astype(BF16),
            'moe_w_down': moe_w_down[l].astype(BF16),
        }
        mod_lat = jax.nn.silu(c) @ w_mod[l] + b_mod[l]
        mod_ctx = jax.nn.silu(c_ctx) @ w_mod[l] + b_mod[l]
        mods = jnp.stack([jnp.broadcast_to(mod_ctx, mod_lat.shape), mod_lat], axis=1).reshape(B, 2, N_MOD, D)
        mods = jnp.pad(mods, ((0, 0), (0, 0), (0, 8 - N_MOD), (0, 0)))

        w_a, w_b, w_c = _permute_w_in(w_in[l])
        h, u_rwkv, u_mla, u_ret = _inproj(x_all, mods, norm1_w[l][None, :], w_a, w_b, w_c)

        ya = _rwkv_branch(u_rwkv, p)
        yb = _mla_branch(u_mla, p, ropes[0], with_ctx)
        yc = _retention_branch(u_ret, p, ropes[1])

        w_router = jnp.pad(jnp.concatenate([moe_w_group[l], moe_w_expert[l]], axis=1),
                           ((0, 0), (0, ROUTER_LANES - N_GROUPS - N_EXPERTS)))
        b_router = jnp.pad(jnp.concatenate([moe_b_group[l], moe_b_expert[l]]),
                           (0, ROUTER_LANES - N_GROUPS - N_EXPERTS))[None, :]
        tile_off = 0 if with_ctx else CTX_LEN // ROW_TILE
        x_mid, h2, logits = _merge(
            x_all, h, ya, yb, yc, mods, w_branch_gate[l].astype(BF16), b_branch_gate[l][None, :],
            w_branch_a[l].astype(BF16), w_branch_b[l].astype(BF16), w_branch_c[l].astype(BF16),
            w_out[l].astype(BF16), norm2_w[l][None, :], w_router, b_router, tile_off)
        Lm = x_mid.shape[1]
        f = _moe(h2.reshape(B * Lm, D), logits.reshape(B * Lm, ROUTER_LANES), p).reshape(B, Lm, D)
        if with_ctx:
            g2 = jnp.concatenate([jnp.broadcast_to(mods[:, 0:1, 5, :], (B, CTX_LEN, D)),
                                  jnp.broadcast_to(mods[:, 1:2, 5, :], (B, L - CTX_LEN, D))], axis=1)
            x_all = x_mid + g2 * f
        else:
            x_lat = x_mid + mods[:, 1:2, 5, :] * f
    return _final_norm(x_lat.reshape(B * n_lat, D), final_norm_w[None, :]).reshape(B, n_lat, D)
```

```python
import functools

import jax
import jax.numpy as jnp
from jax import lax
from jax.experimental import pallas as pl
from jax.experimental.pallas import tpu as pltpu

D_MODEL = 1024
CTX_LEN = 256
GRID_W = 64
N_MOD = 6
RMS_EPS = 1e-6
ROPE_BASE = 10000.0
HEAD_DIM = 64
F32 = jnp.float32
BF16 = jnp.bfloat16

RWKV_HEADS = 4
RWKV_DIM = RWKV_HEADS * HEAD_DIM
DECAY_LORA = 64
AAA_LORA = 64
GATE_LORA = 128
RWKV_GN_EPS = 64e-5

MLA_HEADS = 8
MLA_NOPE = 64
MLA_ROPE = 32
MLA_QK = MLA_NOPE + MLA_ROPE
MLA_V = 64
MLA_Q_RANK = 256
MLA_KV_RANK = 128
MLA_DIM = MLA_HEADS * MLA_V

RET_HEADS = 4
RET_KEY = 64
RET_VAL = 64
RET_DIM = RET_HEADS * RET_VAL

N_GROUPS = 4
EXPERTS_PER_GROUP = 8
N_EXPERTS = N_GROUPS * EXPERTS_PER_GROUP
TOP_K = 2
EXPERT_HIDDEN = 512

U_RWKV = 3 * RWKV_DIM + 2 * DECAY_LORA + 2 * AAA_LORA + GATE_LORA
U_MLA_RAW = MLA_Q_RANK + MLA_KV_RANK + MLA_ROPE
U_MLA = 512
U_RET = 4 * RET_DIM

ROW_TILE = 256
RWKV_CHUNK = 64
RET_CHUNK = 256
ROUTER_LANES = 128
MOE_TILE = 256
VMEM_LIMIT = 56 * 1024 * 1024


def _dot(a, b):
    return jnp.dot(a, b, preferred_element_type=F32)


def _dot_nt(a, b):
    return lax.dot_general(a, b, (((1,), (1,)), ((), ())), preferred_element_type=F32)


def _inproj_kernel(x_ref, mod_ref, nw_ref, wa_ref, wb_ref, wc_ref, h_ref, ua_ref, ub_ref, uc_ref):
    x = x_ref[0]
    y = x * lax.rsqrt(jnp.mean(x * x, axis=-1, keepdims=True) + RMS_EPS) * nw_ref[...]
    h = y * (1.0 + mod_ref[0, 0, 1:2, :]) + mod_ref[0, 0, 0:1, :]
    hb = h.astype(BF16)
    h_ref[0] = hb
    ua_ref[0] = _dot(hb, wa_ref[...])
    ub_ref[0] = _dot(hb, wb_ref[...])
    uc_ref[0] = _dot(hb, wc_ref[...])


def _inproj(x_all, mods, norm_w, w_a, w_b, w_c):
    B, T, D = x_all.shape
    nt = T // ROW_TILE
    tok = lambda w: pl.BlockSpec((1, ROW_TILE, w), lambda b, j: (b, j, 0))
    full = lambda a: pl.BlockSpec(a.shape, lambda b, j: (0,) * a.ndim)
    return pl.pallas_call(
        _inproj_kernel,
        grid=(B, nt),
        in_specs=[tok(D),
                  pl.BlockSpec((1, 1, 8, D), lambda b, j: (b, jnp.minimum(j, 1), 0, 0)),
                  full(norm_w), full(w_a), full(w_b), full(w_c)],
        out_specs=[tok(D), tok(U_RWKV), tok(U_MLA), tok(U_RET)],
        out_shape=[jax.ShapeDtypeStruct((B, T, D), BF16),
                   jax.ShapeDtypeStruct((B, T, U_RWKV), F32),
                   jax.ShapeDtypeStruct((B, T, U_MLA), F32),
                   jax.ShapeDtypeStruct((B, T, U_RET), F32)],
        compiler_params=pltpu.CompilerParams(
            dimension_semantics=("parallel", "parallel"), vmem_limit_bytes=VMEM_LIMIT),
        name="inproj",
    )(x_all, mods, norm_w, w_a, w_b, w_c)


def _dot_hi(a, b):
    return jnp.dot(a, b, preferred_element_type=F32, precision=lax.Precision.HIGHEST)


def _softplus(z):
    return jnp.maximum(z, 0.0) + jnp.log(1.0 + jnp.exp(-jnp.abs(z)))


def _rwkv_prep_kernel(u_ref, up_ref, un_ref, conv_ref, kk_ref, ka_ref, w0_ref, a0_ref, wup_ref, aup_ref,
                      gup_ref, ones_ref, tril_ref, triu_ref,
                      r_ref, k_ref, v_ref, vT_ref, kap_ref, g_ref,
                      w_ref, beta_ref, kaph_ref, kch_ref, bch_ref, rh_ref, kend_ref):
    R = RWKV_DIM
    j = pl.program_id(1)
    u = u_ref[0]
    x = u[:, :3 * R]
    prev_row = jnp.where(j >= 2, up_ref[0, 7:8, :3 * R], 0.0)
    next_row = jnp.where((j >= 1) & (j < pl.num_programs(1) - 1), un_ref[0, 0:1, :3 * R], 0.0)
    rid = lax.broadcasted_iota(jnp.int32, (ROW_TILE, 3 * R), 0)
    x_prev = jnp.where(rid == 0, prev_row, pltpu.roll(x, 1, 0))
    x_next = jnp.where(rid == ROW_TILE - 1, next_row, pltpu.roll(x, ROW_TILE - 1, 0))
    rkv = x_prev * conv_ref[0:1, :] + x * conv_ref[1:2, :] + x_next * conv_ref[2:3, :]
    r, k, v = rkv[:, :R], rkv[:, R:2 * R], rkv[:, 2 * R:]
    ones_blk = ones_ref[...]
    kk = k * kk_ref[...]
    kap = kk / jnp.maximum(jnp.sqrt(_dot_hi(kk * kk, ones_blk)), 1e-12)
    r_ref[0] = r
    k_ref[0] = k
    v_ref[0] = v
    vT_ref[0] = v.T
    kap_ref[0] = kap
    lo = 3 * R
    tw = jnp.tanh(u[:, lo:lo + 2 * DECAY_LORA]).astype(BF16)
    ad = u[:, lo + 2 * DECAY_LORA:lo + 2 * DECAY_LORA + 2 * AAA_LORA].astype(BF16)
    g_ref[0] = _dot(jax.nn.sigmoid(u[:, lo + 2 * DECAY_LORA + 2 * AAA_LORA:]).astype(BF16), gup_ref[...])
    for d in range(2):
        log_w = -_softplus(-(w0_ref[d:d + 1, :] + _dot(tw, wup_ref[d]))) - 0.5
        logw = -jnp.exp(log_w)
        a = jax.nn.sigmoid(a0_ref[d:d + 1, :] + _dot(ad, aup_ref[d]))
        kmod = k * (1.0 + (a - 1.0) * ka_ref[...])
        beta = kap * a
        cs = _dot_hi(tril_ref[...] if d == 0 else triu_ref[...], logw)
        tot = _dot_hi(ones_blk, logw)
        inv = jnp.exp(-cs)
        w_ref[d, 0] = jnp.exp(logw)
        beta_ref[d, 0] = beta
        kaph_ref[d, 0] = kap * jnp.exp(cs - logw)
        kch_ref[d, 0] = kmod * inv
        bch_ref[d, 0] = beta * inv
        rh_ref[d, 0] = r * jnp.exp(cs)
        kend_ref[d, 0] = kmod * jnp.exp(tot - cs)


def _rwkv_prep(u_rwkv, p):
    B, L, _ = u_rwkv.shape
    R = RWKV_DIM
    nt = L // ROW_TILE
    halo = ROW_TILE // 8
    n_halo = L // 8
    idx = jnp.arange(ROW_TILE)
    same_chunk = (idx[:, None] // RWKV_CHUNK) == (idx[None, :] // RWKV_CHUNK)
    ones_blk = same_chunk.astype(F32)
    tril = (same_chunk & (idx[None, :] <= idx[:, None])).astype(F32)
    triu = (same_chunk & (idx[None, :] >= idx[:, None])).astype(F32)
    pad_rows = lambda w, d, n: jnp.pad(w, ((d * n, n - d * n), (0, 0)))
    wup = jnp.stack([pad_rows(p['rwkv_w_up'][d], d, DECAY_LORA) for d in range(2)]).astype(BF16)
    aup = jnp.stack([pad_rows(p['rwkv_a_up'][d], d, AAA_LORA) for d in range(2)]).astype(BF16)
    consts = [p['rwkv_conv'], p['rwkv_k_k'][None, :], p['rwkv_k_a'][None, :], p['rwkv_w0'], p['rwkv_a0'],
              wup, aup, p['rwkv_g_up'].astype(BF16), ones_blk, tril, triu]
    full = lambda a: pl.BlockSpec(a.shape, lambda b, j: (0,) * a.ndim)
    tok = pl.BlockSpec((1, ROW_TILE, R), lambda b, j: (b, j, 0))
    per_dir = pl.BlockSpec((2, 1, ROW_TILE, R), lambda b, j: (0, b, j, 0))
    tok_shape = jax.ShapeDtypeStruct((B, L, R), F32)
    dir_shape = jax.ShapeDtypeStruct((2, B, L, R), F32)
    return pl.pallas_call(
        _rwkv_prep_kernel,
        grid=(B, nt),
        in_specs=[pl.BlockSpec((1, ROW_TILE, U_RWKV), lambda b, j: (b, j, 0)),
                  pl.BlockSpec((1, 8, U_RWKV), lambda b, j: (b, jnp.maximum(j * halo - 1, 0), 0)),
                  pl.BlockSpec((1, 8, U_RWKV), lambda b, j: (b, jnp.minimum((j + 1) * halo, n_halo - 1), 0))]
                 + [full(a) for a in consts],
        out_specs=[tok, tok, tok, pl.BlockSpec((1, R, ROW_TILE), lambda b, j: (b, 0, j)), tok, tok]
                  + [per_dir] * 7,
        out_shape=[tok_shape, tok_shape, tok_shape, jax.ShapeDtypeStruct((B, R, L), F32), tok_shape, tok_shape]
                  + [dir_shape] * 7,
        compiler_params=pltpu.CompilerParams(
            dimension_semantics=("parallel", "parallel"), vmem_limit_bytes=VMEM_LIMIT),
        name="rwkv_prep",
    )(u_rwkv, u_rwkv, u_rwkv, *consts)


RWKV_BATCH = 2
RWKV_ROWS = 2 * RWKV_CHUNK
SUBLANES = 8


def _rwkv_scan_kernel(*refs):
    T = RWKV_CHUNK
    ins, outs = refs[:20], refs[20:22]
    p_ref, seq_ref, mix_ref, y0_ref, cm_ref = refs[22:]
    chains = [(bi, d, h) for bi in range(RWKV_BATCH) for d in range(2) for h in range(RWKV_HEADS)]

    @pl.when(pl.program_id(1) == 0)
    def _():
        p_ref[...] = jnp.zeros_like(p_ref)

    row_t = lax.broadcasted_iota(jnp.int32, (T, T), 0)
    col_j = lax.broadcasted_iota(jnp.int32, (T, T), 1)
    lane = lax.broadcasted_iota(jnp.int32, (SUBLANES, T), 1)
    before = (col_j < row_t, col_j > row_t)
    upto = (col_j <= row_t, col_j >= row_t)

    for slot in range(2):
        for ch, (bi, d, h) in enumerate(chains):
            kap_r, vT_r, v_r, w_r, beta_r, kaph_r, kch_r, bch_r, rh_r, kend_r = ins[10 * d:10 * d + 10]
            q = slot if d == 0 else 1 - slot
            rows, cols = slice(T * q, T * q + T), slice(HEAD_DIM * h, HEAD_DIM * h + HEAD_DIM)
            kch, rh = kch_r[0, bi, rows, cols], rh_r[0, bi, rows, cols]
            b_m = jnp.where(before[d], _dot_nt(kaph_r[0, bi, rows, cols], kch), 0.0)
            e_m = jnp.where(upto[d], _dot_nt(rh, kch), 0.0)
            cm_ref[ch] = jnp.where(upto[d], _dot_nt(rh, bch_r[0, bi, rows, cols]), 0.0)
            mix_ref[ch] = _dot_nt(vT_r[bi, cols, rows], b_m)
            y0_ref[ch] = _dot_nt(rh, p_ref[ch]) + _dot(e_m, v_r[bi, rows, cols])
            seq_ref[0, ch] = kap_r[bi, rows, cols]
            seq_ref[1, ch] = w_r[0, bi, rows, cols]
            seq_ref[2, ch] = beta_r[0, bi, rows, cols]

        def step(i, carry):
            for ch, (bi, d, h) in enumerate(chains):
                t = i if d == 0 else T - 1 - i
                at_t = lane == t
                kap_t, w_t, beta_t = (jnp.broadcast_to(seq_ref[op, ch, pl.ds(t, 1), :], (SUBLANES, T))
                                      for op in range(3))
                for s in range(HEAD_DIM // SUBLANES):
                    sub = slice(SUBLANES * s, SUBLANES * s + SUBLANES)
                    p, mix = p_ref[ch, sub, :], mix_ref[ch, sub, :]
                    sa = jnp.sum(p * kap_t + jnp.where(at_t, mix, 0.0), axis=-1, keepdims=True)
                    p_ref[ch, sub, :] = p * w_t - sa * beta_t
                    mix_ref[ch, sub, :] = jnp.where(at_t, sa, mix)
            return carry

        lax.fori_loop(0, T, step, 0)

        for bi in range(RWKV_BATCH):
            for d in range(2):
                vT_r, kend_r = ins[10 * d + 1], ins[10 * d + 9]
                q = slot if d == 0 else 1 - slot
                rows = slice(T * q, T * q + T)
                ys = []
                for h in range(RWKV_HEADS):
                    ch = (bi * 2 + d) * RWKV_HEADS + h
                    cols = slice(HEAD_DIM * h, HEAD_DIM * h + HEAD_DIM)
                    ys.append(y0_ref[ch] - _dot_nt(cm_ref[ch], mix_ref[ch]))
                    p_ref[ch] = p_ref[ch] + _dot(vT_r[bi, cols, rows], kend_r[0, bi, rows, cols])
                outs[d][bi, rows, :] = jnp.concatenate(ys, axis=-1)


def _rwkv_scan(kap, vT, v, per_dir):
    B, L, R = kap.shape
    nb = L // RWKV_ROWS
    n_ctx = CTX_LEN // RWKV_ROWS
    fwd = lambda c: c
    bwd = lambda c: jnp.where(c < n_ctx, n_ctx - 1 - c, nb + n_ctx - 1 - c)
    in_specs, args = [], []
    for d, blk in enumerate((fwd, bwd)):
        tok = pl.BlockSpec((RWKV_BATCH, RWKV_ROWS, R), lambda b, c, blk=blk: (b, blk(c), 0))
        tr = pl.BlockSpec((RWKV_BATCH, R, RWKV_ROWS), lambda b, c, blk=blk: (b, 0, blk(c)))
        dirs = pl.BlockSpec((1, RWKV_BATCH, RWKV_ROWS, R), lambda b, c, blk=blk, d=d: (d, b, blk(c), 0))
        in_specs += [tok, tr, tok] + [dirs] * 7
        args += [kap, vT, v] + list(per_dir)
    out_specs = [pl.BlockSpec((RWKV_BATCH, RWKV_ROWS, R), lambda b, c, blk=blk: (b, blk(c), 0))
                 for blk in (fwd, bwd)]
    n_chain = RWKV_BATCH * 2 * RWKV_HEADS
    tile = (n_chain, HEAD_DIM, RWKV_CHUNK)
    return pl.pallas_call(
        _rwkv_scan_kernel,
        grid=(B // RWKV_BATCH, nb),
        in_specs=in_specs,
        out_specs=out_specs,
        out_shape=[jax.ShapeDtypeStruct((B, L, R), F32)] * 2,
        scratch_shapes=[pltpu.VMEM(tile, F32), pltpu.VMEM((3,) + tile, F32), pltpu.VMEM(tile, F32),
                        pltpu.VMEM(tile, F32), pltpu.VMEM(tile, F32)],
        compiler_params=pltpu.CompilerParams(
            dimension_semantics=("parallel", "arbitrary"), vmem_limit_bytes=VMEM_LIMIT),
        name="rwkv_scan",
    )(*args)


def _rwkv_out_kernel(yf_ref, yb_ref, r_ref, k_ref, v_ref, g_ref, rk_ref, lnw_ref, lnb_ref, ones_ref, o_ref):
    ones_blk = ones_ref[...]
    y = yf_ref[0] + yb_ref[0]
    mu = _dot_hi(y, ones_blk) * (1.0 / HEAD_DIM)
    yc = y - mu
    var = _dot_hi(yc * yc, ones_blk) * (1.0 / HEAD_DIM)
    yn = yc * lax.rsqrt(var + RWKV_GN_EPS) * lnw_ref[...] + lnb_ref[...]
    bonus = _dot_hi(r_ref[0] * k_ref[0] * rk_ref[...], ones_blk) * v_ref[0]
    o_ref[0] = ((yn + bonus) * g_ref[0]).astype(o_ref.dtype)


def _rwkv_out(y_f, y_b, r, k, v, g, p):
    B, L, R = r.shape
    idx = jnp.arange(R)
    ones_blk = ((idx[:, None] // HEAD_DIM) == (idx[None, :] // HEAD_DIM)).astype(F32)
    consts = [p['rwkv_r_k'].reshape(1, R), p['rwkv_ln_w'][None, :], p['rwkv_ln_b'][None, :], ones_blk]
    tok = pl.BlockSpec((1, ROW_TILE, R), lambda b, j: (b, j, 0))
    return pl.pallas_call(
        _rwkv_out_kernel,
        grid=(B, L // ROW_TILE),
        in_specs=[tok] * 6 + [pl.BlockSpec(a.shape, lambda b, j: (0, 0)) for a in consts],
        out_specs=tok,
        out_shape=jax.ShapeDtypeStruct((B, L, R), BF16),
        compiler_params=pltpu.CompilerParams(dimension_semantics=("parallel", "parallel")),
        name="rwkv_out",
    )(y_f, y_b, r, k, v, g, *consts)


def _attn_kernel(q_ref, k_ref, v_ref, o_ref, *, scale):
    outs = []
    for h in range(q_ref.shape[1]):
        s = _dot_nt(q_ref[0, h], k_ref[0, h]) * scale
        p = jnp.exp(s - jnp.max(s, axis=-1, keepdims=True))
        l = jnp.sum(p, axis=-1, keepdims=True)
        outs.append(_dot(p.astype(BF16), v_ref[0, h]) / l)
    o_ref[0] = jnp.concatenate(outs, axis=-1).astype(o_ref.dtype)


def _attention(q, k, v):
    B, H, Lq, dq = q.shape
    Lk, dv = k.shape[2], v.shape[3]
    return pl.pallas_call(
        functools.partial(_attn_kernel, scale=dq ** -0.5),
        grid=(B, Lq // ROW_TILE),
        in_specs=[pl.BlockSpec((1, H, ROW_TILE, dq), lambda b, i: (b, 0, i, 0)),
                  pl.BlockSpec((1, H, Lk, dq), lambda b, i: (b, 0, 0, 0)),
                  pl.BlockSpec((1, H, Lk, dv), lambda b, i: (b, 0, 0, 0))],
        out_specs=pl.BlockSpec((1, ROW_TILE, H * dv), lambda b, i: (b, i, 0)),
        out_shape=jax.ShapeDtypeStruct((B, Lq, H * dv), BF16),
        compiler_params=pltpu.CompilerParams(
            dimension_semantics=("parallel", "parallel"), vmem_limit_bytes=VMEM_LIMIT),
        name="mla_attention",
    )(q, k, v)


def _rope_rows(x, cos, sin_signed):
    lane = lax.broadcasted_iota(jnp.int32, x.shape, 1)
    partner = jnp.where(lane % RET_KEY < RET_KEY // 2,
                        pltpu.roll(x, x.shape[1] - RET_KEY // 2, 1), pltpu.roll(x, RET_KEY // 2, 1))
    return x * cos + partner * sin_signed


def _ret_kernel(*refs):
    ins, (inner_ref, cross_ref, tailT_ref, cdec_ref), outs, r_ref = refs[:10], refs[10:14], refs[14:16], refs[16]

    @pl.when(pl.program_id(1) == 0)
    def _():
        r_ref[...] = jnp.zeros_like(r_ref)

    for d in range(2):
        q_ref, k_ref, v_ref, cos_ref, sin_ref = ins[5 * d:5 * d + 5]
        cos, sin = cos_ref[...], sin_ref[...]
        q_all = _rope_rows(q_ref[0], cos, sin)
        k_all = _rope_rows(k_ref[0], cos, sin) * (RET_KEY ** -0.5)
        kT_all = k_all.T
        v_all = v_ref[0]
        os = []
        for h in range(RET_HEADS):
            cols = slice(RET_KEY * h, RET_KEY * h + RET_KEY)
            q = q_all[:, cols].astype(BF16)
            v = v_all[:, cols].astype(BF16)
            state = r_ref[d, h]
            s = _dot_nt(q, k_all[:, cols].astype(BF16)) * inner_ref[d, h]
            os.append(_dot(s.astype(BF16), v) + _dot(q, state.astype(BF16)) * cross_ref[d, h])
            r_ref[d, h] = state * cdec_ref[d, h] + _dot((kT_all[cols, :] * tailT_ref[d, h]).astype(BF16), v)
        outs[d][0] = jnp.concatenate(os, axis=-1)


def _retention_scan(u_ret, cos, sin_signed, inner, cross, tailT, cdec):
    B, L, _ = u_ret.shape
    C = RET_CHUNK
    nb = L // C
    n_ctx = CTX_LEN // C
    fwd = lambda c: c
    bwd = lambda c: jnp.where(c < n_ctx, n_ctx - 1 - c, nb + n_ctx - 1 - c)
    in_specs, args = [], []
    for blk in (fwd, bwd):
        in_specs += [pl.BlockSpec((1, C, RET_DIM), lambda b, c, blk=blk, i=i: (b, blk(c), i)) for i in range(3)]
        in_specs += [pl.BlockSpec((C, RET_DIM), lambda b, c, blk=blk: (blk(c), 0))] * 2
        args += [u_ret, u_ret, u_ret, cos, sin_signed]
    consts = [inner, cross, tailT, cdec]
    in_specs += [pl.BlockSpec(a.shape, lambda b, c: (0, 0, 0, 0)) for a in consts]
    return pl.pallas_call(
        _ret_kernel,
        grid=(B, nb),
        in_specs=in_specs,
        out_specs=[pl.BlockSpec((1, C, RET_DIM), lambda b, c, blk=blk: (b, blk(c), 0)) for blk in (fwd, bwd)],
        out_shape=[jax.ShapeDtypeStruct((B, L, RET_DIM), F32)] * 2,
        scratch_shapes=[pltpu.VMEM((2, RET_HEADS, RET_KEY, RET_VAL), F32)],
        compiler_params=pltpu.CompilerParams(
            dimension_semantics=("parallel", "arbitrary"), vmem_limit_bytes=VMEM_LIMIT),
        name="retention_scan",
    )(*args, *consts)


def _ret_out_kernel(of_ref, ob_ref, g_ref, ones_ref, o_ref):
    y = of_ref[0] + ob_ref[0]
    ms = _dot_hi(y * y, ones_ref[...]) * (1.0 / RET_VAL)
    o_ref[0] = (jax.nn.silu(g_ref[0]) * (y * lax.rsqrt(ms + RMS_EPS))).astype(o_ref.dtype)


def _ret_out(o_f, o_b, u_ret):
    B, L, R = o_f.shape
    idx = jnp.arange(R)
    ones_blk = ((idx[:, None] // RET_VAL) == (idx[None, :] // RET_VAL)).astype(F32)
    tok = pl.BlockSpec((1, ROW_TILE, R), lambda b, j: (b, j, 0))
    return pl.pallas_call(
        _ret_out_kernel,
        grid=(B, L // ROW_TILE),
        in_specs=[tok, tok, pl.BlockSpec((1, ROW_TILE, R), lambda b, j: (b, j, 3)),
                  pl.BlockSpec((R, R), lambda b, j: (0, 0))],
        out_specs=tok,
        out_shape=jax.ShapeDtypeStruct((B, L, R), BF16),
        compiler_params=pltpu.CompilerParams(dimension_semantics=("parallel", "parallel")),
        name="ret_out",
    )(o_f, o_b, u_ret, ones_blk)


def _merge_kernel(x_ref, h_ref, ya_ref, yb_ref, yc_ref, mod_ref, wg_ref, bg_ref, wa_ref, wb_ref, wc_ref,
                  wo_ref, n2_ref, wr_ref, br_ref, xo_ref, h2_ref, lg_ref):
    D = D_MODEL
    gates = jax.nn.sigmoid(_dot(h_ref[0], wg_ref[...]) + bg_ref[...])
    m = (gates[:, :D] * _dot(ya_ref[0], wa_ref[...])
         + gates[:, D:2 * D] * _dot(yb_ref[0], wb_ref[...])
         + gates[:, 2 * D:] * _dot(yc_ref[0], wc_ref[...]))
    xn = x_ref[0] + mod_ref[0, 0, 2:3, :] * _dot(m.astype(BF16), wo_ref[...])
    xo_ref[0] = xn
    y = xn * lax.rsqrt(jnp.mean(xn * xn, axis=-1, keepdims=True) + RMS_EPS) * n2_ref[...]
    h2 = y * (1.0 + mod_ref[0, 0, 4:5, :]) + mod_ref[0, 0, 3:4, :]
    h2_ref[0] = h2.astype(BF16)
    lg_ref[0] = jnp.dot(h2, wr_ref[...], preferred_element_type=F32,
                        precision=lax.Precision.HIGHEST) + br_ref[...]


def _merge(x_all, h, ya, yb, yc, mods, wg, bg, wa, wb, wc, wo, n2, wr, br, tile_off):
    B, T, D = x_all.shape
    nt = T // ROW_TILE - tile_off
    tok = lambda w: pl.BlockSpec((1, ROW_TILE, w), lambda b, j: (b, j + tile_off, 0))
    out = lambda w: pl.BlockSpec((1, ROW_TILE, w), lambda b, j: (b, j, 0))
    full = lambda a: pl.BlockSpec(a.shape, lambda b, j: (0,) * a.ndim)
    return pl.pallas_call(
        _merge_kernel,
        grid=(B, nt),
        in_specs=[tok(D), tok(D), tok(RWKV_DIM), tok(MLA_DIM), tok(RET_DIM),
                  pl.BlockSpec((1, 1, 8, D), lambda b, j: (b, jnp.minimum(j + tile_off, 1), 0, 0)),
                  full(wg), full(bg), full(wa), full(wb), full(wc), full(wo), full(n2), full(wr), full(br)],
        out_specs=[out(D), out(D), out(ROUTER_LANES)],
        out_shape=[jax.ShapeDtypeStruct((B, nt * ROW_TILE, D), F32),
                   jax.ShapeDtypeStruct((B, nt * ROW_TILE, D), BF16),
                   jax.ShapeDtypeStruct((B, nt * ROW_TILE, ROUTER_LANES), F32)],
        compiler_params=pltpu.CompilerParams(
            dimension_semantics=("parallel", "parallel"), vmem_limit_bytes=VMEM_LIMIT),
        name="merge",
    )(x_all, h, ya, yb, yc, mods, wg, bg, wa, wb, wc, wo, n2, wr, br)


def _moe_kernel(te_ref, x_ref, cw_ref, wg_ref, wu_ref, wd_ref, o_ref):
    del te_ref
    x = x_ref[...]
    hid = jax.nn.silu(_dot(x, wg_ref[0])) * _dot(x, wu_ref[0])
    o_ref[...] = _dot(hid.astype(BF16), wd_ref[0]) * cw_ref[...]


def _moe_experts(tile_expert, x_sorted, cw_sorted, w_gate, w_up, w_down):
    n_rows, D = x_sorted.shape
    n_tiles = n_rows // MOE_TILE
    grid_spec = pltpu.PrefetchScalarGridSpec(
        num_scalar_prefetch=1,
        grid=(n_tiles,),
        in_specs=[pl.BlockSpec((MOE_TILE, D), lambda i, te: (i, 0)),
                  pl.BlockSpec((MOE_TILE, 1), lambda i, te: (i, 0)),
                  pl.BlockSpec((1, D, EXPERT_HIDDEN), lambda i, te: (te[i], 0, 0)),
                  pl.BlockSpec((1, D, EXPERT_HIDDEN), lambda i, te: (te[i], 0, 0)),
                  pl.BlockSpec((1, EXPERT_HIDDEN, D), lambda i, te: (te[i], 0, 0))],
        out_specs=pl.BlockSpec((MOE_TILE, D), lambda i, te: (i, 0)),
    )
    return pl.pallas_call(
        _moe_kernel,
        grid_spec=grid_spec,
        out_shape=jax.ShapeDtypeStruct((n_rows, D), F32),
        compiler_params=pltpu.CompilerParams(
            dimension_semantics=("arbitrary",), vmem_limit_bytes=VMEM_LIMIT),
        name="moe_experts",
    )(tile_expert, x_sorted, cw_sorted, w_gate, w_up, w_down)


def _final_norm_kernel(x_ref, w_ref, o_ref):
    x = x_ref[...]
    o_ref[...] = x * lax.rsqrt(jnp.mean(x * x, axis=-1, keepdims=True) + RMS_EPS) * w_ref[...]


def _final_norm(x, w):
    n_rows, D = x.shape
    tile = 512
    return pl.pallas_call(
        _final_norm_kernel,
        grid=(n_rows // tile,),
        in_specs=[pl.BlockSpec((tile, D), lambda i: (i, 0)), pl.BlockSpec((1, D), lambda i: (0, 0))],
        out_specs=pl.BlockSpec((tile, D), lambda i: (i, 0)),
        out_shape=jax.ShapeDtypeStruct((n_rows, D), F32),
        compiler_params=pltpu.CompilerParams(dimension_semantics=("parallel",)),
        name="final_norm",
    )(x, w)


def _rmsnorm(x, w):
    return x * lax.rsqrt(jnp.mean(x * x, axis=-1, keepdims=True) + RMS_EPS) * w


def _rope_table(n_tokens, rot_dim):
    rows = n_tokens // GRID_W
    row = jnp.repeat(jnp.arange(rows, dtype=F32), GRID_W)
    col = jnp.tile(jnp.arange(GRID_W, dtype=F32), rows)
    n_freq = rot_dim // 4
    inv_freq = ROPE_BASE ** (-jnp.arange(n_freq, dtype=F32) / n_freq)
    ang = jnp.concatenate([row[:, None] * inv_freq, col[:, None] * inv_freq], axis=-1)
    return jnp.cos(ang), jnp.sin(ang)


def _rope(x, table):
    cos, sin = table
    cos = cos[None, :, None, :]
    sin = sin[None, :, None, :]
    half = x.shape[-1] // 2
    x1, x2 = x[..., :half], x[..., half:]
    return jnp.concatenate([x1 * cos - x2 * sin, x1 * sin + x2 * cos], axis=-1)


def _rwkv_branch(u_rwkv, p):
    r, k, v, vT, kap, g, *per_dir = _rwkv_prep(u_rwkv, p)
    y_f, y_b = _rwkv_scan(kap, vT, v, per_dir)
    return _rwkv_out(y_f, y_b, r, k, v, g, p)


def _mla_segment(u, p, rope):
    B, Lx, _ = u.shape
    q_dn = u[..., :MLA_Q_RANK]
    kv_dn = u[..., MLA_Q_RANK:MLA_Q_RANK + MLA_KV_RANK]
    k_rope = u[..., MLA_Q_RANK + MLA_KV_RANK:U_MLA_RAW]
    q = (_rmsnorm(q_dn, p['mla_q_norm']) @ p['mla_w_uq']).reshape(B, Lx, MLA_HEADS, MLA_QK)
    kv = (_rmsnorm(kv_dn, p['mla_kv_norm']) @ p['mla_w_ukv']).reshape(B, Lx, MLA_HEADS, MLA_NOPE + MLA_V)
    q_r, k_r = q[..., MLA_NOPE:], k_rope[:, :, None, :]
    if rope is not None:
        q_r, k_r = _rope(q_r, rope), _rope(k_r, rope)
    q = jnp.concatenate([q[..., :MLA_NOPE], q_r], axis=-1)
    k = jnp.concatenate([kv[..., :MLA_NOPE], jnp.broadcast_to(k_r, (B, Lx, MLA_HEADS, MLA_ROPE))], axis=-1)
    hm = lambda t: jnp.transpose(t, (0, 2, 1, 3)).astype(BF16)
    return hm(q), hm(k), hm(kv[..., MLA_NOPE:])


def _mla_branch(u_mla, p, rope, with_ctx):
    q_c, k_c, v_c = _mla_segment(u_mla[:, :CTX_LEN], p, None)
    q_l, k_l, v_l = _mla_segment(u_mla[:, CTX_LEN:], p, rope)
    y_lat = _attention(q_l, jnp.concatenate([k_c, k_l], axis=2), jnp.concatenate([v_c, v_l], axis=2))
    if with_ctx:
        y_ctx = _attention(q_c, k_c, v_c)
    else:
        y_ctx = jnp.zeros((u_mla.shape[0], CTX_LEN, MLA_DIM), BF16)
    return jnp.concatenate([y_ctx, y_lat], axis=1)


def _retention_branch(u_ret, p, rope):
    B, L, _ = u_ret.shape
    C = RET_CHUNK
    cos, sin = rope
    n_ctx = L - cos.shape[0]
    cos = jnp.concatenate([jnp.ones((n_ctx, cos.shape[1]), F32), cos], axis=0)
    sin = jnp.concatenate([jnp.zeros((n_ctx, sin.shape[1]), F32), sin], axis=0)
    cos_rows = jnp.tile(cos, (1, 2 * RET_HEADS))
    sin_rows = jnp.tile(jnp.concatenate([-sin, sin], axis=1), (1, RET_HEADS))
    log_gamma = jax.nn.log_sigmoid(p['ret_decay'])
    idx = jnp.arange(C, dtype=F32)
    rel = idx[:, None] - idx[None, :]
    lg_f, lg_b = log_gamma[0][:, None, None], log_gamma[1][:, None, None]
    inner = jnp.stack([jnp.where(rel[None] >= 0, jnp.exp(lg_f * jnp.maximum(rel, 0.0)[None]), 0.0),
                       jnp.where(rel[None] <= 0, jnp.exp(lg_b * jnp.maximum(-rel, 0.0)[None]), 0.0)])
    cross = jnp.stack([jnp.exp(lg_f * (idx + 1.0)[None, :, None]), jnp.exp(lg_b * (C - idx)[None, :, None])])
    cross = jnp.broadcast_to(cross, (2, RET_HEADS, C, RET_VAL))
    tailT = jnp.stack([jnp.exp(lg_f * (C - 1.0 - idx)[None, None, :]), jnp.exp(lg_b * idx[None, None, :])])
    tailT = jnp.broadcast_to(tailT, (2, RET_HEADS, RET_KEY, C))
    cdec = jnp.broadcast_to(jnp.exp(log_gamma * C)[:, :, None, None], (2, RET_HEADS, RET_KEY, RET_VAL))
    o_f, o_b = _retention_scan(u_ret, cos_rows, sin_rows, inner, cross, tailT, cdec)
    return _ret_out(o_f, o_b, u_ret)


def _route(logits, p):
    n_tok = logits.shape[0]
    grp_prob = jax.nn.softmax(logits[:, :N_GROUPS], axis=-1)
    grp_p, grp_idx = lax.top_k(grp_prob, 1)
    exp_logits = logits[:, N_GROUPS:N_GROUPS + N_EXPERTS].reshape(n_tok, N_GROUPS, EXPERTS_PER_GROUP)
    in_group = jnp.take_along_axis(exp_logits, grp_idx[:, :, None], axis=1)[:, 0]
    exp_p, exp_idx = lax.top_k(jax.nn.softmax(in_group, axis=-1), TOP_K)
    weights = grp_p * exp_p / jnp.sum(exp_p, axis=-1, keepdims=True)
    return grp_idx * EXPERTS_PER_GROUP + exp_idx, weights


def _moe(h2, logits, p):
    n_tok, D = h2.shape
    expert_id, weights = _route(logits, p)
    n_assign = n_tok * TOP_K
    flat_e = expert_id.reshape(n_assign).astype(jnp.int32)
    flat_w = weights.reshape(n_assign)
    flat_tok = jnp.arange(n_assign, dtype=jnp.int32) // TOP_K
    counts = jnp.zeros((N_EXPERTS,), jnp.int32).at[flat_e].add(1)
    padded = ((counts + MOE_TILE - 1) // MOE_TILE) * MOE_TILE
    pad_start = jnp.cumsum(padded) - padded
    raw_start = jnp.cumsum(counts) - counts
    order = jnp.argsort(flat_e, stable=True)
    sorted_e = flat_e[order]
    pos_sorted = pad_start[sorted_e] + (jnp.arange(n_assign, dtype=jnp.int32) - raw_start[sorted_e])
    n_tiles = n_assign // MOE_TILE + N_EXPERTS
    n_rows = n_tiles * MOE_TILE
    src_tok = jnp.zeros((n_rows,), jnp.int32).at[pos_sorted].set(flat_tok[order])
    cw = jnp.zeros((n_rows,), F32).at[pos_sorted].set(flat_w[order])
    pos = jnp.zeros((n_assign,), jnp.int32).at[order].set(pos_sorted).reshape(n_tok, TOP_K)
    tile_start = jnp.arange(n_tiles, dtype=jnp.int32) * MOE_TILE
    tile_expert = jnp.minimum(
        jnp.sum(tile_start[:, None] >= jnp.cumsum(padded)[None, :], axis=1), N_EXPERTS - 1).astype(jnp.int32)
    out = _moe_experts(tile_expert, h2[src_tok], cw[:, None], p['moe_w_gate'], p['moe_w_up'], p['moe_w_down'])
    return out[pos[:, 0]] + out[pos[:, 1]]


def _permute_w_in(w_in):
    w_a = w_in[:, :U_RWKV]
    w_b = jnp.pad(w_in[:, U_RWKV:U_RWKV + U_MLA_RAW], ((0, 0), (0, U_MLA - U_MLA_RAW)))
    w_c = w_in[:, U_RWKV + U_MLA_RAW:]
    return w_a.astype(BF16), w_b.astype(BF16), w_c.astype(BF16)


def kernel(x, c, ctx, c_ctx, w_mod, b_mod, norm1_w, w_in, rwkv_conv, rwkv_w0, rwkv_w_up, rwkv_a0, rwkv_a_up, rwkv_g_up, rwkv_k_k, rwkv_k_a, rwkv_r_k, rwkv_ln_w, rwkv_ln_b, mla_q_norm, mla_w_uq, mla_kv_norm, mla_w_ukv, ret_decay, w_branch_a, w_branch_b, w_branch_c, w_branch_gate, b_branch_gate, w_out, norm2_w, moe_w_group, moe_b_group, moe_w_expert, moe_b_expert, moe_w_gate, moe_w_up, moe_w_down, final_norm_w):
    B, n_lat, D = x.shape
    depth = w_mod.shape[0]
    ropes = (_rope_table(n_lat, MLA_ROPE), _rope_table(n_lat, RET_KEY))
    x_all = jnp.concatenate([ctx, x], axis=1)
    L = x_all.shape[1]
    for l in range(depth):
        with_ctx = l < depth - 1
        p = {
            'rwkv_conv': rwkv_conv[l], 'rwkv_w0': rwkv_w0[l], 'rwkv_w_up': rwkv_w_up[l],
            'rwkv_a0': rwkv_a0[l], 'rwkv_a_up': rwkv_a_up[l], 'rwkv_g_up': rwkv_g_up[l],
            'rwkv_k_k': rwkv_k_k[l], 'rwkv_k_a': rwkv_k_a[l], 'rwkv_r_k': rwkv_r_k[l].reshape(RWKV_HEADS, HEAD_DIM),
            'rwkv_ln_w': rwkv_ln_w[l], 'rwkv_ln_b': rwkv_ln_b[l],
            'mla_q_norm': mla_q_norm[l], 'mla_w_uq': mla_w_uq[l], 'mla_kv_norm': mla_kv_norm[l],
            'mla_w_ukv': mla_w_ukv[l], 'ret_decay': ret_decay[l],
            'moe_w_gate': moe_w_gate[l].astype(BF16), 'moe_w_up': moe_w_up[l].astype(BF16),
            'moe_w_down': moe_w_down[l].astype(BF16),
        }
        mod_lat = jax.nn.silu(c) @ w_mod[l] + b_mod[l]
        mod_ctx = jax.nn.silu(c_ctx) @ w_mod[l] + b_mod[l]
        mods = jnp.stack([jnp.broadcast_to(mod_ctx, mod_lat.shape), mod_lat], axis=1).reshape(B, 2, N_MOD, D)
        mods = jnp.pad(mods, ((0, 0), (0, 0), (0, 8 - N_MOD), (0, 0)))

        w_a, w_b, w_c = _permute_w_in(w_in[l])
        h, u_rwkv, u_mla, u_ret = _inproj(x_all, mods, norm1_w[l][None, :], w_a, w_b, w_c)

        ya = _rwkv_branch(u_rwkv, p)
        yb = _mla_branch(u_mla, p, ropes[0], with_ctx)
        yc = _retention_branch(u_ret, p, ropes[1])

        w_router = jnp.pad(jnp.concatenate([moe_w_group[l], moe_w_expert[l]], axis=1),
                           ((0, 0), (0, ROUTER_LANES - N_GROUPS - N_EXPERTS)))
        b_router = jnp.pad(jnp.concatenate([moe_b_group[l], moe_b_expert[l]]),
                           (0, ROUTER_LANES - N_GROUPS - N_EXPERTS))[None, :]
        tile_off = 0 if with_ctx else CTX_LEN // ROW_TILE
        x_mid, h2, logits = _merge(
            x_all, h, ya, yb, yc, mods, w_branch_gate[l].astype(BF16), b_branch_gate[l][None, :],
            w_branch_a[l].astype(BF16), w_branch_b[l].astype(BF16), w_branch_c[l].astype(BF16),
            w_out[l].astype(BF16), norm2_w[l][None, :], w_router, b_router, tile_off)
        Lm = x_mid.shape[1]
        f = _moe(h2.reshape(B * Lm, D), logits.reshape(B * Lm, ROUTER_LANES), p).reshape(B, Lm, D)
        if with_ctx:
            g2 = jnp.concatenate([jnp.broadcast_to(mods[:, 0:1, 5, :], (B, CTX_LEN, D)),
                                  jnp.broadcast_to(mods[:, 1:2, 5, :], (B, L - CTX_LEN, D))], axis=1)
            x_all = x_mid + g2 * f
        else:
            x_lat = x_mid + mods[:, 1:2, 5, :] * f
    return _final_norm(x_lat.reshape(B * n_lat, D), final_norm_w[None, :]).reshape(B, n_lat, D)
```

```python
import functools

import jax
import jax.numpy as jnp
from jax import lax
from jax.experimental import pallas as pl
from jax.experimental.pallas import tpu as pltpu

D_MODEL = 1024
CTX_LEN = 256
GRID_W = 64
N_MOD = 6
RMS_EPS = 1e-6
ROPE_BASE = 10000.0
HEAD_DIM = 64
F32 = jnp.float32
BF16 = jnp.bfloat16

RWKV_HEADS = 4
RWKV_DIM = RWKV_HEADS * HEAD_DIM
DECAY_LORA = 64
AAA_LORA = 64
GATE_LORA = 128
RWKV_GN_EPS = 64e-5

MLA_HEADS = 8
MLA_NOPE = 64
MLA_ROPE = 32
MLA_QK = MLA_NOPE + MLA_ROPE
MLA_V = 64
MLA_Q_RANK = 256
MLA_KV_RANK = 128
MLA_DIM = MLA_HEADS * MLA_V

RET_HEADS = 4
RET_KEY = 64
RET_VAL = 64
RET_DIM = RET_HEADS * RET_VAL

N_GROUPS = 4
EXPERTS_PER_GROUP = 8
N_EXPERTS = N_GROUPS * EXPERTS_PER_GROUP
TOP_K = 2
EXPERT_HIDDEN = 512

U_RWKV = 3 * RWKV_DIM + 2 * DECAY_LORA + 2 * AAA_LORA + GATE_LORA
U_MLA_RAW = MLA_Q_RANK + MLA_KV_RANK + MLA_ROPE
U_MLA = 512
U_RET = 4 * RET_DIM

ROW_TILE = 256
RWKV_CHUNK = 64
RET_CHUNK = 256
ROUTER_LANES = 128
MOE_TILE = 256
VMEM_LIMIT = 56 * 1024 * 1024


def _dot(a, b):
    return jnp.dot(a, b, preferred_element_type=F32)


def _dot_nt(a, b):
    return lax.dot_general(a, b, (((1,), (1,)), ((), ())), preferred_element_type=F32)


def _inproj_kernel(x_ref, mod_ref, nw_ref, wa_ref, wb_ref, wc_ref, h_ref, ua_ref, ub_ref, uc_ref):
    x = x_ref[0]
    y = x * lax.rsqrt(jnp.mean(x * x, axis=-1, keepdims=True) + RMS_EPS) * nw_ref[...]
    h = y * (1.0 + mod_ref[0, 0, 1:2, :]) + mod_ref[0, 0, 0:1, :]
    hb = h.astype(BF16)
    h_ref[0] = hb
    ua_ref[0] = _dot(hb, wa_ref[...])
    ub_ref[0] = _dot(hb, wb_ref[...])
    uc_ref[0] = _dot(hb, wc_ref[...])


def _inproj(x_all, mods, norm_w, w_a, w_b, w_c):
    B, T, D = x_all.shape
    nt = T // ROW_TILE
    tok = lambda w: pl.BlockSpec((1, ROW_TILE, w), lambda b, j: (b, j, 0))
    full = lambda a: pl.BlockSpec(a.shape, lambda b, j: (0,) * a.ndim)
    return pl.pallas_call(
        _inproj_kernel,
        grid=(B, nt),
        in_specs=[tok(D),
                  pl.BlockSpec((1, 1, 8, D), lambda b, j: (b, jnp.minimum(j, 1), 0, 0)),
                  full(norm_w), full(w_a), full(w_b), full(w_c)],
        out_specs=[tok(D), tok(U_RWKV), tok(U_MLA), tok(U_RET)],
        out_shape=[jax.ShapeDtypeStruct((B, T, D), BF16),
                   jax.ShapeDtypeStruct((B, T, U_RWKV), F32),
                   jax.ShapeDtypeStruct((B, T, U_MLA), F32),
                   jax.ShapeDtypeStruct((B, T, U_RET), F32)],
        compiler_params=pltpu.CompilerParams(
            dimension_semantics=("parallel", "parallel"), vmem_limit_bytes=VMEM_LIMIT),
        name="inproj",
    )(x_all, mods, norm_w, w_a, w_b, w_c)


def _dot_hi(a, b):
    return jnp.dot(a, b, preferred_element_type=F32, precision=lax.Precision.HIGHEST)


def _softplus(z):
    return jnp.maximum(z, 0.0) + jnp.log(1.0 + jnp.exp(-jnp.abs(z)))


def _rwkv_prep_kernel(u_ref, up_ref, un_ref, conv_ref, kk_ref, ka_ref, w0_ref, a0_ref, wup_ref, aup_ref,
                      gup_ref, ones_ref, tril_ref, triu_ref,
                      r_ref, k_ref, v_ref, vT_ref, kap_ref, g_ref,
                      w_ref, beta_ref, kaph_ref, kch_ref, bch_ref, rh_ref, kend_ref):
    R = RWKV_DIM
    j = pl.program_id(1)
    u = u_ref[0]
    x = u[:, :3 * R]
    prev_row = jnp.where(j >= 2, up_ref[0, 7:8, :3 * R], 0.0)
    next_row = jnp.where((j >= 1) & (j < pl.num_programs(1) - 1), un_ref[0, 0:1, :3 * R], 0.0)
    rid = lax.broadcasted_iota(jnp.int32, (ROW_TILE, 3 * R), 0)
    x_prev = jnp.where(rid == 0, prev_row, pltpu.roll(x, 1, 0))
    x_next = jnp.where(rid == ROW_TILE - 1, next_row, pltpu.roll(x, ROW_TILE - 1, 0))
    rkv = x_prev * conv_ref[0:1, :] + x * conv_ref[1:2, :] + x_next * conv_ref[2:3, :]
    r, k, v = rkv[:, :R], rkv[:, R:2 * R], rkv[:, 2 * R:]
    ones_blk = ones_ref[...]
    kk = k * kk_ref[...]
    kap = kk / jnp.maximum(jnp.sqrt(_dot_hi(kk * kk, ones_blk)), 1e-12)
    r_ref[0] = r
    k_ref[0] = k
    v_ref[0] = v
    vT_ref[0] = v.T
    kap_ref[0] = kap
    lo = 3 * R
    tw = jnp.tanh(u[:, lo:lo + 2 * DECAY_LORA]).astype(BF16)
    ad = u[:, lo + 2 * DECAY_LORA:lo + 2 * DECAY_LORA + 2 * AAA_LORA].astype(BF16)
    g_ref[0] = _dot(jax.nn.sigmoid(u[:, lo + 2 * DECAY_LORA + 2 * AAA_LORA:]).astype(BF16), gup_ref[...])
    for d in range(2):
        log_w = -_softplus(-(w0_ref[d:d + 1, :] + _dot(tw, wup_ref[d]))) - 0.5
        logw = -jnp.exp(log_w)
        a = jax.nn.sigmoid(a0_ref[d:d + 1, :] + _dot(ad, aup_ref[d]))
        kmod = k * (1.0 + (a - 1.0) * ka_ref[...])
        beta = kap * a
        cs = _dot_hi(tril_ref[...] if d == 0 else triu_ref[...], logw)
        tot = _dot_hi(ones_blk, logw)
        inv = jnp.exp(-cs)
        w_ref[d, 0] = jnp.exp(logw)
        beta_ref[d, 0] = beta
        kaph_ref[d, 0] = kap * jnp.exp(cs - logw)
        kch_ref[d, 0] = kmod * inv
        bch_ref[d, 0] = beta * inv
        rh_ref[d, 0] = r * jnp.exp(cs)
        kend_ref[d, 0] = kmod * jnp.exp(tot - cs)


def _rwkv_prep(u_rwkv, p):
    B, L, _ = u_rwkv.shape
    R = RWKV_DIM
    nt = L // ROW_TILE
    halo = ROW_TILE // 8
    n_halo = L // 8
    idx = jnp.arange(ROW_TILE)
    same_chunk = (idx[:, None] // RWKV_CHUNK) == (idx[None, :] // RWKV_CHUNK)
    ones_blk = same_chunk.astype(F32)
    tril = (same_chunk & (idx[None, :] <= idx[:, None])).astype(F32)
    triu = (same_chunk & (idx[None, :] >= idx[:, None])).astype(F32)
    pad_rows = lambda w, d, n: jnp.pad(w, ((d * n, n - d * n), (0, 0)))
    wup = jnp.stack([pad_rows(p['rwkv_w_up'][d], d, DECAY_LORA) for d in range(2)]).astype(BF16)
    aup = jnp.stack([pad_rows(p['rwkv_a_up'][d], d, AAA_LORA) for d in range(2)]).astype(BF16)
    consts = [p['rwkv_conv'], p['rwkv_k_k'][None, :], p['rwkv_k_a'][None, :], p['rwkv_w0'], p['rwkv_a0'],
              wup, aup, p['rwkv_g_up'].astype(BF16), ones_blk, tril, triu]
    full = lambda a: pl.BlockSpec(a.shape, lambda b, j: (0,) * a.ndim)
    tok = pl.BlockSpec((1, ROW_TILE, R), lambda b, j: (b, j, 0))
    per_dir = pl.BlockSpec((2, 1, ROW_TILE, R), lambda b, j: (0, b, j, 0))
    tok_shape = jax.ShapeDtypeStruct((B, L, R), F32)
    dir_shape = jax.ShapeDtypeStruct((2, B, L, R), F32)
    return pl.pallas_call(
        _rwkv_prep_kernel,
        grid=(B, nt),
        in_specs=[pl.BlockSpec((1, ROW_TILE, U_RWKV), lambda b, j: (b, j, 0)),
                  pl.BlockSpec((1, 8, U_RWKV), lambda b, j: (b, jnp.maximum(j * halo - 1, 0), 0)),
                  pl.BlockSpec((1, 8, U_RWKV), lambda b, j: (b, jnp.minimum((j + 1) * halo, n_halo - 1), 0))]
                 + [full(a) for a in consts],
        out_specs=[tok, tok, tok, pl.BlockSpec((1, R, ROW_TILE), lambda b, j: (b, 0, j)), tok, tok]
                  + [per_dir] * 7,
        out_shape=[tok_shape, tok_shape, tok_shape, jax.ShapeDtypeStruct((B, R, L), F32), tok_shape, tok_shape]
                  + [dir_shape] * 7,
        compiler_params=pltpu.CompilerParams(
            dimension_semantics=("parallel", "parallel"), vmem_limit_bytes=VMEM_LIMIT),
        name="rwkv_prep",
    )(u_rwkv, u_rwkv, u_rwkv, *consts)


RWKV_BATCH = 2
RWKV_ROWS = 2 * RWKV_CHUNK
SUBLANES = 8
RWKV_UNROLL = 4


def _rwkv_scan_kernel(*refs):
    T = RWKV_CHUNK
    ins, outs = refs[:20], refs[20:22]
    p_ref, mix_ref, seq_ref, y0_ref, cm_ref = refs[22:]
    n_pair = RWKV_HEADS // 2
    pairs = [(bi, d, hp) for bi in range(RWKV_BATCH) for d in range(2) for hp in range(n_pair)]
    N = HEAD_DIM

    @pl.when(pl.program_id(1) == 0)
    def _():
        p_ref[...] = jnp.zeros_like(p_ref)

    row_t = lax.broadcasted_iota(jnp.int32, (T, T), 0)
    col_j = lax.broadcasted_iota(jnp.int32, (T, T), 1)
    lane = lax.broadcasted_iota(jnp.int32, (SUBLANES, 2 * N), 1)
    low = lane < N
    lane_t = jnp.where(low, lane, lane - N)
    before = (col_j < row_t, col_j > row_t)
    upto = (col_j <= row_t, col_j >= row_t)

    for slot in range(2):
        for pc, (bi, d, hp) in enumerate(pairs):
            kap_r, vT_r, v_r, w_r, beta_r, kaph_r, kch_r, bch_r, rh_r, kend_r = ins[10 * d:10 * d + 10]
            q = slot if d == 0 else 1 - slot
            rows = slice(T * q, T * q + T)
            p_pair = p_ref[pc]
            mixes = []
            for s in range(2):
                ch = 2 * pc + s
                cols = slice(N * (2 * hp + s), N * (2 * hp + s) + N)
                kch, rh = kch_r[0, bi, rows, cols], rh_r[0, bi, rows, cols]
                b_m = jnp.where(before[d], _dot_nt(kaph_r[0, bi, rows, cols], kch), 0.0)
                e_m = jnp.where(upto[d], _dot_nt(rh, kch), 0.0)
                cm_ref[ch] = jnp.where(upto[d], _dot_nt(rh, bch_r[0, bi, rows, cols]), 0.0)
                mixes.append(_dot_nt(vT_r[bi, cols, rows], b_m))
                y0_ref[ch] = _dot_nt(rh, p_pair[:, N * s:N * s + N]) + _dot(e_m, v_r[bi, rows, cols])
            mix_ref[pc] = jnp.concatenate(mixes, axis=-1)
            cols2 = slice(2 * N * hp, 2 * N * hp + 2 * N)
            seq_ref[0, pc] = kap_r[bi, rows, cols2]
            seq_ref[1, pc] = w_r[0, bi, rows, cols2]
            seq_ref[2, pc] = beta_r[0, bi, rows, cols2]

        def step(i, carry):
            for pc, (bi, d, hp) in enumerate(pairs):
                t = i if d == 0 else T - 1 - i
                at_t = lane_t == t
                kap_t, w_t, beta_t = (jnp.broadcast_to(seq_ref[op, pc, pl.ds(t, 1), :], (SUBLANES, 2 * N))
                                      for op in range(3))
                for s8 in range(N // SUBLANES):
                    sub = slice(SUBLANES * s8, SUBLANES * s8 + SUBLANES)
                    p, mix = p_ref[pc, sub, :], mix_ref[pc, sub, :]
                    prod = p * kap_t + jnp.where(at_t, mix, 0.0)
                    sa = jnp.where(low, jnp.sum(jnp.where(low, prod, 0.0), axis=-1, keepdims=True),
                                   jnp.sum(jnp.where(low, 0.0, prod), axis=-1, keepdims=True))
                    p_ref[pc, sub, :] = p * w_t - sa * beta_t
                    mix_ref[pc, sub, :] = jnp.where(at_t, sa, mix)
            return carry

        lax.fori_loop(0, T, step, 0, unroll=RWKV_UNROLL)

        for bi in range(RWKV_BATCH):
            for d in range(2):
                vT_r, kend_r = ins[10 * d + 1], ins[10 * d + 9]
                q = slot if d == 0 else 1 - slot
                rows = slice(T * q, T * q + T)
                ys = []
                for hp in range(n_pair):
                    pc = (bi * 2 + d) * n_pair + hp
                    sa_pair = mix_ref[pc]
                    grow = []
                    for s in range(2):
                        ch = 2 * pc + s
                        cols = slice(N * (2 * hp + s), N * (2 * hp + s) + N)
                        ys.append(y0_ref[ch] - _dot_nt(cm_ref[ch], sa_pair[:, N * s:N * s + N]))
                        grow.append(_dot(vT_r[bi, cols, rows], kend_r[0, bi, rows, cols]))
                    p_ref[pc] = p_ref[pc] + jnp.concatenate(grow, axis=-1)
                outs[d][bi, rows, :] = jnp.concatenate(ys, axis=-1)


def _rwkv_scan(kap, vT, v, per_dir):
    B, L, R = kap.shape
    nb = L // RWKV_ROWS
    n_ctx = CTX_LEN // RWKV_ROWS
    fwd = lambda c: c
    bwd = lambda c: jnp.where(c < n_ctx, n_ctx - 1 - c, nb + n_ctx - 1 - c)
    in_specs, args = [], []
    for d, blk in enumerate((fwd, bwd)):
        tok = pl.BlockSpec((RWKV_BATCH, RWKV_ROWS, R), lambda b, c, blk=blk: (b, blk(c), 0))
        tr = pl.BlockSpec((RWKV_BATCH, R, RWKV_ROWS), lambda b, c, blk=blk: (b, 0, blk(c)))
        dirs = pl.BlockSpec((1, RWKV_BATCH, RWKV_ROWS, R), lambda b, c, blk=blk, d=d: (d, b, blk(c), 0))
        in_specs += [tok, tr, tok] + [dirs] * 7
        args += [kap, vT, v] + list(per_dir)
    out_specs = [pl.BlockSpec((RWKV_BATCH, RWKV_ROWS, R), lambda b, c, blk=blk: (b, blk(c), 0))
                 for blk in (fwd, bwd)]
    n_chain = RWKV_BATCH * 2 * RWKV_HEADS
    tile = (n_chain, HEAD_DIM, RWKV_CHUNK)
    pair_tile = (n_chain // 2, HEAD_DIM, 2 * HEAD_DIM)
    return pl.pallas_call(
        _rwkv_scan_kernel,
        grid=(B // RWKV_BATCH, nb),
        in_specs=in_specs,
        out_specs=out_specs,
        out_shape=[jax.ShapeDtypeStruct((B, L, R), F32)] * 2,
        scratch_shapes=[pltpu.VMEM(pair_tile, F32), pltpu.VMEM(pair_tile, F32), pltpu.VMEM((3,) + pair_tile, F32),
                        pltpu.VMEM(tile, F32), pltpu.VMEM(tile, F32)],
        compiler_params=pltpu.CompilerParams(
            dimension_semantics=("parallel", "arbitrary"), vmem_limit_bytes=VMEM_LIMIT),
        name="rwkv_scan",
    )(*args)


def _rwkv_out_kernel(yf_ref, yb_ref, r_ref, k_ref, v_ref, g_ref, rk_ref, lnw_ref, lnb_ref, ones_ref, o_ref):
    ones_blk = ones_ref[...]
    y = yf_ref[0] + yb_ref[0]
    mu = _dot_hi(y, ones_blk) * (1.0 / HEAD_DIM)
    yc = y - mu
    var = _dot_hi(yc * yc, ones_blk) * (1.0 / HEAD_DIM)
    yn = yc * lax.rsqrt(var + RWKV_GN_EPS) * lnw_ref[...] + lnb_ref[...]
    bonus = _dot_hi(r_ref[0] * k_ref[0] * rk_ref[...], ones_blk) * v_ref[0]
    o_ref[0] = ((yn + bonus) * g_ref[0]).astype(o_ref.dtype)


def _rwkv_out(y_f, y_b, r, k, v, g, p):
    B, L, R = r.shape
    idx = jnp.arange(R)
    ones_blk = ((idx[:, None] // HEAD_DIM) == (idx[None, :] // HEAD_DIM)).astype(F32)
    consts = [p['rwkv_r_k'].reshape(1, R), p['rwkv_ln_w'][None, :], p['rwkv_ln_b'][None, :], ones_blk]
    tok = pl.BlockSpec((1, ROW_TILE, R), lambda b, j: (b, j, 0))
    return pl.pallas_call(
        _rwkv_out_kernel,
        grid=(B, L // ROW_TILE),
        in_specs=[tok] * 6 + [pl.BlockSpec(a.shape, lambda b, j: (0, 0)) for a in consts],
        out_specs=tok,
        out_shape=jax.ShapeDtypeStruct((B, L, R), BF16),
        compiler_params=pltpu.CompilerParams(dimension_semantics=("parallel", "parallel")),
        name="rwkv_out",
    )(y_f, y_b, r, k, v, g, *consts)


MLA_SLOT = 128
MLA_WIDE = MLA_HEADS * MLA_SLOT


def _rope_mla(x, cos, sin_signed):
    half = MLA_ROPE // 2
    lane = lax.broadcasted_iota(jnp.int32, x.shape, 1) % MLA_SLOT
    partner = jnp.where(lane < MLA_NOPE + half, pltpu.roll(x, x.shape[1] - half, 1), pltpu.roll(x, half, 1))
    return x * cos + partner * sin_signed


def _mla_prep_kernel(u_ref, cos_ref, sin_ref, qn_ref, kvn_ref, wq_ref, wk_ref, wv_ref, q_ref, k_ref, v_ref):
    u = u_ref[0]
    q_dn = u[:, :MLA_Q_RANK]
    kv_dn = u[:, MLA_Q_RANK:MLA_Q_RANK + MLA_KV_RANK]
    k_rot = u[:, MLA_Q_RANK + MLA_KV_RANK:]
    norm = lambda t, w: (t * lax.rsqrt(jnp.mean(t * t, axis=-1, keepdims=True) + RMS_EPS) * w).astype(BF16)
    qn, kvn = norm(q_dn, qn_ref[...]), norm(kv_dn, kvn_ref[...])
    cos, sin = cos_ref[...], sin_ref[...]
    wide = lambda t: jnp.concatenate([t] * MLA_HEADS, axis=1)
    q_ref[0] = (_rope_mla(_dot(qn, wq_ref[...]), wide(cos), wide(sin)) * (MLA_QK ** -0.5)).astype(BF16)
    k_ref[0] = (_dot(kvn, wk_ref[...]) + wide(_rope_mla(k_rot, cos, sin))).astype(BF16)
    v = _dot(kvn, wv_ref[...])
    lane = lax.broadcasted_iota(jnp.int32, v.shape, 1) % MLA_SLOT
    v_ref[0] = jnp.where(lane == MLA_V, 1.0, v).astype(BF16)


def _mla_prep(u_mla, cos_rows, sin_rows, p):
    B, L, _ = u_mla.shape
    slots = lambda w, width: jnp.pad(w.reshape(w.shape[0], MLA_HEADS, width),
                                     ((0, 0), (0, 0), (0, MLA_SLOT - width))).reshape(w.shape[0], MLA_WIDE)
    w_ukv = p['mla_w_ukv'].reshape(MLA_KV_RANK, MLA_HEADS, MLA_NOPE + MLA_V)
    consts = [p['mla_q_norm'][None, :], p['mla_kv_norm'][None, :],
              slots(p['mla_w_uq'], MLA_QK).astype(BF16),
              slots(w_ukv[:, :, :MLA_NOPE].reshape(MLA_KV_RANK, -1), MLA_NOPE).astype(BF16),
              slots(w_ukv[:, :, MLA_NOPE:].reshape(MLA_KV_RANK, -1), MLA_V).astype(BF16)]
    tok = lambda w: pl.BlockSpec((1, ROW_TILE, w), lambda b, j: (b, j, 0))
    table = pl.BlockSpec((ROW_TILE, MLA_SLOT), lambda b, j: (j, 0))
    return pl.pallas_call(
        _mla_prep_kernel,
        grid=(B, L // ROW_TILE),
        in_specs=[tok(U_MLA), table, table] + [pl.BlockSpec(a.shape, lambda b, j: (0, 0)) for a in consts],
        out_specs=[tok(MLA_WIDE)] * 3,
        out_shape=[jax.ShapeDtypeStruct((B, L, MLA_WIDE), BF16)] * 3,
        compiler_params=pltpu.CompilerParams(
            dimension_semantics=("parallel", "parallel"), vmem_limit_bytes=VMEM_LIMIT),
        name="mla_prep",
    )(u_mla, cos_rows, sin_rows, *consts)


def _attn_kernel(q_ref, k_ref, v_ref, o_ref):
    def attend(n_keys):
        outs = []
        for h in range(MLA_HEADS):
            cols = slice(MLA_SLOT * h, MLA_SLOT * h + MLA_SLOT)
            s = _dot_nt(q_ref[0, :, cols], k_ref[0, :n_keys, cols])
            p = jnp.exp((s - jnp.max(s, axis=-1, keepdims=True)).astype(BF16))
            o = _dot(p, v_ref[0, :n_keys, cols])
            outs.append(o[:, :MLA_V] / o[:, MLA_V:MLA_V + 1])
        o_ref[0] = jnp.concatenate(outs, axis=-1).astype(o_ref.dtype)

    @pl.when(pl.program_id(1) == 0)
    def _():
        attend(CTX_LEN)

    @pl.when(pl.program_id(1) > 0)
    def _():
        attend(k_ref.shape[1])


def _attention(q, k, v):
    B, L, W = q.shape
    resident = lambda: pl.BlockSpec((1, L, W), lambda b, i: (b, 0, 0), pipeline_mode=pl.Buffered(1))
    return pl.pallas_call(
        _attn_kernel,
        grid=(B, L // ROW_TILE),
        in_specs=[pl.BlockSpec((1, ROW_TILE, W), lambda b, i: (b, i, 0)), resident(), resident()],
        out_specs=pl.BlockSpec((1, ROW_TILE, MLA_DIM), lambda b, i: (b, i, 0)),
        out_shape=jax.ShapeDtypeStruct((B, L, MLA_DIM), BF16),
        compiler_params=pltpu.CompilerParams(
            dimension_semantics=("parallel", "parallel"), vmem_limit_bytes=VMEM_LIMIT),
        name="mla_attention",
    )(q, k, v)


def _rope_rows(x, cos, sin_signed):
    lane = lax.broadcasted_iota(jnp.int32, x.shape, 1)
    partner = jnp.where(lane % RET_KEY < RET_KEY // 2,
                        pltpu.roll(x, x.shape[1] - RET_KEY // 2, 1), pltpu.roll(x, RET_KEY // 2, 1))
    return x * cos + partner * sin_signed


def _ret_kernel(*refs):
    ins, (inner_ref, cross_ref, tailT_ref, cdec_ref), outs, r_ref = refs[:10], refs[10:14], refs[14:16], refs[16]

    @pl.when(pl.program_id(1) == 0)
    def _():
        r_ref[...] = jnp.zeros_like(r_ref)

    for d in range(2):
        q_ref, k_ref, v_ref, cos_ref, sin_ref = ins[5 * d:5 * d + 5]
        cos, sin = cos_ref[...], sin_ref[...]
        q_all = _rope_rows(q_ref[0], cos, sin)
        k_all = _rope_rows(k_ref[0], cos, sin) * (RET_KEY ** -0.5)
        kT_all = k_all.T
        v_all = v_ref[0]
        os = []
        for h in range(RET_HEADS):
            cols = slice(RET_KEY * h, RET_KEY * h + RET_KEY)
            q = q_all[:, cols].astype(BF16)
            v = v_all[:, cols].astype(BF16)
            state = r_ref[d, h]
            s = _dot_nt(q, k_all[:, cols].astype(BF16)) * inner_ref[d, h]
            os.append(_dot(s.astype(BF16), v) + _dot(q, state.astype(BF16)) * cross_ref[d, h])
            r_ref[d, h] = state * cdec_ref[d, h] + _dot((kT_all[cols, :] * tailT_ref[d, h]).astype(BF16), v)
        outs[d][0] = jnp.concatenate(os, axis=-1)


def _retention_scan(u_ret, cos, sin_signed, inner, cross, tailT, cdec):
    B, L, _ = u_ret.shape
    C = RET_CHUNK
    nb = L // C
    n_ctx = CTX_LEN // C
    fwd = lambda c: c
    bwd = lambda c: jnp.where(c < n_ctx, n_ctx - 1 - c, nb + n_ctx - 1 - c)
    in_specs, args = [], []
    for blk in (fwd, bwd):
        in_specs += [pl.BlockSpec((1, C, RET_DIM), lambda b, c, blk=blk, i=i: (b, blk(c), i)) for i in range(3)]
        in_specs += [pl.BlockSpec((C, RET_DIM), lambda b, c, blk=blk: (blk(c), 0))] * 2
        args += [u_ret, u_ret, u_ret, cos, sin_signed]
    consts = [inner, cross, tailT, cdec]
    in_specs += [pl.BlockSpec(a.shape, lambda b, c: (0, 0, 0, 0)) for a in consts]
    return pl.pallas_call(
        _ret_kernel,
        grid=(B, nb),
        in_specs=in_specs,
        out_specs=[pl.BlockSpec((1, C, RET_DIM), lambda b, c, blk=blk: (b, blk(c), 0)) for blk in (fwd, bwd)],
        out_shape=[jax.ShapeDtypeStruct((B, L, RET_DIM), F32)] * 2,
        scratch_shapes=[pltpu.VMEM((2, RET_HEADS, RET_KEY, RET_VAL), F32)],
        compiler_params=pltpu.CompilerParams(
            dimension_semantics=("parallel", "arbitrary"), vmem_limit_bytes=VMEM_LIMIT),
        name="retention_scan",
    )(*args, *consts)


def _ret_out_kernel(of_ref, ob_ref, g_ref, ones_ref, o_ref):
    y = of_ref[0] + ob_ref[0]
    ms = _dot_hi(y * y, ones_ref[...]) * (1.0 / RET_VAL)
    o_ref[0] = (jax.nn.silu(g_ref[0]) * (y * lax.rsqrt(ms + RMS_EPS))).astype(o_ref.dtype)


def _ret_out(o_f, o_b, u_ret):
    B, L, R = o_f.shape
    idx = jnp.arange(R)
    ones_blk = ((idx[:, None] // RET_VAL) == (idx[None, :] // RET_VAL)).astype(F32)
    tok = pl.BlockSpec((1, ROW_TILE, R), lambda b, j: (b, j, 0))
    return pl.pallas_call(
        _ret_out_kernel,
        grid=(B, L // ROW_TILE),
        in_specs=[tok, tok, pl.BlockSpec((1, ROW_TILE, R), lambda b, j: (b, j, 3)),
                  pl.BlockSpec((R, R), lambda b, j: (0, 0))],
        out_specs=tok,
        out_shape=jax.ShapeDtypeStruct((B, L, R), BF16),
        compiler_params=pltpu.CompilerParams(dimension_semantics=("parallel", "parallel")),
        name="ret_out",
    )(o_f, o_b, u_ret, ones_blk)


def _merge_kernel(x_ref, h_ref, ya_ref, yb_ref, yc_ref, mod_ref, wg_ref, bg_ref, wa_ref, wb_ref, wc_ref,
                  wo_ref, n2_ref, wr_ref, br_ref, xo_ref, h2_ref, lg_ref):
    D = D_MODEL
    gates = jax.nn.sigmoid(_dot(h_ref[0], wg_ref[...]) + bg_ref[...])
    m = (gates[:, :D] * _dot(ya_ref[0], wa_ref[...])
         + gates[:, D:2 * D] * _dot(yb_ref[0], wb_ref[...])
         + gates[:, 2 * D:] * _dot(yc_ref[0], wc_ref[...]))
    xn = x_ref[0] + mod_ref[0, 0, 2:3, :] * _dot(m.astype(BF16), wo_ref[...])
    xo_ref[0] = xn
    y = xn * lax.rsqrt(jnp.mean(xn * xn, axis=-1, keepdims=True) + RMS_EPS) * n2_ref[...]
    h2 = y * (1.0 + mod_ref[0, 0, 4:5, :]) + mod_ref[0, 0, 3:4, :]
    h2_ref[0] = h2.astype(BF16)
    lg_ref[0] = jnp.dot(h2, wr_ref[...], preferred_element_type=F32,
                        precision=lax.Precision.HIGHEST) + br_ref[...]


def _merge(x_all, h, ya, yb, yc, mods, wg, bg, wa, wb, wc, wo, n2, wr, br, tile_off):
    B, T, D = x_all.shape
    nt = T // ROW_TILE - tile_off
    tok = lambda w: pl.BlockSpec((1, ROW_TILE, w), lambda b, j: (b, j + tile_off, 0))
    out = lambda w: pl.BlockSpec((1, ROW_TILE, w), lambda b, j: (b, j, 0))
    full = lambda a: pl.BlockSpec(a.shape, lambda b, j: (0,) * a.ndim)
    return pl.pallas_call(
        _merge_kernel,
        grid=(B, nt),
        in_specs=[tok(D), tok(D), tok(RWKV_DIM), tok(MLA_DIM), tok(RET_DIM),
                  pl.BlockSpec((1, 1, 8, D), lambda b, j: (b, jnp.minimum(j + tile_off, 1), 0, 0)),
                  full(wg), full(bg), full(wa), full(wb), full(wc), full(wo), full(n2), full(wr), full(br)],
        out_specs=[out(D), out(D), out(ROUTER_LANES)],
        out_shape=[jax.ShapeDtypeStruct((B, nt * ROW_TILE, D), F32),
                   jax.ShapeDtypeStruct((B, nt * ROW_TILE, D), BF16),
                   jax.ShapeDtypeStruct((B, nt * ROW_TILE, ROUTER_LANES), F32)],
        compiler_params=pltpu.CompilerParams(
            dimension_semantics=("parallel", "parallel"), vmem_limit_bytes=VMEM_LIMIT),
        name="merge",
    )(x_all, h, ya, yb, yc, mods, wg, bg, wa, wb, wc, wo, n2, wr, br)


def _moe_kernel(te_ref, x_ref, cw_ref, wg_ref, wu_ref, wd_ref, o_ref):
    del te_ref
    x = x_ref[...]
    hid = jax.nn.silu(_dot(x, wg_ref[0])) * _dot(x, wu_ref[0])
    o_ref[...] = _dot(hid.astype(BF16), wd_ref[0]) * cw_ref[...]


def _moe_experts(tile_expert, x_sorted, cw_sorted, w_gate, w_up, w_down):
    n_rows, D = x_sorted.shape
    n_tiles = n_rows // MOE_TILE
    grid_spec = pltpu.PrefetchScalarGridSpec(
        num_scalar_prefetch=1,
        grid=(n_tiles,),
        in_specs=[pl.BlockSpec((MOE_TILE, D), lambda i, te: (i, 0)),
                  pl.BlockSpec((MOE_TILE, 1), lambda i, te: (i, 0)),
                  pl.BlockSpec((1, D, EXPERT_HIDDEN), lambda i, te: (te[i], 0, 0)),
                  pl.BlockSpec((1, D, EXPERT_HIDDEN), lambda i, te: (te[i], 0, 0)),
                  pl.BlockSpec((1, EXPERT_HIDDEN, D), lambda i, te: (te[i], 0, 0))],
        out_specs=pl.BlockSpec((MOE_TILE, D), lambda i, te: (i, 0)),
    )
    return pl.pallas_call(
        _moe_kernel,
        grid_spec=grid_spec,
        out_shape=jax.ShapeDtypeStruct((n_rows, D), F32),
        compiler_params=pltpu.CompilerParams(
            dimension_semantics=("arbitrary",), vmem_limit_bytes=VMEM_LIMIT),
        name="moe_experts",
    )(tile_expert, x_sorted, cw_sorted, w_gate, w_up, w_down)


def _final_norm_kernel(x_ref, w_ref, o_ref):
    x = x_ref[...]
    o_ref[...] = x * lax.rsqrt(jnp.mean(x * x, axis=-1, keepdims=True) + RMS_EPS) * w_ref[...]


def _final_norm(x, w):
    n_rows, D = x.shape
    tile = 512
    return pl.pallas_call(
        _final_norm_kernel,
        grid=(n_rows // tile,),
        in_specs=[pl.BlockSpec((tile, D), lambda i: (i, 0)), pl.BlockSpec((1, D), lambda i: (0, 0))],
        out_specs=pl.BlockSpec((tile, D), lambda i: (i, 0)),
        out_shape=jax.ShapeDtypeStruct((n_rows, D), F32),
        compiler_params=pltpu.CompilerParams(dimension_semantics=("parallel",)),
        name="final_norm",
    )(x, w)


def _rope_table(n_tokens, rot_dim):
    rows = n_tokens // GRID_W
    row = jnp.repeat(jnp.arange(rows, dtype=F32), GRID_W)
    col = jnp.tile(jnp.arange(GRID_W, dtype=F32), rows)
    n_freq = rot_dim // 4
    inv_freq = ROPE_BASE ** (-jnp.arange(n_freq, dtype=F32) / n_freq)
    ang = jnp.concatenate([row[:, None] * inv_freq, col[:, None] * inv_freq], axis=-1)
    return jnp.cos(ang), jnp.sin(ang)


def _rwkv_branch(u_rwkv, p):
    r, k, v, vT, kap, g, *per_dir = _rwkv_prep(u_rwkv, p)
    y_f, y_b = _rwkv_scan(kap, vT, v, per_dir)
    return _rwkv_out(y_f, y_b, r, k, v, g, p)


def _mla_branch(u_mla, p, rope):
    L = u_mla.shape[1]
    cos, sin = rope
    n_ctx = L - cos.shape[0]
    cos = jnp.concatenate([jnp.ones((n_ctx, cos.shape[1]), F32), cos], axis=0)
    sin = jnp.concatenate([jnp.zeros((n_ctx, sin.shape[1]), F32), sin], axis=0)
    tail = MLA_SLOT - MLA_NOPE - MLA_ROPE
    cos_rows = jnp.concatenate([jnp.ones((L, MLA_NOPE), F32), cos, cos, jnp.ones((L, tail), F32)], axis=1)
    sin_rows = jnp.concatenate([jnp.zeros((L, MLA_NOPE), F32), -sin, sin, jnp.zeros((L, tail), F32)], axis=1)
    q, k, v = _mla_prep(u_mla, cos_rows, sin_rows, p)
    return _attention(q, k, v)


def _retention_branch(u_ret, p, rope):
    B, L, _ = u_ret.shape
    C = RET_CHUNK
    cos, sin = rope
    n_ctx = L - cos.shape[0]
    cos = jnp.concatenate([jnp.ones((n_ctx, cos.shape[1]), F32), cos], axis=0)
    sin = jnp.concatenate([jnp.zeros((n_ctx, sin.shape[1]), F32), sin], axis=0)
    cos_rows = jnp.tile(cos, (1, 2 * RET_HEADS))
    sin_rows = jnp.tile(jnp.concatenate([-sin, sin], axis=1), (1, RET_HEADS))
    log_gamma = jax.nn.log_sigmoid(p['ret_decay'])
    idx = jnp.arange(C, dtype=F32)
    rel = idx[:, None] - idx[None, :]
    lg_f, lg_b = log_gamma[0][:, None, None], log_gamma[1][:, None, None]
    inner = jnp.stack([jnp.where(rel[None] >= 0, jnp.exp(lg_f * jnp.maximum(rel, 0.0)[None]), 0.0),
                       jnp.where(rel[None] <= 0, jnp.exp(lg_b * jnp.maximum(-rel, 0.0)[None]), 0.0)])
    cross = jnp.stack([jnp.exp(lg_f * (idx + 1.0)[None, :, None]), jnp.exp(lg_b * (C - idx)[None, :, None])])
    cross = jnp.broadcast_to(cross, (2, RET_HEADS, C, RET_VAL))
    tailT = jnp.stack([jnp.exp(lg_f * (C - 1.0 - idx)[None, None, :]), jnp.exp(lg_b * idx[None, None, :])])
    tailT = jnp.broadcast_to(tailT, (2, RET_HEADS, RET_KEY, C))
    cdec = jnp.broadcast_to(jnp.exp(log_gamma * C)[:, :, None, None], (2, RET_HEADS, RET_KEY, RET_VAL))
    o_f, o_b = _retention_scan(u_ret, cos_rows, sin_rows, inner, cross, tailT, cdec)
    return _ret_out(o_f, o_b, u_ret)


def _route(logits, p):
    n_tok = logits.shape[0]
    grp_prob = jax.nn.softmax(logits[:, :N_GROUPS], axis=-1)
    grp_idx = jnp.argmax(grp_prob, axis=-1)
    grp_p = jnp.max(grp_prob, axis=-1, keepdims=True)
    exp_logits = logits[:, N_GROUPS:N_GROUPS + N_EXPERTS].reshape(n_tok, N_GROUPS, EXPERTS_PER_GROUP)
    grp_hot = grp_idx[:, None] == jnp.arange(N_GROUPS)[None, :]
    in_group = jnp.sum(jnp.where(grp_hot[:, :, None], exp_logits, 0.0), axis=1)
    exp_prob = jax.nn.softmax(in_group, axis=-1)
    lanes = jnp.arange(EXPERTS_PER_GROUP)[None, :]
    e1 = jnp.argmax(exp_prob, axis=-1)
    p1 = jnp.max(exp_prob, axis=-1)
    rest = jnp.where(lanes == e1[:, None], -1.0, exp_prob)
    e2 = jnp.argmax(rest, axis=-1)
    p2 = jnp.max(rest, axis=-1)
    exp_p = jnp.stack([p1, p2], axis=-1)
    exp_idx = jnp.stack([e1, e2], axis=-1)
    weights = grp_p * exp_p / jnp.sum(exp_p, axis=-1, keepdims=True)
    return grp_idx[:, None] * EXPERTS_PER_GROUP + exp_idx, weights


def _moe(h2, logits, p):
    n_tok, D = h2.shape
    expert_id, weights = _route(logits, p)
    n_assign = n_tok * TOP_K
    flat_e = expert_id.reshape(n_assign).astype(jnp.int32)
    flat_w = weights.reshape(n_assign)
    counts = jnp.sum((flat_e[:, None] == jnp.arange(N_EXPERTS, dtype=jnp.int32)[None, :]).astype(jnp.int32), axis=0)
    padded = ((counts + MOE_TILE - 1) // MOE_TILE) * MOE_TILE
    pad_end = jnp.cumsum(padded)
    pad_start = pad_end - padded
    raw_start = jnp.cumsum(counts) - counts
    order = jnp.argsort(flat_e, stable=True).astype(jnp.int32)
    rank = jnp.argsort(order).astype(jnp.int32)
    n_tiles = n_assign // MOE_TILE + N_EXPERTS
    tile_start = jnp.arange(n_tiles, dtype=jnp.int32) * MOE_TILE
    tile_expert = jnp.minimum(
        jnp.sum((tile_start[:, None] >= pad_end[None, :]).astype(jnp.int32), axis=1), N_EXPERTS - 1)
    row_e = jnp.repeat(tile_expert, MOE_TILE)
    within = jnp.arange(n_tiles * MOE_TILE, dtype=jnp.int32) - pad_start[row_e]
    valid = within < counts[row_e]
    src_assign = order[jnp.clip(raw_start[row_e] + within, 0, n_assign - 1)]
    src_tok = jnp.where(valid, src_assign // TOP_K, 0)
    cw = jnp.where(valid, flat_w[src_assign], 0.0)
    pos = (pad_start[flat_e] + rank - raw_start[flat_e]).reshape(n_tok, TOP_K)
    out = _moe_experts(tile_expert, h2[src_tok], cw[:, None], p['moe_w_gate'], p['moe_w_up'], p['moe_w_down'])
    return out[pos[:, 0]] + out[pos[:, 1]]


def _permute_w_in(w_in):
    w_a = w_in[:, :U_RWKV]
    lo = U_RWKV + MLA_Q_RANK + MLA_KV_RANK
    w_b = jnp.concatenate([w_in[:, U_RWKV:lo], jnp.pad(w_in[:, lo:lo + MLA_ROPE], ((0, 0), (MLA_NOPE, MLA_SLOT - MLA_NOPE - MLA_ROPE)))], axis=1)
    w_c = w_in[:, U_RWKV + U_MLA_RAW:]
    return w_a.astype(BF16), w_b.astype(BF16), w_c.astype(BF16)


def kernel(x, c, ctx, c_ctx, w_mod, b_mod, norm1_w, w_in, rwkv_conv, rwkv_w0, rwkv_w_up, rwkv_a0, rwkv_a_up, rwkv_g_up, rwkv_k_k, rwkv_k_a, rwkv_r_k, rwkv_ln_w, rwkv_ln_b, mla_q_norm, mla_w_uq, mla_kv_norm, mla_w_ukv, ret_decay, w_branch_a, w_branch_b, w_branch_c, w_branch_gate, b_branch_gate, w_out, norm2_w, moe_w_group, moe_b_group, moe_w_expert, moe_b_expert, moe_w_gate, moe_w_up, moe_w_down, final_norm_w):
    B, n_lat, D = x.shape
    depth = w_mod.shape[0]
    ropes = (_rope_table(n_lat, MLA_ROPE), _rope_table(n_lat, RET_KEY))
    x_all = jnp.concatenate([ctx, x], axis=1)
    L = x_all.shape[1]
    for l in range(depth):
        with_ctx = l < depth - 1
        p = {
            'rwkv_conv': rwkv_conv[l], 'rwkv_w0': rwkv_w0[l], 'rwkv_w_up': rwkv_w_up[l],
            'rwkv_a0': rwkv_a0[l], 'rwkv_a_up': rwkv_a_up[l], 'rwkv_g_up': rwkv_g_up[l],
            'rwkv_k_k': rwkv_k_k[l], 'rwkv_k_a': rwkv_k_a[l], 'rwkv_r_k': rwkv_r_k[l].reshape(RWKV_HEADS, HEAD_DIM),
            'rwkv_ln_w': rwkv_ln_w[l], 'rwkv_ln_b': rwkv_ln_b[l],
            'mla_q_norm': mla_q_norm[l], 'mla_w_uq': mla_w_uq[l], 'mla_kv_norm': mla_kv_norm[l],
            'mla_w_ukv': mla_w_ukv[l], 'ret_decay': ret_decay[l],
            'moe_w_gate': moe_w_gate[l].astype(BF16), 'moe_w_up': moe_w_up[l].astype(BF16),
            'moe_w_down': moe_w_down[l].astype(BF16),
        }
        mod_lat = jax.nn.silu(c) @ w_mod[l] + b_mod[l]
        mod_ctx = jax.nn.silu(c_ctx) @ w_mod[l] + b_mod[l]
        mods = jnp.stack([jnp.broadcast_to(mod_ctx, mod_lat.shape), mod_lat], axis=1).reshape(B, 2, N_MOD, D)
        mods = jnp.pad(mods, ((0, 0), (0, 0), (0, 8 - N_MOD), (0, 0)))

        w_a, w_b, w_c = _permute_w_in(w_in[l])
        h, u_rwkv, u_mla, u_ret = _inproj(x_all, mods, norm1_w[l][None, :], w_a, w_b, w_c)

        ya = _rwkv_branch(u_rwkv, p)
        yb = _mla_branch(u_mla, p, ropes[0])
        yc = _retention_branch(u_ret, p, ropes[1])

        w_router = jnp.pad(jnp.concatenate([moe_w_group[l], moe_w_expert[l]], axis=1),
                           ((0, 0), (0, ROUTER_LANES - N_GROUPS - N_EXPERTS)))
        b_router = jnp.pad(jnp.concatenate([moe_b_group[l], moe_b_expert[l]]),
                           (0, ROUTER_LANES - N_GROUPS - N_EXPERTS))[None, :]
        tile_off = 0 if with_ctx else CTX_LEN // ROW_TILE
        x_mid, h2, logits = _merge(
            x_all, h, ya, yb, yc, mods, w_branch_gate[l].astype(BF16), b_branch_gate[l][None, :],
            w_branch_a[l].astype(BF16), w_branch_b[l].astype(BF16), w_branch_c[l].astype(BF16),
            w_out[l].astype(BF16), norm2_w[l][None, :], w_router, b_router, tile_off)
        Lm = x_mid.shape[1]
        f = _moe(h2.reshape(B * Lm, D), logits.reshape(B * Lm, ROUTER_LANES), p).reshape(B, Lm, D)
        if with_ctx:
            g2 = jnp.concatenate([jnp.broadcast_to(mods[:, 0:1, 5, :], (B, CTX_LEN, D)),
                                  jnp.broadcast_to(mods[:, 1:2, 5, :], (B, L - CTX_LEN, D))], axis=1)
            x_all = x_mid + g2 * f
        else:
            x_lat = x_mid + mods[:, 1:2, 5, :] * f
    return _final_norm(x_lat.reshape(B * n_lat, D), final_norm_w[None, :]).reshape(B, n_lat, D)
```

```python
import functools

import jax
import jax.numpy as jnp
from jax import lax
from jax.experimental import pallas as pl
from jax.experimental.pallas import tpu as pltpu

D_MODEL = 1024
CTX_LEN = 256
GRID_W = 64
N_MOD = 6
RMS_EPS = 1e-6
ROPE_BASE = 10000.0
HEAD_DIM = 64
F32 = jnp.float32
BF16 = jnp.bfloat16

RWKV_HEADS = 4
RWKV_DIM = RWKV_HEADS * HEAD_DIM
DECAY_LORA = 64
AAA_LORA = 64
GATE_LORA = 128
RWKV_GN_EPS = 64e-5

MLA_HEADS = 8
MLA_NOPE = 64
MLA_ROPE = 32
MLA_QK = MLA_NOPE + MLA_ROPE
MLA_V = 64
MLA_Q_RANK = 256
MLA_KV_RANK = 128
MLA_DIM = MLA_HEADS * MLA_V

RET_HEADS = 4
RET_KEY = 64
RET_VAL = 64
RET_DIM = RET_HEADS * RET_VAL

N_GROUPS = 4
EXPERTS_PER_GROUP = 8
N_EXPERTS = N_GROUPS * EXPERTS_PER_GROUP
TOP_K = 2
EXPERT_HIDDEN = 512

U_RWKV = 3 * RWKV_DIM + 2 * DECAY_LORA + 2 * AAA_LORA + GATE_LORA
U_MLA_RAW = MLA_Q_RANK + MLA_KV_RANK + MLA_ROPE
U_MLA = 512
U_RET = 4 * RET_DIM

ROW_TILE = 256
RWKV_CHUNK = 64
RET_CHUNK = 256
ROUTER_LANES = 128
MOE_TILE = 256
VMEM_LIMIT = 56 * 1024 * 1024


def _dot(a, b):
    return jnp.dot(a, b, preferred_element_type=F32)


def _dot_nt(a, b):
    return lax.dot_general(a, b, (((1,), (1,)), ((), ())), preferred_element_type=F32)


def _inproj_kernel(x_ref, mod_ref, nw_ref, wa_ref, wb_ref, wc_ref, h_ref, ua_ref, ub_ref, uc_ref):
    x = x_ref[0]
    y = x * lax.rsqrt(jnp.mean(x * x, axis=-1, keepdims=True) + RMS_EPS) * nw_ref[...]
    h = y * (1.0 + mod_ref[0, 0, 1:2, :]) + mod_ref[0, 0, 0:1, :]
    hb = h.astype(BF16)
    h_ref[0] = hb
    ua_ref[0] = _dot(hb, wa_ref[...])
    ub_ref[0] = _dot(hb, wb_ref[...])
    uc_ref[0] = _dot(hb, wc_ref[...])


def _inproj(x_all, mods, norm_w, w_a, w_b, w_c):
    B, T, D = x_all.shape
    nt = T // ROW_TILE
    tok = lambda w: pl.BlockSpec((1, ROW_TILE, w), lambda b, j: (b, j, 0))
    full = lambda a: pl.BlockSpec(a.shape, lambda b, j: (0,) * a.ndim)
    return pl.pallas_call(
        _inproj_kernel,
        grid=(B, nt),
        in_specs=[tok(D),
                  pl.BlockSpec((1, 1, 8, D), lambda b, j: (b, jnp.minimum(j, 1), 0, 0)),
                  full(norm_w), full(w_a), full(w_b), full(w_c)],
        out_specs=[tok(D), tok(U_RWKV), tok(U_MLA), tok(U_RET)],
        out_shape=[jax.ShapeDtypeStruct((B, T, D), BF16),
                   jax.ShapeDtypeStruct((B, T, U_RWKV), F32),
                   jax.ShapeDtypeStruct((B, T, U_MLA), F32),
                   jax.ShapeDtypeStruct((B, T, U_RET), F32)],
        compiler_params=pltpu.CompilerParams(
            dimension_semantics=("parallel", "parallel"), vmem_limit_bytes=VMEM_LIMIT),
        name="inproj",
    )(x_all, mods, norm_w, w_a, w_b, w_c)


def _dot_hi(a, b):
    return jnp.dot(a, b, preferred_element_type=F32, precision=lax.Precision.HIGHEST)


def _softplus(z):
    return jnp.maximum(z, 0.0) + jnp.log(1.0 + jnp.exp(-jnp.abs(z)))


def _rwkv_prep_kernel(u_ref, up_ref, un_ref, conv_ref, kk_ref, ka_ref, w0_ref, a0_ref, wup_ref, aup_ref,
                      gup_ref, ones_ref, tril_ref, triu_ref,
                      r_ref, k_ref, v_ref, vTp_ref, kap_ref, g_ref,
                      w_ref, beta_ref, kaph_ref, kch_ref, bch_ref, rh_ref, kend_ref):
    R = RWKV_DIM
    j = pl.program_id(1)
    u = u_ref[0]
    x = u[:, :3 * R]
    prev_row = jnp.where(j >= 2, up_ref[0, 7:8, :3 * R], 0.0)
    next_row = jnp.where((j >= 1) & (j < pl.num_programs(1) - 1), un_ref[0, 0:1, :3 * R], 0.0)
    rid = lax.broadcasted_iota(jnp.int32, (ROW_TILE, 3 * R), 0)
    x_prev = jnp.where(rid == 0, prev_row, pltpu.roll(x, 1, 0))
    x_next = jnp.where(rid == ROW_TILE - 1, next_row, pltpu.roll(x, ROW_TILE - 1, 0))
    rkv = x_prev * conv_ref[0:1, :] + x * conv_ref[1:2, :] + x_next * conv_ref[2:3, :]
    r, k, v = rkv[:, :R], rkv[:, R:2 * R], rkv[:, 2 * R:]
    ones_blk = ones_ref[...]
    kk = k * kk_ref[...]
    kap = kk / jnp.maximum(jnp.sqrt(_dot_hi(kk * kk, ones_blk)), 1e-12)
    r_ref[0] = r
    k_ref[0] = k
    v_ref[0] = v
    vT = v.T
    for c in range(ROW_TILE // RWKV_CHUNK):
        for hp in range(RWKV_HEADS // 2):
            vTp_ref[0, c, hp] = jnp.concatenate(
                [vT[HEAD_DIM * (2 * hp + s):HEAD_DIM * (2 * hp + s + 1), RWKV_CHUNK * c:RWKV_CHUNK * (c + 1)]
                 for s in range(2)], axis=1)
    kap_ref[0] = kap
    lo = 3 * R
    tw = jnp.tanh(u[:, lo:lo + 2 * DECAY_LORA]).astype(BF16)
    ad = u[:, lo + 2 * DECAY_LORA:lo + 2 * DECAY_LORA + 2 * AAA_LORA].astype(BF16)
    g_ref[0] = _dot(jax.nn.sigmoid(u[:, lo + 2 * DECAY_LORA + 2 * AAA_LORA:]).astype(BF16), gup_ref[...])
    for d in range(2):
        log_w = -_softplus(-(w0_ref[d:d + 1, :] + _dot(tw, wup_ref[d]))) - 0.5
        logw = -jnp.exp(log_w)
        a = jax.nn.sigmoid(a0_ref[d:d + 1, :] + _dot(ad, aup_ref[d]))
        kmod = k * (1.0 + (a - 1.0) * ka_ref[...])
        beta = kap * a
        cs = _dot_hi(tril_ref[...] if d == 0 else triu_ref[...], logw)
        tot = _dot_hi(ones_blk, logw)
        inv = jnp.exp(-cs)
        w_ref[d, 0] = jnp.exp(logw)
        beta_ref[d, 0] = beta
        kaph_ref[d, 0] = kap * jnp.exp(cs - logw)
        kch_ref[d, 0] = kmod * inv
        bch_ref[d, 0] = beta * inv
        rh_ref[d, 0] = r * jnp.exp(cs)
        kend_ref[d, 0] = kmod * jnp.exp(tot - cs)


def _rwkv_prep(u_rwkv, p):
    B, L, _ = u_rwkv.shape
    R = RWKV_DIM
    nt = L // ROW_TILE
    halo = ROW_TILE // 8
    n_halo = L // 8
    idx = jnp.arange(ROW_TILE)
    same_chunk = (idx[:, None] // RWKV_CHUNK) == (idx[None, :] // RWKV_CHUNK)
    ones_blk = same_chunk.astype(F32)
    tril = (same_chunk & (idx[None, :] <= idx[:, None])).astype(F32)
    triu = (same_chunk & (idx[None, :] >= idx[:, None])).astype(F32)
    pad_rows = lambda w, d, n: jnp.pad(w, ((d * n, n - d * n), (0, 0)))
    wup = jnp.stack([pad_rows(p['rwkv_w_up'][d], d, DECAY_LORA) for d in range(2)]).astype(BF16)
    aup = jnp.stack([pad_rows(p['rwkv_a_up'][d], d, AAA_LORA) for d in range(2)]).astype(BF16)
    consts = [p['rwkv_conv'], p['rwkv_k_k'][None, :], p['rwkv_k_a'][None, :], p['rwkv_w0'], p['rwkv_a0'],
              wup, aup, p['rwkv_g_up'].astype(BF16), ones_blk, tril, triu]
    full = lambda a: pl.BlockSpec(a.shape, lambda b, j: (0,) * a.ndim)
    tok = pl.BlockSpec((1, ROW_TILE, R), lambda b, j: (b, j, 0))
    per_dir = pl.BlockSpec((2, 1, ROW_TILE, R), lambda b, j: (0, b, j, 0))
    tok_shape = jax.ShapeDtypeStruct((B, L, R), F32)
    dir_shape = jax.ShapeDtypeStruct((2, B, L, R), F32)
    vtp_block = (ROW_TILE // RWKV_CHUNK, RWKV_HEADS // 2, HEAD_DIM, 2 * RWKV_CHUNK)
    return pl.pallas_call(
        _rwkv_prep_kernel,
        grid=(B, nt),
        in_specs=[pl.BlockSpec((1, ROW_TILE, U_RWKV), lambda b, j: (b, j, 0)),
                  pl.BlockSpec((1, 8, U_RWKV), lambda b, j: (b, jnp.maximum(j * halo - 1, 0), 0)),
                  pl.BlockSpec((1, 8, U_RWKV), lambda b, j: (b, jnp.minimum((j + 1) * halo, n_halo - 1), 0))]
                 + [full(a) for a in consts],
        out_specs=[tok, tok, tok, pl.BlockSpec((1,) + vtp_block, lambda b, j: (b, j, 0, 0, 0)), tok, tok]
                  + [per_dir] * 7,
        out_shape=[tok_shape, tok_shape, tok_shape,
                   jax.ShapeDtypeStruct((B, L // RWKV_CHUNK) + vtp_block[1:], F32), tok_shape, tok_shape]
                  + [dir_shape] * 7,
        compiler_params=pltpu.CompilerParams(
            dimension_semantics=("parallel", "parallel"), vmem_limit_bytes=VMEM_LIMIT),
        name="rwkv_prep",
    )(u_rwkv, u_rwkv, u_rwkv, *consts)


RWKV_BATCH = 2
RWKV_ROWS = 2 * RWKV_CHUNK
SUBLANES = 8
RWKV_UNROLL = 8


def _rwkv_scan_kernel(*refs):
    T, N = RWKV_CHUNK, HEAD_DIM
    W = 2 * N
    ins, outs = refs[:20], refs[20:22]
    p_ref, mix_ref, seq_ref, y0_ref, cm_ref = refs[22:]
    n_pair = RWKV_HEADS // 2
    pairs = [(bi, d, hp) for bi in range(RWKV_BATCH) for d in range(2) for hp in range(n_pair)]

    @pl.when(pl.program_id(1) == 0)
    def _():
        p_ref[...] = jnp.zeros_like(p_ref)

    row_t = lax.broadcasted_iota(jnp.int32, (T, W), 0)
    col = lax.broadcasted_iota(jnp.int32, (T, W), 1)
    first = col < N
    col_j = jnp.where(first, col, col - N)
    before = (col_j < row_t, col_j > row_t)
    upto = (col_j <= row_t, col_j >= row_t)
    lane = lax.broadcasted_iota(jnp.int32, (SUBLANES, W), 1)
    low = lane < N
    lane_t = jnp.where(low, lane, lane - N)

    def blockdiag(x):
        return jnp.concatenate([jnp.where(first, x, 0.0), jnp.where(first, 0.0, x)], axis=0)

    for slot in range(2):
        for pc, (bi, d, hp) in enumerate(pairs):
            kap_r, vTp_r, v_r, w_r, beta_r, kaph_r, kch_r, bch_r, rh_r, kend_r = ins[10 * d:10 * d + 10]
            q = slot if d == 0 else 1 - slot
            rows, cols = slice(T * q, T * q + T), slice(W * hp, W * hp + W)
            kch2 = blockdiag(kch_r[0, bi, rows, cols])
            rh = rh_r[0, bi, rows, cols]
            b_m = jnp.where(before[d], _dot_nt(kaph_r[0, bi, rows, cols], kch2), 0.0)
            e_m = jnp.where(upto[d], _dot_nt(rh, kch2), 0.0)
            cm_ref[pc] = jnp.where(upto[d], _dot_nt(rh, blockdiag(bch_r[0, bi, rows, cols])), 0.0)
            mix_ref[pc] = _dot_nt(vTp_r[bi, q, hp], blockdiag(b_m))
            y0_ref[pc] = _dot_nt(rh, blockdiag(p_ref[pc])) + _dot(e_m, blockdiag(v_r[bi, rows, cols]))
            seq_ref[0, pc] = kap_r[bi, rows, cols]
            seq_ref[1, pc] = w_r[0, bi, rows, cols]
            seq_ref[2, pc] = beta_r[0, bi, rows, cols]

        def step(i, carry):
            for pc, (bi, d, hp) in enumerate(pairs):
                t = i if d == 0 else T - 1 - i
                at_t = lane_t == t
                kap_t, w_t, beta_t = (jnp.broadcast_to(seq_ref[op, pc, pl.ds(t, 1), :], (SUBLANES, W))
                                      for op in range(3))
                for s8 in range(N // SUBLANES):
                    sub = slice(SUBLANES * s8, SUBLANES * s8 + SUBLANES)
                    p, mix = p_ref[pc, sub, :], mix_ref[pc, sub, :]
                    prod = p * kap_t + jnp.where(at_t, mix, 0.0)
                    sa = jnp.where(low, jnp.sum(jnp.where(low, prod, 0.0), axis=-1, keepdims=True),
                                   jnp.sum(jnp.where(low, 0.0, prod), axis=-1, keepdims=True))
                    p_ref[pc, sub, :] = p * w_t - sa * beta_t
                    mix_ref[pc, sub, :] = jnp.where(at_t, sa, mix)
            return carry

        lax.fori_loop(0, T, step, 0, unroll=RWKV_UNROLL)

        for bi in range(RWKV_BATCH):
            for d in range(2):
                vTp_r, kend_r = ins[10 * d + 1], ins[10 * d + 9]
                q = slot if d == 0 else 1 - slot
                rows = slice(T * q, T * q + T)
                ys = []
                for hp in range(n_pair):
                    pc = (bi * 2 + d) * n_pair + hp
                    cols = slice(W * hp, W * hp + W)
                    ys.append(y0_ref[pc] - _dot_nt(cm_ref[pc], blockdiag(mix_ref[pc])))
                    p_ref[pc] = p_ref[pc] + _dot(vTp_r[bi, q, hp], blockdiag(kend_r[0, bi, rows, cols]))
                outs[d][bi, rows, :] = jnp.concatenate(ys, axis=-1)


def _rwkv_scan(kap, vTp, v, per_dir):
    B, L, R = kap.shape
    nb = L // RWKV_ROWS
    n_ctx = CTX_LEN // RWKV_ROWS
    chunks = RWKV_ROWS // RWKV_CHUNK
    fwd = lambda c: c
    bwd = lambda c: jnp.where(c < n_ctx, n_ctx - 1 - c, nb + n_ctx - 1 - c)
    in_specs, args = [], []
    for d, blk in enumerate((fwd, bwd)):
        tok = pl.BlockSpec((RWKV_BATCH, RWKV_ROWS, R), lambda b, c, blk=blk: (b, blk(c), 0))
        tr = pl.BlockSpec((RWKV_BATCH, chunks) + vTp.shape[2:], lambda b, c, blk=blk: (b, blk(c), 0, 0, 0))
        dirs = pl.BlockSpec((1, RWKV_BATCH, RWKV_ROWS, R), lambda b, c, blk=blk, d=d: (d, b, blk(c), 0))
        in_specs += [tok, tr, tok] + [dirs] * 7
        args += [kap, vTp, v] + list(per_dir)
    out_specs = [pl.BlockSpec((RWKV_BATCH, RWKV_ROWS, R), lambda b, c, blk=blk: (b, blk(c), 0))
                 for blk in (fwd, bwd)]
    pair_tile = (RWKV_BATCH * RWKV_HEADS, HEAD_DIM, 2 * HEAD_DIM)
    return pl.pallas_call(
        _rwkv_scan_kernel,
        grid=(B // RWKV_BATCH, nb),
        in_specs=in_specs,
        out_specs=out_specs,
        out_shape=[jax.ShapeDtypeStruct((B, L, R), F32)] * 2,
        scratch_shapes=[pltpu.VMEM(pair_tile, F32), pltpu.VMEM(pair_tile, F32), pltpu.VMEM((3,) + pair_tile, F32),
                        pltpu.VMEM(pair_tile, F32), pltpu.VMEM(pair_tile, F32)],
        compiler_params=pltpu.CompilerParams(
            dimension_semantics=("parallel", "arbitrary"), vmem_limit_bytes=VMEM_LIMIT),
        name="rwkv_scan",
    )(*args)


def _rwkv_out_kernel(yf_ref, yb_ref, r_ref, k_ref, v_ref, g_ref, rk_ref, lnw_ref, lnb_ref, ones_ref, o_ref):
    ones_blk = ones_ref[...]
    y = yf_ref[0] + yb_ref[0]
    mu = _dot_hi(y, ones_blk) * (1.0 / HEAD_DIM)
    yc = y - mu
    var = _dot_hi(yc * yc, ones_blk) * (1.0 / HEAD_DIM)
    yn = yc * lax.rsqrt(var + RWKV_GN_EPS) * lnw_ref[...] + lnb_ref[...]
    bonus = _dot_hi(r_ref[0] * k_ref[0] * rk_ref[...], ones_blk) * v_ref[0]
    o_ref[0] = ((yn + bonus) * g_ref[0]).astype(o_ref.dtype)


def _rwkv_out(y_f, y_b, r, k, v, g, p):
    B, L, R = r.shape
    idx = jnp.arange(R)
    ones_blk = ((idx[:, None] // HEAD_DIM) == (idx[None, :] // HEAD_DIM)).astype(F32)
    consts = [p['rwkv_r_k'].reshape(1, R), p['rwkv_ln_w'][None, :], p['rwkv_ln_b'][None, :], ones_blk]
    tok = pl.BlockSpec((1, ROW_TILE, R), lambda b, j: (b, j, 0))
    return pl.pallas_call(
        _rwkv_out_kernel,
        grid=(B, L // ROW_TILE),
        in_specs=[tok] * 6 + [pl.BlockSpec(a.shape, lambda b, j: (0, 0)) for a in consts],
        out_specs=tok,
        out_shape=jax.ShapeDtypeStruct((B, L, R), BF16),
        compiler_params=pltpu.CompilerParams(dimension_semantics=("parallel", "parallel")),
        name="rwkv_out",
    )(y_f, y_b, r, k, v, g, *consts)


MLA_SLOT = 128
MLA_WIDE = MLA_HEADS * MLA_SLOT


def _rope_mla(x, cos, sin_signed):
    half = MLA_ROPE // 2
    lane = lax.broadcasted_iota(jnp.int32, x.shape, 1) % MLA_SLOT
    partner = jnp.where(lane < MLA_NOPE + half, pltpu.roll(x, x.shape[1] - half, 1), pltpu.roll(x, half, 1))
    return x * cos + partner * sin_signed


def _mla_prep_kernel(u_ref, cos_ref, sin_ref, qn_ref, kvn_ref, wq_ref, wk_ref, wv_ref, q_ref, k_ref, v_ref):
    u = u_ref[0]
    q_dn = u[:, :MLA_Q_RANK]
    kv_dn = u[:, MLA_Q_RANK:MLA_Q_RANK + MLA_KV_RANK]
    k_rot = u[:, MLA_Q_RANK + MLA_KV_RANK:]
    norm = lambda t, w: (t * lax.rsqrt(jnp.mean(t * t, axis=-1, keepdims=True) + RMS_EPS) * w).astype(BF16)
    qn, kvn = norm(q_dn, qn_ref[...]), norm(kv_dn, kvn_ref[...])
    cos, sin = cos_ref[...], sin_ref[...]
    wide = lambda t: jnp.concatenate([t] * MLA_HEADS, axis=1)
    q_ref[0] = (_rope_mla(_dot(qn, wq_ref[...]), wide(cos), wide(sin)) * (MLA_QK ** -0.5)).astype(BF16)
    k_ref[0] = (_dot(kvn, wk_ref[...]) + wide(_rope_mla(k_rot, cos, sin))).astype(BF16)
    v = _dot(kvn, wv_ref[...])
    lane = lax.broadcasted_iota(jnp.int32, v.shape, 1) % MLA_SLOT
    v_ref[0] = jnp.where(lane == MLA_V, 1.0, v).astype(BF16)


def _mla_prep(u_mla, cos_rows, sin_rows, p):
    B, L, _ = u_mla.shape
    slots = lambda w, width: jnp.pad(w.reshape(w.shape[0], MLA_HEADS, width),
                                     ((0, 0), (0, 0), (0, MLA_SLOT - width))).reshape(w.shape[0], MLA_WIDE)
    w_ukv = p['mla_w_ukv'].reshape(MLA_KV_RANK, MLA_HEADS, MLA_NOPE + MLA_V)
    consts = [p['mla_q_norm'][None, :], p['mla_kv_norm'][None, :],
              slots(p['mla_w_uq'], MLA_QK).astype(BF16),
              slots(w_ukv[:, :, :MLA_NOPE].reshape(MLA_KV_RANK, -1), MLA_NOPE).astype(BF16),
              slots(w_ukv[:, :, MLA_NOPE:].reshape(MLA_KV_RANK, -1), MLA_V).astype(BF16)]
    tok = lambda w: pl.BlockSpec((1, ROW_TILE, w), lambda b, j: (b, j, 0))
    table = pl.BlockSpec((ROW_TILE, MLA_SLOT), lambda b, j: (j, 0))
    return pl.pallas_call(
        _mla_prep_kernel,
        grid=(B, L // ROW_TILE),
        in_specs=[tok(U_MLA), table, table] + [pl.BlockSpec(a.shape, lambda b, j: (0, 0)) for a in consts],
        out_specs=[tok(MLA_WIDE)] * 3,
        out_shape=[jax.ShapeDtypeStruct((B, L, MLA_WIDE), BF16)] * 3,
        compiler_params=pltpu.CompilerParams(
            dimension_semantics=("parallel", "parallel"), vmem_limit_bytes=VMEM_LIMIT),
        name="mla_prep",
    )(u_mla, cos_rows, sin_rows, *consts)


def _attn_kernel(q_ref, k_ref, v_ref, o_ref):
    def attend(n_keys):
        outs = []
        for h in range(MLA_HEADS):
            cols = slice(MLA_SLOT * h, MLA_SLOT * h + MLA_SLOT)
            s = _dot_nt(q_ref[0, :, cols], k_ref[0, :n_keys, cols])
            p = jnp.exp((s - jnp.max(s, axis=-1, keepdims=True)).astype(BF16))
            o = _dot(p, v_ref[0, :n_keys, cols])
            outs.append(o[:, :MLA_V] / o[:, MLA_V:MLA_V + 1])
        o_ref[0] = jnp.concatenate(outs, axis=-1).astype(o_ref.dtype)

    @pl.when(pl.program_id(1) == 0)
    def _():
        attend(CTX_LEN)

    @pl.when(pl.program_id(1) > 0)
    def _():
        attend(k_ref.shape[1])


def _attention(q, k, v):
    B, L, W = q.shape
    resident = lambda: pl.BlockSpec((1, L, W), lambda b, i: (b, 0, 0), pipeline_mode=pl.Buffered(1))
    return pl.pallas_call(
        _attn_kernel,
        grid=(B, L // ROW_TILE),
        in_specs=[pl.BlockSpec((1, ROW_TILE, W), lambda b, i: (b, i, 0)), resident(), resident()],
        out_specs=pl.BlockSpec((1, ROW_TILE, MLA_DIM), lambda b, i: (b, i, 0)),
        out_shape=jax.ShapeDtypeStruct((B, L, MLA_DIM), BF16),
        compiler_params=pltpu.CompilerParams(
            dimension_semantics=("parallel", "parallel"), vmem_limit_bytes=VMEM_LIMIT),
        name="mla_attention",
    )(q, k, v)


def _rope_rows(x, cos, sin_signed):
    lane = lax.broadcasted_iota(jnp.int32, x.shape, 1)
    partner = jnp.where(lane % RET_KEY < RET_KEY // 2,
                        pltpu.roll(x, x.shape[1] - RET_KEY // 2, 1), pltpu.roll(x, RET_KEY // 2, 1))
    return x * cos + partner * sin_signed


def _ret_kernel(*refs):
    ins, (inner_ref, cross_ref, tailT_ref, cdec_ref), outs, r_ref = refs[:10], refs[10:14], refs[14:16], refs[16]

    @pl.when(pl.program_id(1) == 0)
    def _():
        r_ref[...] = jnp.zeros_like(r_ref)

    for d in range(2):
        q_ref, k_ref, v_ref, cos_ref, sin_ref = ins[5 * d:5 * d + 5]
        cos, sin = cos_ref[...], sin_ref[...]
        q_all = _rope_rows(q_ref[0], cos, sin)
        k_all = _rope_rows(k_ref[0], cos, sin) * (RET_KEY ** -0.5)
        kT_all = k_all.T
        v_all = v_ref[0]
        os = []
        for h in range(RET_HEADS):
            cols = slice(RET_KEY * h, RET_KEY * h + RET_KEY)
            q = q_all[:, cols].astype(BF16)
            v = v_all[:, cols].astype(BF16)
            state = r_ref[d, h]
            s = _dot_nt(q, k_all[:, cols].astype(BF16)) * inner_ref[d, h]
            os.append(_dot(s.astype(BF16), v) + _dot(q, state.astype(BF16)) * cross_ref[d, h])
            r_ref[d, h] = state * cdec_ref[d, h] + _dot((kT_all[cols, :] * tailT_ref[d, h]).astype(BF16), v)
        outs[d][0] = jnp.concatenate(os, axis=-1)


def _retention_scan(u_ret, cos, sin_signed, inner, cross, tailT, cdec):
    B, L, _ = u_ret.shape
    C = RET_CHUNK
    nb = L // C
    n_ctx = CTX_LEN // C
    fwd = lambda c: c
    bwd = lambda c: jnp.where(c < n_ctx, n_ctx - 1 - c, nb + n_ctx - 1 - c)
    in_specs, args = [], []
    for blk in (fwd, bwd):
        in_specs += [pl.BlockSpec((1, C, RET_DIM), lambda b, c, blk=blk, i=i: (b, blk(c), i)) for i in range(3)]
        in_specs += [pl.BlockSpec((C, RET_DIM), lambda b, c, blk=blk: (blk(c), 0))] * 2
        args += [u_ret, u_ret, u_ret, cos, sin_signed]
    consts = [inner, cross, tailT, cdec]
    in_specs += [pl.BlockSpec(a.shape, lambda b, c: (0, 0, 0, 0)) for a in consts]
    return pl.pallas_call(
        _ret_kernel,
        grid=(B, nb),
        in_specs=in_specs,
        out_specs=[pl.BlockSpec((1, C, RET_DIM), lambda b, c, blk=blk: (b, blk(c), 0)) for blk in (fwd, bwd)],
        out_shape=[jax.ShapeDtypeStruct((B, L, RET_DIM), F32)] * 2,
        scratch_shapes=[pltpu.VMEM((2, RET_HEADS, RET_KEY, RET_VAL), F32)],
        compiler_params=pltpu.CompilerParams(
            dimension_semantics=("parallel", "arbitrary"), vmem_limit_bytes=VMEM_LIMIT),
        name="retention_scan",
    )(*args, *consts)


def _ret_out_kernel(of_ref, ob_ref, g_ref, ones_ref, o_ref):
    y = of_ref[0] + ob_ref[0]
    ms = _dot_hi(y * y, ones_ref[...]) * (1.0 / RET_VAL)
    o_ref[0] = (jax.nn.silu(g_ref[0]) * (y * lax.rsqrt(ms + RMS_EPS))).astype(o_ref.dtype)


def _ret_out(o_f, o_b, u_ret):
    B, L, R = o_f.shape
    idx = jnp.arange(R)
    ones_blk = ((idx[:, None] // RET_VAL) == (idx[None, :] // RET_VAL)).astype(F32)
    tok = pl.BlockSpec((1, ROW_TILE, R), lambda b, j: (b, j, 0))
    return pl.pallas_call(
        _ret_out_kernel,
        grid=(B, L // ROW_TILE),
        in_specs=[tok, tok, pl.BlockSpec((1, ROW_TILE, R), lambda b, j: (b, j, 3)),
                  pl.BlockSpec((R, R), lambda b, j: (0, 0))],
        out_specs=tok,
        out_shape=jax.ShapeDtypeStruct((B, L, R), BF16),
        compiler_params=pltpu.CompilerParams(dimension_semantics=("parallel", "parallel")),
        name="ret_out",
    )(o_f, o_b, u_ret, ones_blk)


def _merge_kernel(x_ref, h_ref, ya_ref, yb_ref, yc_ref, mod_ref, wg_ref, bg_ref, wa_ref, wb_ref, wc_ref,
                  wo_ref, n2_ref, wr_ref, br_ref, xo_ref, h2_ref, lg_ref):
    D = D_MODEL
    gates = jax.nn.sigmoid(_dot(h_ref[0], wg_ref[...]) + bg_ref[...])
    m = (gates[:, :D] * _dot(ya_ref[0], wa_ref[...])
         + gates[:, D:2 * D] * _dot(yb_ref[0], wb_ref[...])
         + gates[:, 2 * D:] * _dot(yc_ref[0], wc_ref[...]))
    xn = x_ref[0] + mod_ref[0, 0, 2:3, :] * _dot(m.astype(BF16), wo_ref[...])
    xo_ref[0] = xn
    y = xn * lax.rsqrt(jnp.mean(xn * xn, axis=-1, keepdims=True) + RMS_EPS) * n2_ref[...]
    h2 = y * (1.0 + mod_ref[0, 0, 4:5, :]) + mod_ref[0, 0, 3:4, :]
    h2_hi = h2.astype(BF16)
    h2_ref[0] = h2_hi
    h2_lo = (h2 - h2_hi.astype(F32)).astype(BF16)
    lg_ref[0] = (_dot(h2_hi, wr_ref[0]) + (_dot(h2_lo, wr_ref[0]) + _dot(h2_hi, wr_ref[1]))) + br_ref[...]


def _merge(x_all, h, ya, yb, yc, mods, wg, bg, wa, wb, wc, wo, n2, wr, br, tile_off):
    B, T, D = x_all.shape
    nt = T // ROW_TILE - tile_off
    tok = lambda w: pl.BlockSpec((1, ROW_TILE, w), lambda b, j: (b, j + tile_off, 0))
    out = lambda w: pl.BlockSpec((1, ROW_TILE, w), lambda b, j: (b, j, 0))
    full = lambda a: pl.BlockSpec(a.shape, lambda b, j: (0,) * a.ndim)
    return pl.pallas_call(
        _merge_kernel,
        grid=(B, nt),
        in_specs=[tok(D), tok(D), tok(RWKV_DIM), tok(MLA_DIM), tok(RET_DIM),
                  pl.BlockSpec((1, 1, 8, D), lambda b, j: (b, jnp.minimum(j + tile_off, 1), 0, 0)),
                  full(wg), full(bg), full(wa), full(wb), full(wc), full(wo), full(n2), full(wr), full(br)],
        out_specs=[out(D), out(D), out(ROUTER_LANES)],
        out_shape=[jax.ShapeDtypeStruct((B, nt * ROW_TILE, D), F32),
                   jax.ShapeDtypeStruct((B, nt * ROW_TILE, D), BF16),
                   jax.ShapeDtypeStruct((B, nt * ROW_TILE, ROUTER_LANES), F32)],
        compiler_params=pltpu.CompilerParams(
            dimension_semantics=("parallel", "parallel"), vmem_limit_bytes=VMEM_LIMIT),
        name="merge",
    )(x_all, h, ya, yb, yc, mods, wg, bg, wa, wb, wc, wo, n2, wr, br)


def _moe_kernel(te_ref, x_ref, cw_ref, wg_ref, wu_ref, wd_ref, o_ref):
    del te_ref
    x = x_ref[...]
    hid = jax.nn.silu(_dot(x, wg_ref[0])) * _dot(x, wu_ref[0])
    o_ref[...] = _dot(hid.astype(BF16), wd_ref[0]) * cw_ref[...]


def _moe_experts(tile_expert, x_sorted, cw_sorted, w_gate, w_up, w_down):
    n_rows, D = x_sorted.shape
    n_tiles = n_rows // MOE_TILE
    grid_spec = pltpu.PrefetchScalarGridSpec(
        num_scalar_prefetch=1,
        grid=(n_tiles,),
        in_specs=[pl.BlockSpec((MOE_TILE, D), lambda i, te: (i, 0)),
                  pl.BlockSpec((MOE_TILE, 1), lambda i, te: (i, 0)),
                  pl.BlockSpec((1, D, EXPERT_HIDDEN), lambda i, te: (te[i], 0, 0)),
                  pl.BlockSpec((1, D, EXPERT_HIDDEN), lambda i, te: (te[i], 0, 0)),
                  pl.BlockSpec((1, EXPERT_HIDDEN, D), lambda i, te: (te[i], 0, 0))],
        out_specs=pl.BlockSpec((MOE_TILE, D), lambda i, te: (i, 0)),
    )
    return pl.pallas_call(
        _moe_kernel,
        grid_spec=grid_spec,
        out_shape=jax.ShapeDtypeStruct((n_rows, D), F32),
        compiler_params=pltpu.CompilerParams(
            dimension_semantics=("arbitrary",), vmem_limit_bytes=VMEM_LIMIT),
        name="moe_experts",
    )(tile_expert, x_sorted, cw_sorted, w_gate, w_up, w_down)


def _final_norm_kernel(x_ref, w_ref, o_ref):
    x = x_ref[...]
    o_ref[...] = x * lax.rsqrt(jnp.mean(x * x, axis=-1, keepdims=True) + RMS_EPS) * w_ref[...]


def _final_norm(x, w):
    n_rows, D = x.shape
    tile = 512
    return pl.pallas_call(
        _final_norm_kernel,
        grid=(n_rows // tile,),
        in_specs=[pl.BlockSpec((tile, D), lambda i: (i, 0)), pl.BlockSpec((1, D), lambda i: (0, 0))],
        out_specs=pl.BlockSpec((tile, D), lambda i: (i, 0)),
        out_shape=jax.ShapeDtypeStruct((n_rows, D), F32),
        compiler_params=pltpu.CompilerParams(dimension_semantics=("parallel",)),
        name="final_norm",
    )(x, w)


def _rope_table(n_tokens, rot_dim):
    rows = n_tokens // GRID_W
    row = jnp.repeat(jnp.arange(rows, dtype=F32), GRID_W)
    col = jnp.tile(jnp.arange(GRID_W, dtype=F32), rows)
    n_freq = rot_dim // 4
    inv_freq = ROPE_BASE ** (-jnp.arange(n_freq, dtype=F32) / n_freq)
    ang = jnp.concatenate([row[:, None] * inv_freq, col[:, None] * inv_freq], axis=-1)
    return jnp.cos(ang), jnp.sin(ang)


def _rwkv_branch(u_rwkv, p):
    r, k, v, vTp, kap, g, *per_dir = _rwkv_prep(u_rwkv, p)
    y_f, y_b = _rwkv_scan(kap, vTp, v, per_dir)
    return _rwkv_out(y_f, y_b, r, k, v, g, p)


def _mla_branch(u_mla, p, rope):
    L = u_mla.shape[1]
    cos, sin = rope
    n_ctx = L - cos.shape[0]
    cos = jnp.concatenate([jnp.ones((n_ctx, cos.shape[1]), F32), cos], axis=0)
    sin = jnp.concatenate([jnp.zeros((n_ctx, sin.shape[1]), F32), sin], axis=0)
    tail = MLA_SLOT - MLA_NOPE - MLA_ROPE
    cos_rows = jnp.concatenate([jnp.ones((L, MLA_NOPE), F32), cos, cos, jnp.ones((L, tail), F32)], axis=1)
    sin_rows = jnp.concatenate([jnp.zeros((L, MLA_NOPE), F32), -sin, sin, jnp.zeros((L, tail), F32)], axis=1)
    q, k, v = _mla_prep(u_mla, cos_rows, sin_rows, p)
    return _attention(q, k, v)


def _retention_branch(u_ret, p, rope):
    B, L, _ = u_ret.shape
    C = RET_CHUNK
    cos, sin = rope
    n_ctx = L - cos.shape[0]
    cos = jnp.concatenate([jnp.ones((n_ctx, cos.shape[1]), F32), cos], axis=0)
    sin = jnp.concatenate([jnp.zeros((n_ctx, sin.shape[1]), F32), sin], axis=0)
    cos_rows = jnp.tile(cos, (1, 2 * RET_HEADS))
    sin_rows = jnp.tile(jnp.concatenate([-sin, sin], axis=1), (1, RET_HEADS))
    log_gamma = jax.nn.log_sigmoid(p['ret_decay'])
    idx = jnp.arange(C, dtype=F32)
    rel = idx[:, None] - idx[None, :]
    lg_f, lg_b = log_gamma[0][:, None, None], log_gamma[1][:, None, None]
    inner = jnp.stack([jnp.where(rel[None] >= 0, jnp.exp(lg_f * jnp.maximum(rel, 0.0)[None]), 0.0),
                       jnp.where(rel[None] <= 0, jnp.exp(lg_b * jnp.maximum(-rel, 0.0)[None]), 0.0)])
    cross = jnp.stack([jnp.exp(lg_f * (idx + 1.0)[None, :, None]), jnp.exp(lg_b * (C - idx)[None, :, None])])
    cross = jnp.broadcast_to(cross, (2, RET_HEADS, C, RET_VAL))
    tailT = jnp.stack([jnp.exp(lg_f * (C - 1.0 - idx)[None, None, :]), jnp.exp(lg_b * idx[None, None, :])])
    tailT = jnp.broadcast_to(tailT, (2, RET_HEADS, RET_KEY, C))
    cdec = jnp.broadcast_to(jnp.exp(log_gamma * C)[:, :, None, None], (2, RET_HEADS, RET_KEY, RET_VAL))
    o_f, o_b = _retention_scan(u_ret, cos_rows, sin_rows, inner, cross, tailT, cdec)
    return _ret_out(o_f, o_b, u_ret)


def _route(logits, p):
    n_tok = logits.shape[0]
    grp_prob = jax.nn.softmax(logits[:, :N_GROUPS], axis=-1)
    grp_idx = jnp.argmax(grp_prob, axis=-1)
    grp_p = jnp.max(grp_prob, axis=-1, keepdims=True)
    exp_logits = logits[:, N_GROUPS:N_GROUPS + N_EXPERTS].reshape(n_tok, N_GROUPS, EXPERTS_PER_GROUP)
    grp_hot = grp_idx[:, None] == jnp.arange(N_GROUPS)[None, :]
    in_group = jnp.sum(jnp.where(grp_hot[:, :, None], exp_logits, 0.0), axis=1)
    exp_prob = jax.nn.softmax(in_group, axis=-1)
    lanes = jnp.arange(EXPERTS_PER_GROUP)[None, :]
    e1 = jnp.argmax(exp_prob, axis=-1)
    p1 = jnp.max(exp_prob, axis=-1)
    rest = jnp.where(lanes == e1[:, None], -1.0, exp_prob)
    e2 = jnp.argmax(rest, axis=-1)
    p2 = jnp.max(rest, axis=-1)
    exp_p = jnp.stack([p1, p2], axis=-1)
    exp_idx = jnp.stack([e1, e2], axis=-1)
    weights = grp_p * exp_p / jnp.sum(exp_p, axis=-1, keepdims=True)
    return grp_idx[:, None] * EXPERTS_PER_GROUP + exp_idx, weights


def _moe(h2, logits, p):
    n_tok, D = h2.shape
    expert_id, weights = _route(logits, p)
    n_assign = n_tok * TOP_K
    flat_e = expert_id.reshape(n_assign).astype(jnp.int32)
    flat_w = weights.reshape(n_assign)
    sorted_e, order = lax.sort((flat_e, jnp.arange(n_assign, dtype=jnp.int32)), num_keys=1, is_stable=True)
    bounds = jnp.searchsorted(sorted_e, jnp.arange(N_EXPERTS + 1, dtype=jnp.int32)).astype(jnp.int32)
    raw_start, counts = bounds[:-1], bounds[1:] - bounds[:-1]
    padded = ((counts + MOE_TILE - 1) // MOE_TILE) * MOE_TILE
    pad_end = jnp.cumsum(padded)
    pad_start = pad_end - padded
    rank = jnp.argsort(order).astype(jnp.int32)
    n_tiles = n_assign // MOE_TILE + N_EXPERTS
    tile_start = jnp.arange(n_tiles, dtype=jnp.int32) * MOE_TILE
    tile_expert = jnp.minimum(
        jnp.sum((tile_start[:, None] >= pad_end[None, :]).astype(jnp.int32), axis=1), N_EXPERTS - 1)
    row_e = jnp.repeat(tile_expert, MOE_TILE)
    within = jnp.arange(n_tiles * MOE_TILE, dtype=jnp.int32) - pad_start[row_e]
    valid = within < counts[row_e]
    src_assign = order[jnp.clip(raw_start[row_e] + within, 0, n_assign - 1)]
    src_tok = jnp.where(valid, src_assign // TOP_K, 0)
    cw = jnp.where(valid, flat_w[src_assign], 0.0)
    pos = (pad_start[flat_e] + rank - raw_start[flat_e]).reshape(n_tok, TOP_K)
    out = _moe_experts(tile_expert, h2[src_tok], cw[:, None], p['moe_w_gate'], p['moe_w_up'], p['moe_w_down'])
    return out[pos[:, 0]] + out[pos[:, 1]]


def _permute_w_in(w_in):
    w_a = w_in[:, :U_RWKV]
    lo = U_RWKV + MLA_Q_RANK + MLA_KV_RANK
    w_b = jnp.concatenate([w_in[:, U_RWKV:lo], jnp.pad(w_in[:, lo:lo + MLA_ROPE], ((0, 0), (MLA_NOPE, MLA_SLOT - MLA_NOPE - MLA_ROPE)))], axis=1)
    w_c = w_in[:, U_RWKV + U_MLA_RAW:]
    return w_a.astype(BF16), w_b.astype(BF16), w_c.astype(BF16)


def kernel(x, c, ctx, c_ctx, w_mod, b_mod, norm1_w, w_in, rwkv_conv, rwkv_w0, rwkv_w_up, rwkv_a0, rwkv_a_up, rwkv_g_up, rwkv_k_k, rwkv_k_a, rwkv_r_k, rwkv_ln_w, rwkv_ln_b, mla_q_norm, mla_w_uq, mla_kv_norm, mla_w_ukv, ret_decay, w_branch_a, w_branch_b, w_branch_c, w_branch_gate, b_branch_gate, w_out, norm2_w, moe_w_group, moe_b_group, moe_w_expert, moe_b_expert, moe_w_gate, moe_w_up, moe_w_down, final_norm_w):
    B, n_lat, D = x.shape
    depth = w_mod.shape[0]
    ropes = (_rope_table(n_lat, MLA_ROPE), _rope_table(n_lat, RET_KEY))
    x_all = jnp.concatenate([ctx, x], axis=1)
    L = x_all.shape[1]
    for l in range(depth):
        with_ctx = l < depth - 1
        p = {
            'rwkv_conv': rwkv_conv[l], 'rwkv_w0': rwkv_w0[l], 'rwkv_w_up': rwkv_w_up[l],
            'rwkv_a0': rwkv_a0[l], 'rwkv_a_up': rwkv_a_up[l], 'rwkv_g_up': rwkv_g_up[l],
            'rwkv_k_k': rwkv_k_k[l], 'rwkv_k_a': rwkv_k_a[l], 'rwkv_r_k': rwkv_r_k[l].reshape(RWKV_HEADS, HEAD_DIM),
            'rwkv_ln_w': rwkv_ln_w[l], 'rwkv_ln_b': rwkv_ln_b[l],
            'mla_q_norm': mla_q_norm[l], 'mla_w_uq': mla_w_uq[l], 'mla_kv_norm': mla_kv_norm[l],
            'mla_w_ukv': mla_w_ukv[l], 'ret_decay': ret_decay[l],
            'moe_w_gate': moe_w_gate[l].astype(BF16), 'moe_w_up': moe_w_up[l].astype(BF16),
            'moe_w_down': moe_w_down[l].astype(BF16),
        }
        mod_lat = jax.nn.silu(c) @ w_mod[l] + b_mod[l]
        mod_ctx = jax.nn.silu(c_ctx) @ w_mod[l] + b_mod[l]
        mods = jnp.stack([jnp.broadcast_to(mod_ctx, mod_lat.shape), mod_lat], axis=1).reshape(B, 2, N_MOD, D)
        mods = jnp.pad(mods, ((0, 0), (0, 0), (0, 8 - N_MOD), (0, 0)))

        w_a, w_b, w_c = _permute_w_in(w_in[l])
        h, u_rwkv, u_mla, u_ret = _inproj(x_all, mods, norm1_w[l][None, :], w_a, w_b, w_c)

        ya = _rwkv_branch(u_rwkv, p)
        yb = _mla_branch(u_mla, p, ropes[0])
        yc = _retention_branch(u_ret, p, ropes[1])

        w_router = jnp.pad(jnp.concatenate([moe_w_group[l], moe_w_expert[l]], axis=1),
                           ((0, 0), (0, ROUTER_LANES - N_GROUPS - N_EXPERTS)))
        w_router_hi = w_router.astype(BF16)
        w_router = jnp.stack([w_router_hi, (w_router - w_router_hi.astype(F32)).astype(BF16)])
        b_router = jnp.pad(jnp.concatenate([moe_b_group[l], moe_b_expert[l]]),
                           (0, ROUTER_LANES - N_GROUPS - N_EXPERTS))[None, :]
        tile_off = 0 if with_ctx else CTX_LEN // ROW_TILE
        x_mid, h2, logits = _merge(
            x_all, h, ya, yb, yc, mods, w_branch_gate[l].astype(BF16), b_branch_gate[l][None, :],
            w_branch_a[l].astype(BF16), w_branch_b[l].astype(BF16), w_branch_c[l].astype(BF16),
            w_out[l].astype(BF16), norm2_w[l][None, :], w_router, b_router, tile_off)
        Lm = x_mid.shape[1]
        f = _moe(h2.reshape(B * Lm, D), logits.reshape(B * Lm, ROUTER_LANES), p).reshape(B, Lm, D)
        if with_ctx:
            g2 = jnp.concatenate([jnp.broadcast_to(mods[:, 0:1, 5, :], (B, CTX_LEN, D)),
                                  jnp.broadcast_to(mods[:, 1:2, 5, :], (B, L - CTX_LEN, D))], axis=1)
            x_all = x_mid + g2 * f
        else:
            x_lat = x_mid + mods[:, 1:2, 5, :] * f
    return _final_norm(x_lat.reshape(B * n_lat, D), final_norm_w[None, :]).reshape(B, n_lat, D)
```

```python
import functools

import jax
import jax.numpy as jnp
from jax import lax
from jax.experimental import pallas as pl
from jax.experimental.pallas import tpu as pltpu

D_MODEL = 1024
CTX_LEN = 256
GRID_W = 64
N_MOD = 6
RMS_EPS = 1e-6
ROPE_BASE = 10000.0
HEAD_DIM = 64
F32 = jnp.float32
BF16 = jnp.bfloat16

RWKV_HEADS = 4
RWKV_DIM = RWKV_HEADS * HEAD_DIM
DECAY_LORA = 64
AAA_LORA = 64
GATE_LORA = 128
RWKV_GN_EPS = 64e-5

MLA_HEADS = 8
MLA_NOPE = 64
MLA_ROPE = 32
MLA_QK = MLA_NOPE + MLA_ROPE
MLA_V = 64
MLA_Q_RANK = 256
MLA_KV_RANK = 128
MLA_DIM = MLA_HEADS * MLA_V

RET_HEADS = 4
RET_KEY = 64
RET_VAL = 64
RET_DIM = RET_HEADS * RET_VAL

N_GROUPS = 4
EXPERTS_PER_GROUP = 8
N_EXPERTS = N_GROUPS * EXPERTS_PER_GROUP
TOP_K = 2
EXPERT_HIDDEN = 512

U_RWKV = 3 * RWKV_DIM + 2 * DECAY_LORA + 2 * AAA_LORA + GATE_LORA
U_MLA_RAW = MLA_Q_RANK + MLA_KV_RANK + MLA_ROPE
U_MLA = 512
U_RET = 4 * RET_DIM

ROW_TILE = 256
RWKV_CHUNK = 64
RET_CHUNK = 256
ROUTER_LANES = 128
MOE_TILE = 256
VMEM_LIMIT = 56 * 1024 * 1024


def _dot(a, b):
    return jnp.dot(a, b, preferred_element_type=F32)


def _dot_nt(a, b):
    return lax.dot_general(a, b, (((1,), (1,)), ((), ())), preferred_element_type=F32)


def _inproj_kernel(x_ref, mod_ref, nw_ref, wa_ref, wb_ref, wc_ref, h_ref, ua_ref, ub_ref, uc_ref):
    x = x_ref[0]
    y = x * lax.rsqrt(jnp.mean(x * x, axis=-1, keepdims=True) + RMS_EPS) * nw_ref[...]
    h = y * (1.0 + mod_ref[0, 0, 1:2, :]) + mod_ref[0, 0, 0:1, :]
    hb = h.astype(BF16)
    h_ref[0] = hb
    ua_ref[0] = _dot(hb, wa_ref[...])
    ub_ref[0] = _dot(hb, wb_ref[...])
    uc_ref[0] = _dot(hb, wc_ref[...])


def _inproj(x_all, mods, norm_w, w_a, w_b, w_c):
    B, T, D = x_all.shape
    nt = T // ROW_TILE
    tok = lambda w: pl.BlockSpec((1, ROW_TILE, w), lambda b, j: (b, j, 0))
    full = lambda a: pl.BlockSpec(a.shape, lambda b, j: (0,) * a.ndim)
    return pl.pallas_call(
        _inproj_kernel,
        grid=(B, nt),
        in_specs=[tok(D),
                  pl.BlockSpec((1, 1, 8, D), lambda b, j: (b, jnp.minimum(j, 1), 0, 0)),
                  full(norm_w), full(w_a), full(w_b), full(w_c)],
        out_specs=[tok(D), tok(U_RWKV), tok(U_MLA), tok(U_RET)],
        out_shape=[jax.ShapeDtypeStruct((B, T, D), BF16),
                   jax.ShapeDtypeStruct((B, T, U_RWKV), F32),
                   jax.ShapeDtypeStruct((B, T, U_MLA), F32),
                   jax.ShapeDtypeStruct((B, T, U_RET), F32)],
        compiler_params=pltpu.CompilerParams(
            dimension_semantics=("parallel", "parallel"), vmem_limit_bytes=VMEM_LIMIT),
        name="inproj",
    )(x_all, mods, norm_w, w_a, w_b, w_c)


def _dot_hi(a, b):
    return jnp.dot(a, b, preferred_element_type=F32, precision=lax.Precision.HIGHEST)


def _softplus(z):
    return jnp.maximum(z, 0.0) + jnp.log(1.0 + jnp.exp(-jnp.abs(z)))


def _rwkv_prep_kernel(u_ref, up_ref, un_ref, conv_ref, kk_ref, ka_ref, w0_ref, a0_ref, wup_ref, aup_ref,
                      gup_ref, ones_ref, tril_ref, triu_ref,
                      r_ref, k_ref, v_ref, vTp_ref, kap_ref, g_ref,
                      w_ref, beta_ref, kaph_ref, kch_ref, bch_ref, rh_ref, kend_ref):
    R = RWKV_DIM
    j = pl.program_id(1)
    u = u_ref[0]
    x = u[:, :3 * R]
    prev_row = jnp.where(j >= 2, up_ref[0, 7:8, :3 * R], 0.0)
    next_row = jnp.where((j >= 1) & (j < pl.num_programs(1) - 1), un_ref[0, 0:1, :3 * R], 0.0)
    rid = lax.broadcasted_iota(jnp.int32, (ROW_TILE, 3 * R), 0)
    x_prev = jnp.where(rid == 0, prev_row, pltpu.roll(x, 1, 0))
    x_next = jnp.where(rid == ROW_TILE - 1, next_row, pltpu.roll(x, ROW_TILE - 1, 0))
    rkv = x_prev * conv_ref[0:1, :] + x * conv_ref[1:2, :] + x_next * conv_ref[2:3, :]
    r, k, v = rkv[:, :R], rkv[:, R:2 * R], rkv[:, 2 * R:]
    ones_blk = ones_ref[...]
    kk = k * kk_ref[...]
    kap = kk / jnp.maximum(jnp.sqrt(_dot_hi(kk * kk, ones_blk)), 1e-12)
    r_ref[0] = r
    k_ref[0] = k
    v_ref[0] = v
    vT = v.T
    for c in range(ROW_TILE // RWKV_CHUNK):
        for hp in range(RWKV_HEADS // 2):
            vTp_ref[0, c, hp] = jnp.concatenate(
                [vT[HEAD_DIM * (2 * hp + s):HEAD_DIM * (2 * hp + s + 1), RWKV_CHUNK * c:RWKV_CHUNK * (c + 1)]
                 for s in range(2)], axis=1)
    kap_ref[0] = kap
    lo = 3 * R
    tw = jnp.tanh(u[:, lo:lo + 2 * DECAY_LORA]).astype(BF16)
    ad = u[:, lo + 2 * DECAY_LORA:lo + 2 * DECAY_LORA + 2 * AAA_LORA].astype(BF16)
    g_ref[0] = _dot(jax.nn.sigmoid(u[:, lo + 2 * DECAY_LORA + 2 * AAA_LORA:]).astype(BF16), gup_ref[...])
    for d in range(2):
        log_w = -_softplus(-(w0_ref[d:d + 1, :] + _dot(tw, wup_ref[d]))) - 0.5
        logw = -jnp.exp(log_w)
        a = jax.nn.sigmoid(a0_ref[d:d + 1, :] + _dot(ad, aup_ref[d]))
        kmod = k * (1.0 + (a - 1.0) * ka_ref[...])
        beta = kap * a
        cs = _dot_hi(tril_ref[...] if d == 0 else triu_ref[...], logw)
        tot = _dot_hi(ones_blk, logw)
        inv = jnp.exp(-cs)
        w_ref[d, 0] = jnp.exp(logw)
        beta_ref[d, 0] = beta
        kaph_ref[d, 0] = kap * jnp.exp(cs - logw)
        kch_ref[d, 0] = kmod * inv
        bch_ref[d, 0] = beta * inv
        rh_ref[d, 0] = r * jnp.exp(cs)
        kend_ref[d, 0] = kmod * jnp.exp(tot - cs)


def _rwkv_prep(u_rwkv, p):
    B, L, _ = u_rwkv.shape
    R = RWKV_DIM
    nt = L // ROW_TILE
    halo = ROW_TILE // 8
    n_halo = L // 8
    idx = jnp.arange(ROW_TILE)
    same_chunk = (idx[:, None] // RWKV_CHUNK) == (idx[None, :] // RWKV_CHUNK)
    ones_blk = same_chunk.astype(F32)
    tril = (same_chunk & (idx[None, :] <= idx[:, None])).astype(F32)
    triu = (same_chunk & (idx[None, :] >= idx[:, None])).astype(F32)
    pad_rows = lambda w, d, n: jnp.pad(w, ((d * n, n - d * n), (0, 0)))
    wup = jnp.stack([pad_rows(p['rwkv_w_up'][d], d, DECAY_LORA) for d in range(2)]).astype(BF16)
    aup = jnp.stack([pad_rows(p['rwkv_a_up'][d], d, AAA_LORA) for d in range(2)]).astype(BF16)
    consts = [p['rwkv_conv'], p['rwkv_k_k'][None, :], p['rwkv_k_a'][None, :], p['rwkv_w0'], p['rwkv_a0'],
              wup, aup, p['rwkv_g_up'].astype(BF16), ones_blk, tril, triu]
    full = lambda a: pl.BlockSpec(a.shape, lambda b, j: (0,) * a.ndim)
    tok = pl.BlockSpec((1, ROW_TILE, R), lambda b, j: (b, j, 0))
    per_dir = pl.BlockSpec((2, 1, ROW_TILE, R), lambda b, j: (0, b, j, 0))
    tok_shape = jax.ShapeDtypeStruct((B, L, R), F32)
    dir_shape = jax.ShapeDtypeStruct((2, B, L, R), F32)
    vtp_block = (ROW_TILE // RWKV_CHUNK, RWKV_HEADS // 2, HEAD_DIM, 2 * RWKV_CHUNK)
    return pl.pallas_call(
        _rwkv_prep_kernel,
        grid=(B, nt),
        in_specs=[pl.BlockSpec((1, ROW_TILE, U_RWKV), lambda b, j: (b, j, 0)),
                  pl.BlockSpec((1, 8, U_RWKV), lambda b, j: (b, jnp.maximum(j * halo - 1, 0), 0)),
                  pl.BlockSpec((1, 8, U_RWKV), lambda b, j: (b, jnp.minimum((j + 1) * halo, n_halo - 1), 0))]
                 + [full(a) for a in consts],
        out_specs=[tok, tok, tok, pl.BlockSpec((1,) + vtp_block, lambda b, j: (b, j, 0, 0, 0)), tok, tok]
                  + [per_dir] * 7,
        out_shape=[tok_shape, tok_shape, tok_shape,
                   jax.ShapeDtypeStruct((B, L // RWKV_CHUNK) + vtp_block[1:], F32), tok_shape, tok_shape]
                  + [dir_shape] * 7,
        compiler_params=pltpu.CompilerParams(
            dimension_semantics=("parallel", "parallel"), vmem_limit_bytes=VMEM_LIMIT),
        name="rwkv_prep",
    )(u_rwkv, u_rwkv, u_rwkv, *consts)


RWKV_BATCH = 2
RWKV_ROWS = 2 * RWKV_CHUNK
SUBLANES = 8
RWKV_UNROLL = 8


def _rwkv_scan_kernel(*refs):
    T, N = RWKV_CHUNK, HEAD_DIM
    W = 2 * N
    ins, outs = refs[:20], refs[20:22]
    p_ref, mix_ref, seq_ref, y0_ref, cm_ref = refs[22:]
    n_pair = RWKV_HEADS // 2
    pairs = [(bi, d, hp) for bi in range(RWKV_BATCH) for d in range(2) for hp in range(n_pair)]

    @pl.when(pl.program_id(1) == 0)
    def _():
        p_ref[...] = jnp.zeros_like(p_ref)

    row_t = lax.broadcasted_iota(jnp.int32, (T, W), 0)
    col = lax.broadcasted_iota(jnp.int32, (T, W), 1)
    first = col < N
    col_j = jnp.where(first, col, col - N)
    before = (col_j < row_t, col_j > row_t)
    upto = (col_j <= row_t, col_j >= row_t)
    lane = lax.broadcasted_iota(jnp.int32, (SUBLANES, W), 1)
    low = lane < N
    lane_t = jnp.where(low, lane, lane - N)

    def blockdiag(x):
        return jnp.concatenate([jnp.where(first, x, 0.0), jnp.where(first, 0.0, x)], axis=0)

    for slot in range(2):
        for pc, (bi, d, hp) in enumerate(pairs):
            kap_r, vTp_r, v_r, w_r, beta_r, kaph_r, kch_r, bch_r, rh_r, kend_r = ins[10 * d:10 * d + 10]
            q = slot if d == 0 else 1 - slot
            rows, cols = slice(T * q, T * q + T), slice(W * hp, W * hp + W)
            kch2 = blockdiag(kch_r[0, bi, rows, cols])
            rh = rh_r[0, bi, rows, cols]
            b_m = jnp.where(before[d], _dot_nt(kaph_r[0, bi, rows, cols], kch2), 0.0)
            e_m = jnp.where(upto[d], _dot_nt(rh, kch2), 0.0)
            cm_ref[pc] = jnp.where(upto[d], _dot_nt(rh, blockdiag(bch_r[0, bi, rows, cols])), 0.0)
            mix_ref[pc] = _dot_nt(vTp_r[bi, q, hp], blockdiag(b_m))
            y0_ref[pc] = _dot_nt(rh, blockdiag(p_ref[pc])) + _dot(e_m, blockdiag(v_r[bi, rows, cols]))
            seq_ref[0, pc] = kap_r[bi, rows, cols]
            seq_ref[1, pc] = w_r[0, bi, rows, cols]
            seq_ref[2, pc] = beta_r[0, bi, rows, cols]

        def step(i, carry):
            for pc, (bi, d, hp) in enumerate(pairs):
                t = i if d == 0 else T - 1 - i
                at_t = lane_t == t
                kap_t, w_t, beta_t = (jnp.broadcast_to(seq_ref[op, pc, pl.ds(t, 1), :], (SUBLANES, W))
                                      for op in range(3))
                for s8 in range(N // SUBLANES):
                    sub = slice(SUBLANES * s8, SUBLANES * s8 + SUBLANES)
                    p, mix = p_ref[pc, sub, :], mix_ref[pc, sub, :]
                    prod = p * kap_t + jnp.where(at_t, mix, 0.0)
                    sa = jnp.where(low, jnp.sum(jnp.where(low, prod, 0.0), axis=-1, keepdims=True),
                                   jnp.sum(jnp.where(low, 0.0, prod), axis=-1, keepdims=True))
                    p_ref[pc, sub, :] = p * w_t - sa * beta_t
                    mix_ref[pc, sub, :] = jnp.where(at_t, sa, mix)
            return carry

        lax.fori_loop(0, T, step, 0, unroll=RWKV_UNROLL)

        for bi in range(RWKV_BATCH):
            for d in range(2):
                vTp_r, kend_r = ins[10 * d + 1], ins[10 * d + 9]
                q = slot if d == 0 else 1 - slot
                rows = slice(T * q, T * q + T)
                ys = []
                for hp in range(n_pair):
                    pc = (bi * 2 + d) * n_pair + hp
                    cols = slice(W * hp, W * hp + W)
                    ys.append(y0_ref[pc] - _dot_nt(cm_ref[pc], blockdiag(mix_ref[pc])))
                    p_ref[pc] = p_ref[pc] + _dot(vTp_r[bi, q, hp], blockdiag(kend_r[0, bi, rows, cols]))
                outs[d][bi, rows, :] = jnp.concatenate(ys, axis=-1)


def _rwkv_scan(kap, vTp, v, per_dir):
    B, L, R = kap.shape
    nb = L // RWKV_ROWS
    n_ctx = CTX_LEN // RWKV_ROWS
    chunks = RWKV_ROWS // RWKV_CHUNK
    fwd = lambda c: c
    bwd = lambda c: jnp.where(c < n_ctx, n_ctx - 1 - c, nb + n_ctx - 1 - c)
    in_specs, args = [], []
    for d, blk in enumerate((fwd, bwd)):
        tok = pl.BlockSpec((RWKV_BATCH, RWKV_ROWS, R), lambda b, c, blk=blk: (b, blk(c), 0))
        tr = pl.BlockSpec((RWKV_BATCH, chunks) + vTp.shape[2:], lambda b, c, blk=blk: (b, blk(c), 0, 0, 0))
        dirs = pl.BlockSpec((1, RWKV_BATCH, RWKV_ROWS, R), lambda b, c, blk=blk, d=d: (d, b, blk(c), 0))
        in_specs += [tok, tr, tok] + [dirs] * 7
        args += [kap, vTp, v] + list(per_dir)
    out_specs = [pl.BlockSpec((RWKV_BATCH, RWKV_ROWS, R), lambda b, c, blk=blk: (b, blk(c), 0))
                 for blk in (fwd, bwd)]
    pair_tile = (RWKV_BATCH * RWKV_HEADS, HEAD_DIM, 2 * HEAD_DIM)
    return pl.pallas_call(
        _rwkv_scan_kernel,
        grid=(B // RWKV_BATCH, nb),
        in_specs=in_specs,
        out_specs=out_specs,
        out_shape=[jax.ShapeDtypeStruct((B, L, R), F32)] * 2,
        scratch_shapes=[pltpu.VMEM(pair_tile, F32), pltpu.VMEM(pair_tile, F32), pltpu.VMEM((3,) + pair_tile, F32),
                        pltpu.VMEM(pair_tile, F32), pltpu.VMEM(pair_tile, F32)],
        compiler_params=pltpu.CompilerParams(
            dimension_semantics=("parallel", "arbitrary"), vmem_limit_bytes=VMEM_LIMIT),
        name="rwkv_scan",
    )(*args)


def _rwkv_out_kernel(yf_ref, yb_ref, r_ref, k_ref, v_ref, g_ref, rk_ref, lnw_ref, lnb_ref, ones_ref, o_ref):
    ones_blk = ones_ref[...]
    y = yf_ref[0] + yb_ref[0]
    mu = _dot_hi(y, ones_blk) * (1.0 / HEAD_DIM)
    yc = y - mu
    var = _dot_hi(yc * yc, ones_blk) * (1.0 / HEAD_DIM)
    yn = yc * lax.rsqrt(var + RWKV_GN_EPS) * lnw_ref[...] + lnb_ref[...]
    bonus = _dot_hi(r_ref[0] * k_ref[0] * rk_ref[...], ones_blk) * v_ref[0]
    o_ref[0] = ((yn + bonus) * g_ref[0]).astype(o_ref.dtype)


def _rwkv_out(y_f, y_b, r, k, v, g, p):
    B, L, R = r.shape
    idx = jnp.arange(R)
    ones_blk = ((idx[:, None] // HEAD_DIM) == (idx[None, :] // HEAD_DIM)).astype(F32)
    consts = [p['rwkv_r_k'].reshape(1, R), p['rwkv_ln_w'][None, :], p['rwkv_ln_b'][None, :], ones_blk]
    tok = pl.BlockSpec((1, ROW_TILE, R), lambda b, j: (b, j, 0))
    return pl.pallas_call(
        _rwkv_out_kernel,
        grid=(B, L // ROW_TILE),
        in_specs=[tok] * 6 + [pl.BlockSpec(a.shape, lambda b, j: (0, 0)) for a in consts],
        out_specs=tok,
        out_shape=jax.ShapeDtypeStruct((B, L, R), BF16),
        compiler_params=pltpu.CompilerParams(dimension_semantics=("parallel", "parallel")),
        name="rwkv_out",
    )(y_f, y_b, r, k, v, g, *consts)


MLA_SLOT = 128
MLA_WIDE = MLA_HEADS * MLA_SLOT


def _rope_mla(x, cos, sin_signed):
    half = MLA_ROPE // 2
    lane = lax.broadcasted_iota(jnp.int32, x.shape, 1) % MLA_SLOT
    partner = jnp.where(lane < MLA_NOPE + half, pltpu.roll(x, x.shape[1] - half, 1), pltpu.roll(x, half, 1))
    return x * cos + partner * sin_signed


def _mla_prep_kernel(u_ref, cos_ref, sin_ref, qn_ref, kvn_ref, wq_ref, wk_ref, wv_ref, q_ref, k_ref, v_ref):
    u = u_ref[0]
    q_dn = u[:, :MLA_Q_RANK]
    kv_dn = u[:, MLA_Q_RANK:MLA_Q_RANK + MLA_KV_RANK]
    k_rot = u[:, MLA_Q_RANK + MLA_KV_RANK:]
    norm = lambda t, w: (t * lax.rsqrt(jnp.mean(t * t, axis=-1, keepdims=True) + RMS_EPS) * w).astype(BF16)
    qn, kvn = norm(q_dn, qn_ref[...]), norm(kv_dn, kvn_ref[...])
    cos, sin = cos_ref[...], sin_ref[...]
    wide = lambda t: jnp.concatenate([t] * MLA_HEADS, axis=1)
    q_ref[0] = (_rope_mla(_dot(qn, wq_ref[...]), wide(cos), wide(sin)) * (MLA_QK ** -0.5)).astype(BF16)
    k_ref[0] = (_dot(kvn, wk_ref[...]) + wide(_rope_mla(k_rot, cos, sin))).astype(BF16)
    v = _dot(kvn, wv_ref[...])
    lane = lax.broadcasted_iota(jnp.int32, v.shape, 1) % MLA_SLOT
    v_ref[0] = jnp.where(lane == MLA_V, 1.0, v).astype(BF16)


def _mla_prep(u_mla, cos_rows, sin_rows, p):
    B, L, _ = u_mla.shape
    slots = lambda w, width: jnp.pad(w.reshape(w.shape[0], MLA_HEADS, width),
                                     ((0, 0), (0, 0), (0, MLA_SLOT - width))).reshape(w.shape[0], MLA_WIDE)
    w_ukv = p['mla_w_ukv'].reshape(MLA_KV_RANK, MLA_HEADS, MLA_NOPE + MLA_V)
    consts = [p['mla_q_norm'][None, :], p['mla_kv_norm'][None, :],
              slots(p['mla_w_uq'], MLA_QK).astype(BF16),
              slots(w_ukv[:, :, :MLA_NOPE].reshape(MLA_KV_RANK, -1), MLA_NOPE).astype(BF16),
              slots(w_ukv[:, :, MLA_NOPE:].reshape(MLA_KV_RANK, -1), MLA_V).astype(BF16)]
    tok = lambda w: pl.BlockSpec((1, ROW_TILE, w), lambda b, j: (b, j, 0))
    table = pl.BlockSpec((ROW_TILE, MLA_SLOT), lambda b, j: (j, 0))
    return pl.pallas_call(
        _mla_prep_kernel,
        grid=(B, L // ROW_TILE),
        in_specs=[tok(U_MLA), table, table] + [pl.BlockSpec(a.shape, lambda b, j: (0, 0)) for a in consts],
        out_specs=[tok(MLA_WIDE)] * 3,
        out_shape=[jax.ShapeDtypeStruct((B, L, MLA_WIDE), BF16)] * 3,
        compiler_params=pltpu.CompilerParams(
            dimension_semantics=("parallel", "parallel"), vmem_limit_bytes=VMEM_LIMIT),
        name="mla_prep",
    )(u_mla, cos_rows, sin_rows, *consts)


ATTN_Q_TILES = 2


def _attn_kernel(*refs):
    q_refs, (k_ref, v_ref, o_ref) = refs[:-3], refs[-3:]
    outs = []
    for h in range(MLA_HEADS):
        cols = slice(MLA_SLOT * h, MLA_SLOT * h + MLA_SLOT)
        q = jnp.concatenate([r[0, :, cols] for r in q_refs], axis=0)
        s = _dot_nt(q, k_ref[0, :, cols])
        p = jnp.exp((s - jnp.max(s, axis=-1, keepdims=True)).astype(BF16))
        o = _dot(p, v_ref[0, :, cols])
        outs.append(o[:, :MLA_V] / o[:, MLA_V:MLA_V + 1])
    o_ref[0] = jnp.concatenate(outs, axis=-1).astype(o_ref.dtype)


def _attention(q, k, v):
    B, L, W = q.shape
    params = pltpu.CompilerParams(dimension_semantics=("parallel", "parallel"), vmem_limit_bytes=VMEM_LIMIT)
    ctx_blk = lambda w: pl.BlockSpec((1, CTX_LEN, w), lambda b, i: (b, 0, 0))
    y_ctx = pl.pallas_call(
        _attn_kernel, grid=(B, 1), in_specs=[ctx_blk(W)] * 3, out_specs=ctx_blk(MLA_DIM),
        out_shape=jax.ShapeDtypeStruct((B, CTX_LEN, MLA_DIM), BF16), compiler_params=params,
        name="mla_attention_ctx",
    )(q, k, v)
    n_ctx = CTX_LEN // ROW_TILE
    rows = ATTN_Q_TILES * ROW_TILE
    resident = lambda: pl.BlockSpec((1, L, W), lambda b, i: (b, 0, 0), pipeline_mode=pl.Buffered(1))
    q_specs = [pl.BlockSpec((1, ROW_TILE, W), lambda b, i, t=t: (b, ATTN_Q_TILES * i + n_ctx + t, 0))
               for t in range(ATTN_Q_TILES)]
    y_lat = pl.pallas_call(
        _attn_kernel, grid=(B, (L - CTX_LEN) // rows), in_specs=q_specs + [resident(), resident()],
        out_specs=pl.BlockSpec((1, rows, MLA_DIM), lambda b, i: (b, i, 0)),
        out_shape=jax.ShapeDtypeStruct((B, L - CTX_LEN, MLA_DIM), BF16), compiler_params=params,
        name="mla_attention",
    )(*([q] * ATTN_Q_TILES), k, v)
    return jnp.concatenate([y_ctx, y_lat], axis=1)


def _rope_rows(x, cos, sin_signed):
    lane = lax.broadcasted_iota(jnp.int32, x.shape, 1)
    partner = jnp.where(lane % RET_KEY < RET_KEY // 2,
                        pltpu.roll(x, x.shape[1] - RET_KEY // 2, 1), pltpu.roll(x, RET_KEY // 2, 1))
    return x * cos + partner * sin_signed


def _ret_kernel(*refs):
    ins, (inner_ref, cross_ref, tailT_ref, cdec_ref), outs, r_ref = refs[:10], refs[10:14], refs[14:16], refs[16]

    @pl.when(pl.program_id(1) == 0)
    def _():
        r_ref[...] = jnp.zeros_like(r_ref)

    for d in range(2):
        q_ref, k_ref, v_ref, cos_ref, sin_ref = ins[5 * d:5 * d + 5]
        cos, sin = cos_ref[...], sin_ref[...]
        q_all = _rope_rows(q_ref[0], cos, sin)
        k_all = _rope_rows(k_ref[0], cos, sin) * (RET_KEY ** -0.5)
        kT_all = k_all.T
        v_all = v_ref[0]
        os = []
        for h in range(RET_HEADS):
            cols = slice(RET_KEY * h, RET_KEY * h + RET_KEY)
            q = q_all[:, cols].astype(BF16)
            v = v_all[:, cols].astype(BF16)
            state = r_ref[d, h]
            s = _dot_nt(q, k_all[:, cols].astype(BF16)) * inner_ref[d, h]
            os.append(_dot(s.astype(BF16), v) + _dot(q, state.astype(BF16)) * cross_ref[d, h])
            r_ref[d, h] = state * cdec_ref[d, h] + _dot((kT_all[cols, :] * tailT_ref[d, h]).astype(BF16), v)
        outs[d][0] = jnp.concatenate(os, axis=-1)


def _retention_scan(u_ret, cos, sin_signed, inner, cross, tailT, cdec):
    B, L, _ = u_ret.shape
    C = RET_CHUNK
    nb = L // C
    n_ctx = CTX_LEN // C
    fwd = lambda c: c
    bwd = lambda c: jnp.where(c < n_ctx, n_ctx - 1 - c, nb + n_ctx - 1 - c)
    in_specs, args = [], []
    for blk in (fwd, bwd):
        in_specs += [pl.BlockSpec((1, C, RET_DIM), lambda b, c, blk=blk, i=i: (b, blk(c), i)) for i in range(3)]
        in_specs += [pl.BlockSpec((C, RET_DIM), lambda b, c, blk=blk: (blk(c), 0))] * 2
        args += [u_ret, u_ret, u_ret, cos, sin_signed]
    consts = [inner, cross, tailT, cdec]
    in_specs += [pl.BlockSpec(a.shape, lambda b, c: (0, 0, 0, 0)) for a in consts]
    return pl.pallas_call(
        _ret_kernel,
        grid=(B, nb),
        in_specs=in_specs,
        out_specs=[pl.BlockSpec((1, C, RET_DIM), lambda b, c, blk=blk: (b, blk(c), 0)) for blk in (fwd, bwd)],
        out_shape=[jax.ShapeDtypeStruct((B, L, RET_DIM), F32)] * 2,
        scratch_shapes=[pltpu.VMEM((2, RET_HEADS, RET_KEY, RET_VAL), F32)],
        compiler_params=pltpu.CompilerParams(
            dimension_semantics=("parallel", "arbitrary"), vmem_limit_bytes=VMEM_LIMIT),
        name="retention_scan",
    )(*args, *consts)


def _ret_out_kernel(of_ref, ob_ref, g_ref, ones_ref, o_ref):
    y = of_ref[0] + ob_ref[0]
    ms = _dot_hi(y * y, ones_ref[...]) * (1.0 / RET_VAL)
    o_ref[0] = (jax.nn.silu(g_ref[0]) * (y * lax.rsqrt(ms + RMS_EPS))).astype(o_ref.dtype)


def _ret_out(o_f, o_b, u_ret):
    B, L, R = o_f.shape
    idx = jnp.arange(R)
    ones_blk = ((idx[:, None] // RET_VAL) == (idx[None, :] // RET_VAL)).astype(F32)
    tok = pl.BlockSpec((1, ROW_TILE, R), lambda b, j: (b, j, 0))
    return pl.pallas_call(
        _ret_out_kernel,
        grid=(B, L // ROW_TILE),
        in_specs=[tok, tok, pl.BlockSpec((1, ROW_TILE, R), lambda b, j: (b, j, 3)),
                  pl.BlockSpec((R, R), lambda b, j: (0, 0))],
        out_specs=tok,
        out_shape=jax.ShapeDtypeStruct((B, L, R), BF16),
        compiler_params=pltpu.CompilerParams(dimension_semantics=("parallel", "parallel")),
        name="ret_out",
    )(o_f, o_b, u_ret, ones_blk)


def _route_rows(lg):
    lane = lax.broadcasted_iota(jnp.int32, lg.shape, 1).astype(F32)
    low = -3.0e38
    first_at = lambda t, m: jnp.min(jnp.where(t == m, lane, float(ROUTER_LANES)), axis=-1, keepdims=True)
    is_grp = lane < N_GROUPS
    gl = jnp.where(is_grp, lg, low)
    gmax = jnp.max(gl, axis=-1, keepdims=True)
    grp = first_at(gl, gmax)
    grp_p = 1.0 / jnp.sum(jnp.where(is_grp, jnp.exp(gl - gmax), 0.0), axis=-1, keepdims=True)
    lo = N_GROUPS + EXPERTS_PER_GROUP * grp
    in_grp = (lane >= lo) & (lane < lo + EXPERTS_PER_GROUP)
    el = jnp.where(in_grp, lg, low)
    emax = jnp.max(el, axis=-1, keepdims=True)
    e1 = first_at(el, emax)
    esum = jnp.sum(jnp.where(in_grp, jnp.exp(el - emax), 0.0), axis=-1, keepdims=True)
    rest = jnp.where(lane == e1, low, el)
    emax2 = jnp.max(rest, axis=-1, keepdims=True)
    e2 = first_at(rest, emax2)
    p1 = 1.0 / esum
    p2 = jnp.exp(emax2 - emax) / esum
    w1 = grp_p * p1 / (p1 + p2)
    w2 = grp_p * p2 / (p1 + p2)
    return jnp.where(lane == 0, e1 - N_GROUPS,
                     jnp.where(lane == 1, e2 - N_GROUPS, jnp.where(lane == 2, w1, jnp.where(lane == 3, w2, 0.0))))


def _merge_kernel(x_ref, h_ref, ya_ref, yb_ref, yc_ref, mod_ref, wg_ref, bg_ref, wa_ref, wb_ref, wc_ref,
                  wo_ref, n2_ref, wr_ref, br_ref, xo_ref, h2_ref, lg_ref):
    D = D_MODEL
    gates = jax.nn.sigmoid(_dot(h_ref[0], wg_ref[...]) + bg_ref[...])
    m = (gates[:, :D] * _dot(ya_ref[0], wa_ref[...])
         + gates[:, D:2 * D] * _dot(yb_ref[0], wb_ref[...])
         + gates[:, 2 * D:] * _dot(yc_ref[0], wc_ref[...]))
    xn = x_ref[0] + mod_ref[0, 0, 2:3, :] * _dot(m.astype(BF16), wo_ref[...])
    xo_ref[0] = xn
    y = xn * lax.rsqrt(jnp.mean(xn * xn, axis=-1, keepdims=True) + RMS_EPS) * n2_ref[...]
    h2 = y * (1.0 + mod_ref[0, 0, 4:5, :]) + mod_ref[0, 0, 3:4, :]
    h2_hi = h2.astype(BF16)
    h2_ref[0] = h2_hi
    h2_lo = (h2 - h2_hi.astype(F32)).astype(BF16)
    lg_ref[0] = _route_rows(
        (_dot(h2_hi, wr_ref[0]) + (_dot(h2_lo, wr_ref[0]) + _dot(h2_hi, wr_ref[1]))) + br_ref[...])


def _merge(x_all, h, ya, yb, yc, mods, wg, bg, wa, wb, wc, wo, n2, wr, br, tile_off):
    B, T, D = x_all.shape
    nt = T // ROW_TILE - tile_off
    tok = lambda w: pl.BlockSpec((1, ROW_TILE, w), lambda b, j: (b, j + tile_off, 0))
    out = lambda w: pl.BlockSpec((1, ROW_TILE, w), lambda b, j: (b, j, 0))
    full = lambda a: pl.BlockSpec(a.shape, lambda b, j: (0,) * a.ndim)
    return pl.pallas_call(
        _merge_kernel,
        grid=(B, nt),
        in_specs=[tok(D), tok(D), tok(RWKV_DIM), tok(MLA_DIM), tok(RET_DIM),
                  pl.BlockSpec((1, 1, 8, D), lambda b, j: (b, jnp.minimum(j + tile_off, 1), 0, 0)),
                  full(wg), full(bg), full(wa), full(wb), full(wc), full(wo), full(n2), full(wr), full(br)],
        out_specs=[out(D), out(D), out(ROUTER_LANES)],
        out_shape=[jax.ShapeDtypeStruct((B, nt * ROW_TILE, D), F32),
                   jax.ShapeDtypeStruct((B, nt * ROW_TILE, D), BF16),
                   jax.ShapeDtypeStruct((B, nt * ROW_TILE, ROUTER_LANES), F32)],
        compiler_params=pltpu.CompilerParams(
            dimension_semantics=("parallel", "parallel"), vmem_limit_bytes=VMEM_LIMIT),
        name="merge",
    )(x_all, h, ya, yb, yc, mods, wg, bg, wa, wb, wc, wo, n2, wr, br)


def _moe_kernel(te_ref, x_ref, cw_ref, wg_ref, wu_ref, wd_ref, o_ref):
    del te_ref
    x = x_ref[...]
    hid = jax.nn.silu(_dot(x, wg_ref[0])) * _dot(x, wu_ref[0])
    o_ref[...] = _dot(hid.astype(BF16), wd_ref[0]) * cw_ref[...]


def _moe_experts(tile_expert, x_sorted, cw_sorted, w_gate, w_up, w_down):
    n_rows, D = x_sorted.shape
    n_tiles = n_rows // MOE_TILE
    grid_spec = pltpu.PrefetchScalarGridSpec(
        num_scalar_prefetch=1,
        grid=(n_tiles,),
        in_specs=[pl.BlockSpec((MOE_TILE, D), lambda i, te: (i, 0)),
                  pl.BlockSpec((MOE_TILE, 1), lambda i, te: (i, 0)),
                  pl.BlockSpec((1, D, EXPERT_HIDDEN), lambda i, te: (te[i], 0, 0)),
                  pl.BlockSpec((1, D, EXPERT_HIDDEN), lambda i, te: (te[i], 0, 0)),
                  pl.BlockSpec((1, EXPERT_HIDDEN, D), lambda i, te: (te[i], 0, 0))],
        out_specs=pl.BlockSpec((MOE_TILE, D), lambda i, te: (i, 0)),
    )
    return pl.pallas_call(
        _moe_kernel,
        grid_spec=grid_spec,
        out_shape=jax.ShapeDtypeStruct((n_rows, D), F32),
        compiler_params=pltpu.CompilerParams(
            dimension_semantics=("arbitrary",), vmem_limit_bytes=VMEM_LIMIT),
        name="moe_experts",
    )(tile_expert, x_sorted, cw_sorted, w_gate, w_up, w_down)


def _final_norm_kernel(x_ref, w_ref, o_ref):
    x = x_ref[...]
    o_ref[...] = x * lax.rsqrt(jnp.mean(x * x, axis=-1, keepdims=True) + RMS_EPS) * w_ref[...]


def _final_norm(x, w):
    n_rows, D = x.shape
    tile = 512
    return pl.pallas_call(
        _final_norm_kernel,
        grid=(n_rows // tile,),
        in_specs=[pl.BlockSpec((tile, D), lambda i: (i, 0)), pl.BlockSpec((1, D), lambda i: (0, 0))],
        out_specs=pl.BlockSpec((tile, D), lambda i: (i, 0)),
        out_shape=jax.ShapeDtypeStruct((n_rows, D), F32),
        compiler_params=pltpu.CompilerParams(dimension_semantics=("parallel",)),
        name="final_norm",
    )(x, w)


def _rope_table(n_tokens, rot_dim):
    rows = n_tokens // GRID_W
    row = jnp.repeat(jnp.arange(rows, dtype=F32), GRID_W)
    col = jnp.tile(jnp.arange(GRID_W, dtype=F32), rows)
    n_freq = rot_dim // 4
    inv_freq = ROPE_BASE ** (-jnp.arange(n_freq, dtype=F32) / n_freq)
    ang = jnp.concatenate([row[:, None] * inv_freq, col[:, None] * inv_freq], axis=-1)
    return jnp.cos(ang), jnp.sin(ang)


def _rwkv_branch(u_rwkv, p):
    r, k, v, vTp, kap, g, *per_dir = _rwkv_prep(u_rwkv, p)
    y_f, y_b = _rwkv_scan(kap, vTp, v, per_dir)
    return _rwkv_out(y_f, y_b, r, k, v, g, p)


def _mla_branch(u_mla, p, rope):
    L = u_mla.shape[1]
    cos, sin = rope
    n_ctx = L - cos.shape[0]
    cos = jnp.concatenate([jnp.ones((n_ctx, cos.shape[1]), F32), cos], axis=0)
    sin = jnp.concatenate([jnp.zeros((n_ctx, sin.shape[1]), F32), sin], axis=0)
    tail = MLA_SLOT - MLA_NOPE - MLA_ROPE
    cos_rows = jnp.concatenate([jnp.ones((L, MLA_NOPE), F32), cos, cos, jnp.ones((L, tail), F32)], axis=1)
    sin_rows = jnp.concatenate([jnp.zeros((L, MLA_NOPE), F32), -sin, sin, jnp.zeros((L, tail), F32)], axis=1)
    q, k, v = _mla_prep(u_mla, cos_rows, sin_rows, p)
    return _attention(q, k, v)


def _retention_branch(u_ret, p, rope):
    B, L, _ = u_ret.shape
    C = RET_CHUNK
    cos, sin = rope
    n_ctx = L - cos.shape[0]
    cos = jnp.concatenate([jnp.ones((n_ctx, cos.shape[1]), F32), cos], axis=0)
    sin = jnp.concatenate([jnp.zeros((n_ctx, sin.shape[1]), F32), sin], axis=0)
    cos_rows = jnp.tile(cos, (1, 2 * RET_HEADS))
    sin_rows = jnp.tile(jnp.concatenate([-sin, sin], axis=1), (1, RET_HEADS))
    log_gamma = jax.nn.log_sigmoid(p['ret_decay'])
    idx = jnp.arange(C, dtype=F32)
    rel = idx[:, None] - idx[None, :]
    lg_f, lg_b = log_gamma[0][:, None, None], log_gamma[1][:, None, None]
    inner = jnp.stack([jnp.where(rel[None] >= 0, jnp.exp(lg_f * jnp.maximum(rel, 0.0)[None]), 0.0),
                       jnp.where(rel[None] <= 0, jnp.exp(lg_b * jnp.maximum(-rel, 0.0)[None]), 0.0)])
    cross = jnp.stack([jnp.exp(lg_f * (idx + 1.0)[None, :, None]), jnp.exp(lg_b * (C - idx)[None, :, None])])
    cross = jnp.broadcast_to(cross, (2, RET_HEADS, C, RET_VAL))
    tailT = jnp.stack([jnp.exp(lg_f * (C - 1.0 - idx)[None, None, :]), jnp.exp(lg_b * idx[None, None, :])])
    tailT = jnp.broadcast_to(tailT, (2, RET_HEADS, RET_KEY, C))
    cdec = jnp.broadcast_to(jnp.exp(log_gamma * C)[:, :, None, None], (2, RET_HEADS, RET_KEY, RET_VAL))
    o_f, o_b = _retention_scan(u_ret, cos_rows, sin_rows, inner, cross, tailT, cdec)
    return _ret_out(o_f, o_b, u_ret)


def _moe(h2, routed, p):
    n_tok, D = h2.shape
    expert_id, weights = routed[:, :TOP_K].astype(jnp.int32), routed[:, TOP_K:2 * TOP_K]
    n_assign = n_tok * TOP_K
    flat_e = expert_id.reshape(n_assign).astype(jnp.int32)
    flat_w = weights.reshape(n_assign)
    sorted_e, order = lax.sort((flat_e, jnp.arange(n_assign, dtype=jnp.int32)), num_keys=1, is_stable=True)
    bounds = jnp.searchsorted(sorted_e, jnp.arange(N_EXPERTS + 1, dtype=jnp.int32)).astype(jnp.int32)
    raw_start, counts = bounds[:-1], bounds[1:] - bounds[:-1]
    padded = ((counts + MOE_TILE - 1) // MOE_TILE) * MOE_TILE
    pad_end = jnp.cumsum(padded)
    pad_start = pad_end - padded
    rank = jnp.argsort(order).astype(jnp.int32)
    n_tiles = n_assign // MOE_TILE + N_EXPERTS
    tile_start = jnp.arange(n_tiles, dtype=jnp.int32) * MOE_TILE
    tile_expert = jnp.minimum(
        jnp.sum((tile_start[:, None] >= pad_end[None, :]).astype(jnp.int32), axis=1), N_EXPERTS - 1)
    row_e = jnp.repeat(tile_expert, MOE_TILE)
    within = jnp.arange(n_tiles * MOE_TILE, dtype=jnp.int32) - pad_start[row_e]
    valid = within < counts[row_e]
    src_assign = order[jnp.clip(raw_start[row_e] + within, 0, n_assign - 1)]
    src_tok = jnp.where(valid, src_assign // TOP_K, 0)
    cw = jnp.where(valid, flat_w[src_assign], 0.0)
    pos = (pad_start[flat_e] + rank - raw_start[flat_e]).reshape(n_tok, TOP_K)
    out = _moe_experts(tile_expert, h2[src_tok], cw[:, None], p['moe_w_gate'], p['moe_w_up'], p['moe_w_down'])
    return out[pos[:, 0]] + out[pos[:, 1]]


def _permute_w_in(w_in):
    w_a = w_in[:, :U_RWKV]
    lo = U_RWKV + MLA_Q_RANK + MLA_KV_RANK
    w_b = jnp.concatenate([w_in[:, U_RWKV:lo], jnp.pad(w_in[:, lo:lo + MLA_ROPE], ((0, 0), (MLA_NOPE, MLA_SLOT - MLA_NOPE - MLA_ROPE)))], axis=1)
    w_c = w_in[:, U_RWKV + U_MLA_RAW:]
    return w_a.astype(BF16), w_b.astype(BF16), w_c.astype(BF16)


def kernel(x, c, ctx, c_ctx, w_mod, b_mod, norm1_w, w_in, rwkv_conv, rwkv_w0, rwkv_w_up, rwkv_a0, rwkv_a_up, rwkv_g_up, rwkv_k_k, rwkv_k_a, rwkv_r_k, rwkv_ln_w, rwkv_ln_b, mla_q_norm, mla_w_uq, mla_kv_norm, mla_w_ukv, ret_decay, w_branch_a, w_branch_b, w_branch_c, w_branch_gate, b_branch_gate, w_out, norm2_w, moe_w_group, moe_b_group, moe_w_expert, moe_b_expert, moe_w_gate, moe_w_up, moe_w_down, final_norm_w):
    B, n_lat, D = x.shape
    depth = w_mod.shape[0]
    ropes = (_rope_table(n_lat, MLA_ROPE), _rope_table(n_lat, RET_KEY))
    x_all = jnp.concatenate([ctx, x], axis=1)
    L = x_all.shape[1]
    for l in range(depth):
        with_ctx = l < depth - 1
        p = {
            'rwkv_conv': rwkv_conv[l], 'rwkv_w0': rwkv_w0[l], 'rwkv_w_up': rwkv_w_up[l],
            'rwkv_a0': rwkv_a0[l], 'rwkv_a_up': rwkv_a_up[l], 'rwkv_g_up': rwkv_g_up[l],
            'rwkv_k_k': rwkv_k_k[l], 'rwkv_k_a': rwkv_k_a[l], 'rwkv_r_k': rwkv_r_k[l].reshape(RWKV_HEADS, HEAD_DIM),
            'rwkv_ln_w': rwkv_ln_w[l], 'rwkv_ln_b': rwkv_ln_b[l],
            'mla_q_norm': mla_q_norm[l], 'mla_w_uq': mla_w_uq[l], 'mla_kv_norm': mla_kv_norm[l],
            'mla_w_ukv': mla_w_ukv[l], 'ret_decay': ret_decay[l],
            'moe_w_gate': moe_w_gate[l].astype(BF16), 'moe_w_up': moe_w_up[l].astype(BF16),
            'moe_w_down': moe_w_down[l].astype(BF16),
        }
        mod_lat = jax.nn.silu(c) @ w_mod[l] + b_mod[l]
        mod_ctx = jax.nn.silu(c_ctx) @ w_mod[l] + b_mod[l]
        mods = jnp.stack([jnp.broadcast_to(mod_ctx, mod_lat.shape), mod_lat], axis=1).reshape(B, 2, N_MOD, D)
        mods = jnp.pad(mods, ((0, 0), (0, 0), (0, 8 - N_MOD), (0, 0)))

        w_a, w_b, w_c = _permute_w_in(w_in[l])
        h, u_rwkv, u_mla, u_ret = _inproj(x_all, mods, norm1_w[l][None, :], w_a, w_b, w_c)

        ya = _rwkv_branch(u_rwkv, p)
        yb = _mla_branch(u_mla, p, ropes[0])
        yc = _retention_branch(u_ret, p, ropes[1])

        w_router = jnp.pad(jnp.concatenate([moe_w_group[l], moe_w_expert[l]], axis=1),
                           ((0, 0), (0, ROUTER_LANES - N_GROUPS - N_EXPERTS)))
        w_router_hi = w_router.astype(BF16)
        w_router = jnp.stack([w_router_hi, (w_router - w_router_hi.astype(F32)).astype(BF16)])
        b_router = jnp.pad(jnp.concatenate([moe_b_group[l], moe_b_expert[l]]),
                           (0, ROUTER_LANES - N_GROUPS - N_EXPERTS))[None, :]
        tile_off = 0 if with_ctx else CTX_LEN // ROW_TILE
        x_mid, h2, logits = _merge(
            x_all, h, ya, yb, yc, mods, w_branch_gate[l].astype(BF16), b_branch_gate[l][None, :],
            w_branch_a[l].astype(BF16), w_branch_b[l].astype(BF16), w_branch_c[l].astype(BF16),
            w_out[l].astype(BF16), norm2_w[l][None, :], w_router, b_router, tile_off)
        Lm = x_mid.shape[1]
        f = _moe(h2.reshape(B * Lm, D), logits.reshape(B * Lm, ROUTER_LANES), p).reshape(B, Lm, D)
        if with_ctx:
            g2 = jnp.concatenate([jnp.broadcast_to(mods[:, 0:1, 5, :], (B, CTX_LEN, D)),
                                  jnp.broadcast_to(mods[:, 1:2, 5, :], (B, L - CTX_LEN, D))], axis=1)
            x_all = x_mid + g2 * f
        else:
            x_lat = x_mid + mods[:, 1:2, 5, :] * f
    return _final_norm(x_lat.reshape(B * n_lat, D), final_norm_w[None, :]).reshape(B, n_lat, D)
```

```python
import functools

import jax
import jax.numpy as jnp
from jax import lax
from jax.experimental import pallas as pl
from jax.experimental.pallas import tpu as pltpu

D_MODEL = 1024
CTX_LEN = 256
GRID_W = 64
N_MOD = 6
RMS_EPS = 1e-6
ROPE_BASE = 10000.0
HEAD_DIM = 64
F32 = jnp.float32
BF16 = jnp.bfloat16

RWKV_HEADS = 4
RWKV_DIM = RWKV_HEADS * HEAD_DIM
DECAY_LORA = 64
AAA_LORA = 64
GATE_LORA = 128
RWKV_GN_EPS = 64e-5

MLA_HEADS = 8
MLA_NOPE = 64
MLA_ROPE = 32
MLA_QK = MLA_NOPE + MLA_ROPE
MLA_V = 64
MLA_Q_RANK = 256
MLA_KV_RANK = 128
MLA_DIM = MLA_HEADS * MLA_V

RET_HEADS = 4
RET_KEY = 64
RET_VAL = 64
RET_DIM = RET_HEADS * RET_VAL

N_GROUPS = 4
EXPERTS_PER_GROUP = 8
N_EXPERTS = N_GROUPS * EXPERTS_PER_GROUP
TOP_K = 2
EXPERT_HIDDEN = 512

U_RWKV = 3 * RWKV_DIM + 2 * DECAY_LORA + 2 * AAA_LORA + GATE_LORA
U_MLA_RAW = MLA_Q_RANK + MLA_KV_RANK + MLA_ROPE
U_MLA = 512
U_RET = 4 * RET_DIM

ROW_TILE = 256
RWKV_CHUNK = 64
RET_CHUNK = 256
ROUTER_LANES = 128
MOE_TILE = 256
VMEM_LIMIT = 56 * 1024 * 1024


def _dot(a, b):
    return jnp.dot(a, b, preferred_element_type=F32)


def _dot_nt(a, b):
    return lax.dot_general(a, b, (((1,), (1,)), ((), ())), preferred_element_type=F32)


def _inproj_kernel(x_ref, mod_ref, nw_ref, wa_ref, wb_ref, wc_ref, h_ref, ua_ref, ub_ref, uc_ref):
    x = x_ref[0]
    y = x * lax.rsqrt(jnp.mean(x * x, axis=-1, keepdims=True) + RMS_EPS) * nw_ref[...]
    h = y * (1.0 + mod_ref[0, 0, 1:2, :]) + mod_ref[0, 0, 0:1, :]
    hb = h.astype(BF16)
    h_ref[0] = hb
    ua_ref[0] = _dot(hb, wa_ref[...])
    ub_ref[0] = _dot(hb, wb_ref[...])
    uc_ref[0] = _dot(hb, wc_ref[...])


def _inproj(x_all, mods, norm_w, w_a, w_b, w_c):
    B, T, D = x_all.shape
    nt = T // ROW_TILE
    tok = lambda w: pl.BlockSpec((1, ROW_TILE, w), lambda b, j: (b, j, 0))
    full = lambda a: pl.BlockSpec(a.shape, lambda b, j: (0,) * a.ndim)
    return pl.pallas_call(
        _inproj_kernel,
        grid=(B, nt),
        in_specs=[tok(D),
                  pl.BlockSpec((1, 1, 8, D), lambda b, j: (b, jnp.minimum(j, 1), 0, 0)),
                  full(norm_w), full(w_a), full(w_b), full(w_c)],
        out_specs=[tok(D), tok(U_RWKV), tok(U_MLA), tok(U_RET)],
        out_shape=[jax.ShapeDtypeStruct((B, T, D), BF16),
                   jax.ShapeDtypeStruct((B, T, U_RWKV), F32),
                   jax.ShapeDtypeStruct((B, T, U_MLA), F32),
                   jax.ShapeDtypeStruct((B, T, U_RET), F32)],
        compiler_params=pltpu.CompilerParams(
            dimension_semantics=("parallel", "parallel"), vmem_limit_bytes=VMEM_LIMIT),
        name="inproj",
    )(x_all, mods, norm_w, w_a, w_b, w_c)


def _split3(x):
    hi = x.astype(BF16)
    rem = x - hi.astype(F32)
    mid = rem.astype(BF16)
    return hi, mid, (rem - mid.astype(F32)).astype(BF16)


def _sums_right(x, mask01):
    hi, mid, lo = _split3(x)
    return _dot(hi, mask01) + (_dot(mid, mask01) + _dot(lo, mask01))


def _softplus(z):
    return jnp.maximum(z, 0.0) + jnp.log(1.0 + jnp.exp(-jnp.abs(z)))


def _rwkv_prep_kernel(u_ref, up_ref, un_ref, conv_ref, kk_ref, ka_ref, w0_ref, a0_ref, wup_ref, aup_ref,
                      gup_ref, ones_ref, tril_ref, triu_ref,
                      r_ref, k_ref, v_ref, vTp_ref, kap_ref, g_ref,
                      w_ref, beta_ref, kaph_ref, kch_ref, bch_ref, rh_ref, kend_ref):
    R = RWKV_DIM
    j = pl.program_id(1)
    u = u_ref[0]
    x = u[:, :3 * R]
    prev_row = jnp.where(j >= 2, up_ref[0, 7:8, :3 * R], 0.0)
    next_row = jnp.where((j >= 1) & (j < pl.num_programs(1) - 1), un_ref[0, 0:1, :3 * R], 0.0)
    rid = lax.broadcasted_iota(jnp.int32, (ROW_TILE, 3 * R), 0)
    x_prev = jnp.where(rid == 0, prev_row, pltpu.roll(x, 1, 0))
    x_next = jnp.where(rid == ROW_TILE - 1, next_row, pltpu.roll(x, ROW_TILE - 1, 0))
    rkv = x_prev * conv_ref[0:1, :] + x * conv_ref[1:2, :] + x_next * conv_ref[2:3, :]
    r, k, v = rkv[:, :R], rkv[:, R:2 * R], rkv[:, 2 * R:]
    ones_blk = ones_ref[...]
    kk = k * kk_ref[...]
    kap = kk / jnp.maximum(jnp.sqrt(_sums_right(kk * kk, ones_blk)), 1e-12)
    r_ref[0] = r
    k_ref[0] = k
    v_ref[0] = v
    vT = v.T
    for c in range(ROW_TILE // RWKV_CHUNK):
        for hp in range(RWKV_HEADS // 2):
            vTp_ref[0, c, hp] = jnp.concatenate(
                [vT[HEAD_DIM * (2 * hp + s):HEAD_DIM * (2 * hp + s + 1), RWKV_CHUNK * c:RWKV_CHUNK * (c + 1)]
                 for s in range(2)], axis=1)
    kap_ref[0] = kap
    lo = 3 * R
    tw = jnp.tanh(u[:, lo:lo + 2 * DECAY_LORA]).astype(BF16)
    ad = u[:, lo + 2 * DECAY_LORA:lo + 2 * DECAY_LORA + 2 * AAA_LORA].astype(BF16)
    g_ref[0] = _dot(jax.nn.sigmoid(u[:, lo + 2 * DECAY_LORA + 2 * AAA_LORA:]).astype(BF16), gup_ref[...])
    for d in range(2):
        log_w = -_softplus(-(w0_ref[d:d + 1, :] + _dot(tw, wup_ref[d]))) - 0.5
        logw = -jnp.exp(log_w)
        a = jax.nn.sigmoid(a0_ref[d:d + 1, :] + _dot(ad, aup_ref[d]))
        kmod = k * (1.0 + (a - 1.0) * ka_ref[...])
        beta = kap * a
        hi, mid, lo = _split3(logw)
        tri = tril_ref[...] if d == 0 else triu_ref[...]
        cs = _dot(tri, hi) + (_dot(tri, mid) + _dot(tri, lo))
        tot = _dot(ones_blk, hi) + (_dot(ones_blk, mid) + _dot(ones_blk, lo))
        inv = jnp.exp(-cs)
        w_ref[d, 0] = jnp.exp(logw)
        beta_ref[d, 0] = beta
        kaph_ref[d, 0] = kap * jnp.exp(cs - logw)
        kch_ref[d, 0] = kmod * inv
        bch_ref[d, 0] = beta * inv
        rh_ref[d, 0] = r * jnp.exp(cs)
        kend_ref[d, 0] = kmod * jnp.exp(tot - cs)


def _rwkv_prep(u_rwkv, p):
    B, L, _ = u_rwkv.shape
    R = RWKV_DIM
    nt = L // ROW_TILE
    halo = ROW_TILE // 8
    n_halo = L // 8
    idx = jnp.arange(ROW_TILE)
    same_chunk = (idx[:, None] // RWKV_CHUNK) == (idx[None, :] // RWKV_CHUNK)
    ones_blk = same_chunk.astype(BF16)
    tril = (same_chunk & (idx[None, :] <= idx[:, None])).astype(BF16)
    triu = (same_chunk & (idx[None, :] >= idx[:, None])).astype(BF16)
    pad_rows = lambda w, d, n: jnp.pad(w, ((d * n, n - d * n), (0, 0)))
    wup = jnp.stack([pad_rows(p['rwkv_w_up'][d], d, DECAY_LORA) for d in range(2)]).astype(BF16)
    aup = jnp.stack([pad_rows(p['rwkv_a_up'][d], d, AAA_LORA) for d in range(2)]).astype(BF16)
    consts = [p['rwkv_conv'], p['rwkv_k_k'][None, :], p['rwkv_k_a'][None, :], p['rwkv_w0'], p['rwkv_a0'],
              wup, aup, p['rwkv_g_up'].astype(BF16), ones_blk, tril, triu]
    full = lambda a: pl.BlockSpec(a.shape, lambda b, j: (0,) * a.ndim)
    tok = pl.BlockSpec((1, ROW_TILE, R), lambda b, j: (b, j, 0))
    per_dir = pl.BlockSpec((2, 1, ROW_TILE, R), lambda b, j: (0, b, j, 0))
    tok_shape = jax.ShapeDtypeStruct((B, L, R), F32)
    dir_shape = jax.ShapeDtypeStruct((2, B, L, R), F32)
    vtp_block = (ROW_TILE // RWKV_CHUNK, RWKV_HEADS // 2, HEAD_DIM, 2 * RWKV_CHUNK)
    return pl.pallas_call(
        _rwkv_prep_kernel,
        grid=(B, nt),
        in_specs=[pl.BlockSpec((1, ROW_TILE, U_RWKV), lambda b, j: (b, j, 0)),
                  pl.BlockSpec((1, 8, U_RWKV), lambda b, j: (b, jnp.maximum(j * halo - 1, 0), 0)),
                  pl.BlockSpec((1, 8, U_RWKV), lambda b, j: (b, jnp.minimum((j + 1) * halo, n_halo - 1), 0))]
                 + [full(a) for a in consts],
        out_specs=[tok, tok, tok, pl.BlockSpec((1,) + vtp_block, lambda b, j: (b, j, 0, 0, 0)), tok, tok]
                  + [per_dir] * 7,
        out_shape=[tok_shape, tok_shape, tok_shape,
                   jax.ShapeDtypeStruct((B, L // RWKV_CHUNK) + vtp_block[1:], F32), tok_shape, tok_shape]
                  + [dir_shape] * 7,
        compiler_params=pltpu.CompilerParams(
            dimension_semantics=("parallel", "parallel"), vmem_limit_bytes=VMEM_LIMIT),
        name="rwkv_prep",
    )(u_rwkv, u_rwkv, u_rwkv, *consts)


RWKV_BATCH = 2
RWKV_ROWS = 2 * RWKV_CHUNK
SUBLANES = 8
RWKV_UNROLL = 8


def _rwkv_scan_kernel(*refs):
    T, N = RWKV_CHUNK, HEAD_DIM
    W = 2 * N
    ins, outs = refs[:20], refs[20:22]
    p_ref, mix_ref, seq_ref, y0_ref, cm_ref = refs[22:]
    n_pair = RWKV_HEADS // 2
    pairs = [(bi, d, hp) for bi in range(RWKV_BATCH) for d in range(2) for hp in range(n_pair)]

    @pl.when(pl.program_id(1) == 0)
    def _():
        p_ref[...] = jnp.zeros_like(p_ref)

    row_t = lax.broadcasted_iota(jnp.int32, (T, W), 0)
    col = lax.broadcasted_iota(jnp.int32, (T, W), 1)
    first = col < N
    col_j = jnp.where(first, col, col - N)
    before = (col_j < row_t, col_j > row_t)
    upto = (col_j <= row_t, col_j >= row_t)
    lane = lax.broadcasted_iota(jnp.int32, (SUBLANES, W), 1)
    low = lane < N
    lane_t = jnp.where(low, lane, lane - N)

    def blockdiag(x):
        return jnp.concatenate([jnp.where(first, x, 0.0), jnp.where(first, 0.0, x)], axis=0)

    for slot in range(2):
        for pc, (bi, d, hp) in enumerate(pairs):
            kap_r, vTp_r, v_r, w_r, beta_r, kaph_r, kch_r, bch_r, rh_r, kend_r = ins[10 * d:10 * d + 10]
            q = slot if d == 0 else 1 - slot
            rows, cols = slice(T * q, T * q + T), slice(W * hp, W * hp + W)
            kch2 = blockdiag(kch_r[0, bi, rows, cols])
            rh = rh_r[0, bi, rows, cols]
            b_m = jnp.where(before[d], _dot_nt(kaph_r[0, bi, rows, cols], kch2), 0.0)
            e_m = jnp.where(upto[d], _dot_nt(rh, kch2), 0.0)
            cm_ref[pc] = jnp.where(upto[d], _dot_nt(rh, blockdiag(bch_r[0, bi, rows, cols])), 0.0)
            mix_ref[pc] = _dot_nt(vTp_r[bi, q, hp], blockdiag(b_m))
            y0_ref[pc] = _dot_nt(rh, blockdiag(p_ref[pc])) + _dot(e_m, blockdiag(v_r[bi, rows, cols]))
            seq_ref[0, pc] = kap_r[bi, rows, cols]
            seq_ref[1, pc] = w_r[0, bi, rows, cols]
            seq_ref[2, pc] = beta_r[0, bi, rows, cols]

        def step(i, carry):
            for pc, (bi, d, hp) in enumerate(pairs):
                t = i if d == 0 else T - 1 - i
                at_t = lane_t == t
                kap_t, w_t, beta_t = (jnp.broadcast_to(seq_ref[op, pc, pl.ds(t, 1), :], (SUBLANES, W))
                                      for op in range(3))
                for s8 in range(N // SUBLANES):
                    sub = slice(SUBLANES * s8, SUBLANES * s8 + SUBLANES)
                    p, mix = p_ref[pc, sub, :], mix_ref[pc, sub, :]
                    prod = p * kap_t + jnp.where(at_t, mix, 0.0)
                    sa = jnp.where(low, jnp.sum(jnp.where(low, prod, 0.0), axis=-1, keepdims=True),
                                   jnp.sum(jnp.where(low, 0.0, prod), axis=-1, keepdims=True))
                    p_ref[pc, sub, :] = p * w_t - sa * beta_t
                    mix_ref[pc, sub, :] = jnp.where(at_t, sa, mix)
            return carry

        lax.fori_loop(0, T, step, 0, unroll=RWKV_UNROLL)

        for bi in range(RWKV_BATCH):
            for d in range(2):
                vTp_r, kend_r = ins[10 * d + 1], ins[10 * d + 9]
                q = slot if d == 0 else 1 - slot
                rows = slice(T * q, T * q + T)
                ys = []
                for hp in range(n_pair):
                    pc = (bi * 2 + d) * n_pair + hp
                    cols = slice(W * hp, W * hp + W)
                    ys.append(y0_ref[pc] - _dot_nt(cm_ref[pc], blockdiag(mix_ref[pc])))
                    p_ref[pc] = p_ref[pc] + _dot(vTp_r[bi, q, hp], blockdiag(kend_r[0, bi, rows, cols]))
                outs[d][bi, rows, :] = jnp.concatenate(ys, axis=-1)


def _rwkv_scan(kap, vTp, v, per_dir):
    B, L, R = kap.shape
    nb = L // RWKV_ROWS
    n_ctx = CTX_LEN // RWKV_ROWS
    chunks = RWKV_ROWS // RWKV_CHUNK
    fwd = lambda c: c
    bwd = lambda c: jnp.where(c < n_ctx, n_ctx - 1 - c, nb + n_ctx - 1 - c)
    in_specs, args = [], []
    for d, blk in enumerate((fwd, bwd)):
        tok = pl.BlockSpec((RWKV_BATCH, RWKV_ROWS, R), lambda b, c, blk=blk: (b, blk(c), 0))
        tr = pl.BlockSpec((RWKV_BATCH, chunks) + vTp.shape[2:], lambda b, c, blk=blk: (b, blk(c), 0, 0, 0))
        dirs = pl.BlockSpec((1, RWKV_BATCH, RWKV_ROWS, R), lambda b, c, blk=blk, d=d: (d, b, blk(c), 0))
        in_specs += [tok, tr, tok] + [dirs] * 7
        args += [kap, vTp, v] + list(per_dir)
    out_specs = [pl.BlockSpec((RWKV_BATCH, RWKV_ROWS, R), lambda b, c, blk=blk: (b, blk(c), 0))
                 for blk in (fwd, bwd)]
    pair_tile = (RWKV_BATCH * RWKV_HEADS, HEAD_DIM, 2 * HEAD_DIM)
    return pl.pallas_call(
        _rwkv_scan_kernel,
        grid=(B // RWKV_BATCH, nb),
        in_specs=in_specs,
        out_specs=out_specs,
        out_shape=[jax.ShapeDtypeStruct((B, L, R), F32)] * 2,
        scratch_shapes=[pltpu.VMEM(pair_tile, F32), pltpu.VMEM(pair_tile, F32), pltpu.VMEM((3,) + pair_tile, F32),
                        pltpu.VMEM(pair_tile, F32), pltpu.VMEM(pair_tile, F32)],
        compiler_params=pltpu.CompilerParams(
            dimension_semantics=("parallel", "arbitrary"), vmem_limit_bytes=VMEM_LIMIT),
        name="rwkv_scan",
    )(*args)


def _rwkv_out_rows(y, r, k, v, g, rk, lnw, lnb, ones_blk):
    mu = _sums_right(y, ones_blk) * (1.0 / HEAD_DIM)
    yc = y - mu
    var = _sums_right(yc * yc, ones_blk) * (1.0 / HEAD_DIM)
    yn = yc * lax.rsqrt(var + RWKV_GN_EPS) * lnw + lnb
    bonus = _sums_right(r * k * rk, ones_blk) * v
    return ((yn + bonus) * g).astype(BF16)


MLA_SLOT = 128
MLA_WIDE = MLA_HEADS * MLA_SLOT


def _rope_mla(x, cos, sin_signed):
    half = MLA_ROPE // 2
    lane = lax.broadcasted_iota(jnp.int32, x.shape, 1) % MLA_SLOT
    partner = jnp.where(lane < MLA_NOPE + half, pltpu.roll(x, x.shape[1] - half, 1), pltpu.roll(x, half, 1))
    return x * cos + partner * sin_signed


def _mla_prep_kernel(u_ref, cos_ref, sin_ref, qn_ref, kvn_ref, wq_ref, wk_ref, wv_ref, q_ref, k_ref, v_ref):
    u = u_ref[0]
    q_dn = u[:, :MLA_Q_RANK]
    kv_dn = u[:, MLA_Q_RANK:MLA_Q_RANK + MLA_KV_RANK]
    k_rot = u[:, MLA_Q_RANK + MLA_KV_RANK:]
    norm = lambda t, w: (t * lax.rsqrt(jnp.mean(t * t, axis=-1, keepdims=True) + RMS_EPS) * w).astype(BF16)
    qn, kvn = norm(q_dn, qn_ref[...]), norm(kv_dn, kvn_ref[...])
    cos, sin = cos_ref[...], sin_ref[...]
    wide = lambda t: jnp.concatenate([t] * MLA_HEADS, axis=1)
    q_ref[0] = (_rope_mla(_dot(qn, wq_ref[...]), wide(cos), wide(sin)) * (MLA_QK ** -0.5)).astype(BF16)
    k_ref[0] = (_dot(kvn, wk_ref[...]) + wide(_rope_mla(k_rot, cos, sin))).astype(BF16)
    v = _dot(kvn, wv_ref[...])
    lane = lax.broadcasted_iota(jnp.int32, v.shape, 1) % MLA_SLOT
    v_ref[0] = jnp.where(lane == MLA_V, 1.0, v).astype(BF16)


def _mla_prep(u_mla, cos_rows, sin_rows, p):
    B, L, _ = u_mla.shape
    slots = lambda w, width: jnp.pad(w.reshape(w.shape[0], MLA_HEADS, width),
                                     ((0, 0), (0, 0), (0, MLA_SLOT - width))).reshape(w.shape[0], MLA_WIDE)
    w_ukv = p['mla_w_ukv'].reshape(MLA_KV_RANK, MLA_HEADS, MLA_NOPE + MLA_V)
    consts = [p['mla_q_norm'][None, :], p['mla_kv_norm'][None, :],
              slots(p['mla_w_uq'], MLA_QK).astype(BF16),
              slots(w_ukv[:, :, :MLA_NOPE].reshape(MLA_KV_RANK, -1), MLA_NOPE).astype(BF16),
              slots(w_ukv[:, :, MLA_NOPE:].reshape(MLA_KV_RANK, -1), MLA_V).astype(BF16)]
    tok = lambda w: pl.BlockSpec((1, ROW_TILE, w), lambda b, j: (b, j, 0))
    table = pl.BlockSpec((ROW_TILE, MLA_SLOT), lambda b, j: (j, 0))
    return pl.pallas_call(
        _mla_prep_kernel,
        grid=(B, L // ROW_TILE),
        in_specs=[tok(U_MLA), table, table] + [pl.BlockSpec(a.shape, lambda b, j: (0, 0)) for a in consts],
        out_specs=[tok(MLA_WIDE)] * 3,
        out_shape=[jax.ShapeDtypeStruct((B, L, MLA_WIDE), BF16)] * 3,
        compiler_params=pltpu.CompilerParams(
            dimension_semantics=("parallel", "parallel"), vmem_limit_bytes=VMEM_LIMIT),
        name="mla_prep",
    )(u_mla, cos_rows, sin_rows, *consts)


ATTN_Q_TILES = 2


def _attn_kernel(*refs):
    q_refs, (k_ref, v_ref, o_ref) = refs[:-3], refs[-3:]
    outs = []
    for h in range(MLA_HEADS):
        cols = slice(MLA_SLOT * h, MLA_SLOT * h + MLA_SLOT)
        q = jnp.concatenate([r[0, :, cols] for r in q_refs], axis=0)
        s = _dot_nt(q, k_ref[0, :, cols])
        p = jnp.exp((s - jnp.max(s, axis=-1, keepdims=True)).astype(BF16))
        o = _dot(p, v_ref[0, :, cols])
        outs.append(o[:, :MLA_V] / o[:, MLA_V:MLA_V + 1])
    o_ref[0] = jnp.concatenate(outs, axis=-1).astype(o_ref.dtype)


def _attention(q, k, v):
    B, L, W = q.shape
    params = pltpu.CompilerParams(dimension_semantics=("parallel", "parallel"), vmem_limit_bytes=VMEM_LIMIT)
    ctx_blk = lambda w: pl.BlockSpec((1, CTX_LEN, w), lambda b, i: (b, 0, 0))
    y_ctx = pl.pallas_call(
        _attn_kernel, grid=(B, 1), in_specs=[ctx_blk(W)] * 3, out_specs=ctx_blk(MLA_DIM),
        out_shape=jax.ShapeDtypeStruct((B, CTX_LEN, MLA_DIM), BF16), compiler_params=params,
        name="mla_attention_ctx",
    )(q, k, v)
    n_ctx = CTX_LEN // ROW_TILE
    rows = ATTN_Q_TILES * ROW_TILE
    resident = lambda: pl.BlockSpec((1, L, W), lambda b, i: (b, 0, 0), pipeline_mode=pl.Buffered(1))
    q_specs = [pl.BlockSpec((1, ROW_TILE, W), lambda b, i, t=t: (b, ATTN_Q_TILES * i + n_ctx + t, 0))
               for t in range(ATTN_Q_TILES)]
    y_lat = pl.pallas_call(
        _attn_kernel, grid=(B, (L - CTX_LEN) // rows), in_specs=q_specs + [resident(), resident()],
        out_specs=pl.BlockSpec((1, rows, MLA_DIM), lambda b, i: (b, i, 0)),
        out_shape=jax.ShapeDtypeStruct((B, L - CTX_LEN, MLA_DIM), BF16), compiler_params=params,
        name="mla_attention",
    )(*([q] * ATTN_Q_TILES), k, v)
    return jnp.concatenate([y_ctx, y_lat], axis=1)


def _rope_rows(x, cos, sin_signed):
    lane = lax.broadcasted_iota(jnp.int32, x.shape, 1)
    partner = jnp.where(lane % RET_KEY < RET_KEY // 2,
                        pltpu.roll(x, x.shape[1] - RET_KEY // 2, 1), pltpu.roll(x, RET_KEY // 2, 1))
    return x * cos + partner * sin_signed


def _ret_kernel(*refs):
    ins, (inner_ref, cross_ref, tailT_ref, cdec_ref), outs, r_ref = refs[:10], refs[10:14], refs[14:16], refs[16]

    @pl.when(pl.program_id(1) == 0)
    def _():
        r_ref[...] = jnp.zeros_like(r_ref)

    for d in range(2):
        q_ref, k_ref, v_ref, cos_ref, sin_ref = ins[5 * d:5 * d + 5]
        cos, sin = cos_ref[...], sin_ref[...]
        q_all = _rope_rows(q_ref[0], cos, sin)
        k_all = _rope_rows(k_ref[0], cos, sin) * (RET_KEY ** -0.5)
        kT_all = k_all.T
        v_all = v_ref[0]
        os = []
        for h in range(RET_HEADS):
            cols = slice(RET_KEY * h, RET_KEY * h + RET_KEY)
            q = q_all[:, cols].astype(BF16)
            v = v_all[:, cols].astype(BF16)
            state = r_ref[d, h]
            s = _dot_nt(q, k_all[:, cols].astype(BF16)) * inner_ref[d, h]
            os.append(_dot(s.astype(BF16), v) + _dot(q, state.astype(BF16)) * cross_ref[d, h])
            r_ref[d, h] = state * cdec_ref[d, h] + _dot((kT_all[cols, :] * tailT_ref[d, h]).astype(BF16), v)
        outs[d][0] = jnp.concatenate(os, axis=-1)


def _retention_scan(u_ret, cos, sin_signed, inner, cross, tailT, cdec):
    B, L, _ = u_ret.shape
    C = RET_CHUNK
    nb = L // C
    n_ctx = CTX_LEN // C
    fwd = lambda c: c
    bwd = lambda c: jnp.where(c < n_ctx, n_ctx - 1 - c, nb + n_ctx - 1 - c)
    in_specs, args = [], []
    for blk in (fwd, bwd):
        in_specs += [pl.BlockSpec((1, C, RET_DIM), lambda b, c, blk=blk, i=i: (b, blk(c), i)) for i in range(3)]
        in_specs += [pl.BlockSpec((C, RET_DIM), lambda b, c, blk=blk: (blk(c), 0))] * 2
        args += [u_ret, u_ret, u_ret, cos, sin_signed]
    consts = [inner, cross, tailT, cdec]
    in_specs += [pl.BlockSpec(a.shape, lambda b, c: (0, 0, 0, 0)) for a in consts]
    return pl.pallas_call(
        _ret_kernel,
        grid=(B, nb),
        in_specs=in_specs,
        out_specs=[pl.BlockSpec((1, C, RET_DIM), lambda b, c, blk=blk: (b, blk(c), 0)) for blk in (fwd, bwd)],
        out_shape=[jax.ShapeDtypeStruct((B, L, RET_DIM), F32)] * 2,
        scratch_shapes=[pltpu.VMEM((2, RET_HEADS, RET_KEY, RET_VAL), F32)],
        compiler_params=pltpu.CompilerParams(
            dimension_semantics=("parallel", "arbitrary"), vmem_limit_bytes=VMEM_LIMIT),
        name="retention_scan",
    )(*args, *consts)


def _ret_out_rows(y, g, ones_blk):
    ms = _sums_right(y * y, ones_blk) * (1.0 / RET_VAL)
    return (jax.nn.silu(g) * (y * lax.rsqrt(ms + RMS_EPS))).astype(BF16)


def _route_rows(lg):
    lane = lax.broadcasted_iota(jnp.int32, lg.shape, 1).astype(F32)
    low = -3.0e38
    first_at = lambda t, m: jnp.min(jnp.where(t == m, lane, float(ROUTER_LANES)), axis=-1, keepdims=True)
    is_grp = lane < N_GROUPS
    gl = jnp.where(is_grp, lg, low)
    gmax = jnp.max(gl, axis=-1, keepdims=True)
    grp = first_at(gl, gmax)
    grp_p = 1.0 / jnp.sum(jnp.where(is_grp, jnp.exp(gl - gmax), 0.0), axis=-1, keepdims=True)
    lo = N_GROUPS + EXPERTS_PER_GROUP * grp
    in_grp = (lane >= lo) & (lane < lo + EXPERTS_PER_GROUP)
    el = jnp.where(in_grp, lg, low)
    emax = jnp.max(el, axis=-1, keepdims=True)
    e1 = first_at(el, emax)
    esum = jnp.sum(jnp.where(in_grp, jnp.exp(el - emax), 0.0), axis=-1, keepdims=True)
    rest = jnp.where(lane == e1, low, el)
    emax2 = jnp.max(rest, axis=-1, keepdims=True)
    e2 = first_at(rest, emax2)
    p1 = 1.0 / esum
    p2 = jnp.exp(emax2 - emax) / esum
    w1 = grp_p * p1 / (p1 + p2)
    w2 = grp_p * p2 / (p1 + p2)
    return jnp.where(lane == 0, e1 - N_GROUPS,
                     jnp.where(lane == 1, e2 - N_GROUPS, jnp.where(lane == 2, w1, jnp.where(lane == 3, w2, 0.0))))


def _merge_kernel(x_ref, h_ref, yb_ref, mod_ref,
                  rf_ref, rb_ref, rr_ref, rk_ref, rv_ref, rg_ref, rwk_ref, lnw_ref, lnb_ref, ones_ref,
                  of_ref, ob_ref, og_ref,
                  wg_ref, bg_ref, wa_ref, wb_ref, wc_ref, wo_ref, n2_ref, wr_ref, br_ref,
                  xo_ref, h2_ref, lg_ref):
    D = D_MODEL
    ones_blk = ones_ref[...]
    ya = _rwkv_out_rows(rf_ref[0] + rb_ref[0], rr_ref[0], rk_ref[0], rv_ref[0], rg_ref[0],
                        rwk_ref[...], lnw_ref[...], lnb_ref[...], ones_blk)
    yc = _ret_out_rows(of_ref[0] + ob_ref[0], og_ref[0], ones_blk)
    gates = jax.nn.sigmoid(_dot(h_ref[0], wg_ref[...]) + bg_ref[...])
    m = (gates[:, :D] * _dot(ya, wa_ref[...])
         + gates[:, D:2 * D] * _dot(yb_ref[0], wb_ref[...])
         + gates[:, 2 * D:] * _dot(yc, wc_ref[...]))
    xn = x_ref[0] + mod_ref[0, 0, 2:3, :] * _dot(m.astype(BF16), wo_ref[...])
    xo_ref[0] = xn
    y = xn * lax.rsqrt(jnp.mean(xn * xn, axis=-1, keepdims=True) + RMS_EPS) * n2_ref[...]
    h2 = y * (1.0 + mod_ref[0, 0, 4:5, :]) + mod_ref[0, 0, 3:4, :]
    h2_hi = h2.astype(BF16)
    h2_ref[0] = h2_hi
    h2_lo = (h2 - h2_hi.astype(F32)).astype(BF16)
    lg_ref[0] = _route_rows(
        (_dot(h2_hi, wr_ref[0]) + (_dot(h2_lo, wr_ref[0]) + _dot(h2_hi, wr_ref[1]))) + br_ref[...])


def _merge(x_all, h, yb, mods, rwkv_parts, rwkv_consts, ret_parts, u_ret, weights, tile_off):
    B, T, D = x_all.shape
    nt = T // ROW_TILE - tile_off
    tok = lambda w: pl.BlockSpec((1, ROW_TILE, w), lambda b, j: (b, j + tile_off, 0))
    out = lambda w: pl.BlockSpec((1, ROW_TILE, w), lambda b, j: (b, j, 0))
    full = lambda a: pl.BlockSpec(a.shape, lambda b, j: (0,) * a.ndim)
    ret_gate = pl.BlockSpec((1, ROW_TILE, RET_DIM), lambda b, j: (b, j + tile_off, 3))
    return pl.pallas_call(
        _merge_kernel,
        grid=(B, nt),
        in_specs=[tok(D), tok(D), tok(MLA_DIM),
                  pl.BlockSpec((1, 1, 8, D), lambda b, j: (b, jnp.minimum(j + tile_off, 1), 0, 0))]
                 + [tok(RWKV_DIM)] * 6 + [full(a) for a in rwkv_consts]
                 + [tok(RET_DIM)] * 2 + [ret_gate] + [full(a) for a in weights],
        out_specs=[out(D), out(D), out(ROUTER_LANES)],
        out_shape=[jax.ShapeDtypeStruct((B, nt * ROW_TILE, D), F32),
                   jax.ShapeDtypeStruct((B, nt * ROW_TILE, D), BF16),
                   jax.ShapeDtypeStruct((B, nt * ROW_TILE, ROUTER_LANES), F32)],
        compiler_params=pltpu.CompilerParams(
            dimension_semantics=("parallel", "parallel"), vmem_limit_bytes=VMEM_LIMIT),
        name="merge",
    )(x_all, h, yb, mods, *rwkv_parts, *rwkv_consts, *ret_parts, u_ret, *weights)


def _moe_kernel(te_ref, x_ref, cw_ref, wg_ref, wu_ref, wd_ref, o_ref):
    del te_ref
    x = x_ref[...]
    hid = jax.nn.silu(_dot(x, wg_ref[0])) * _dot(x, wu_ref[0])
    o_ref[...] = _dot(hid.astype(BF16), wd_ref[0]) * cw_ref[...]


def _moe_experts(tile_expert, x_sorted, cw_sorted, w_gate, w_up, w_down):
    n_rows, D = x_sorted.shape
    n_tiles = n_rows // MOE_TILE
    grid_spec = pltpu.PrefetchScalarGridSpec(
        num_scalar_prefetch=1,
        grid=(n_tiles,),
        in_specs=[pl.BlockSpec((MOE_TILE, D), lambda i, te: (i, 0)),
                  pl.BlockSpec((MOE_TILE, 1), lambda i, te: (i, 0)),
                  pl.BlockSpec((1, D, EXPERT_HIDDEN), lambda i, te: (te[i], 0, 0)),
                  pl.BlockSpec((1, D, EXPERT_HIDDEN), lambda i, te: (te[i], 0, 0)),
                  pl.BlockSpec((1, EXPERT_HIDDEN, D), lambda i, te: (te[i], 0, 0))],
        out_specs=pl.BlockSpec((MOE_TILE, D), lambda i, te: (i, 0)),
    )
    return pl.pallas_call(
        _moe_kernel,
        grid_spec=grid_spec,
        out_shape=jax.ShapeDtypeStruct((n_rows, D), F32),
        compiler_params=pltpu.CompilerParams(
            dimension_semantics=("arbitrary",), vmem_limit_bytes=VMEM_LIMIT),
        name="moe_experts",
    )(tile_expert, x_sorted, cw_sorted, w_gate, w_up, w_down)


def _final_norm_kernel(x_ref, w_ref, o_ref):
    x = x_ref[...]
    o_ref[...] = x * lax.rsqrt(jnp.mean(x * x, axis=-1, keepdims=True) + RMS_EPS) * w_ref[...]


def _final_norm(x, w):
    n_rows, D = x.shape
    tile = 512
    return pl.pallas_call(
        _final_norm_kernel,
        grid=(n_rows // tile,),
        in_specs=[pl.BlockSpec((tile, D), lambda i: (i, 0)), pl.BlockSpec((1, D), lambda i: (0, 0))],
        out_specs=pl.BlockSpec((tile, D), lambda i: (i, 0)),
        out_shape=jax.ShapeDtypeStruct((n_rows, D), F32),
        compiler_params=pltpu.CompilerParams(dimension_semantics=("parallel",)),
        name="final_norm",
    )(x, w)


def _rope_table(n_tokens, rot_dim):
    rows = n_tokens // GRID_W
    row = jnp.repeat(jnp.arange(rows, dtype=F32), GRID_W)
    col = jnp.tile(jnp.arange(GRID_W, dtype=F32), rows)
    n_freq = rot_dim // 4
    inv_freq = ROPE_BASE ** (-jnp.arange(n_freq, dtype=F32) / n_freq)
    ang = jnp.concatenate([row[:, None] * inv_freq, col[:, None] * inv_freq], axis=-1)
    return jnp.cos(ang), jnp.sin(ang)


def _rwkv_branch(u_rwkv, p):
    r, k, v, vTp, kap, g, *per_dir = _rwkv_prep(u_rwkv, p)
    y_f, y_b = _rwkv_scan(kap, vTp, v, per_dir)
    return y_f, y_b, r, k, v, g


def _mla_branch(u_mla, p, rope):
    L = u_mla.shape[1]
    cos, sin = rope
    n_ctx = L - cos.shape[0]
    cos = jnp.concatenate([jnp.ones((n_ctx, cos.shape[1]), F32), cos], axis=0)
    sin = jnp.concatenate([jnp.zeros((n_ctx, sin.shape[1]), F32), sin], axis=0)
    tail = MLA_SLOT - MLA_NOPE - MLA_ROPE
    cos_rows = jnp.concatenate([jnp.ones((L, MLA_NOPE), F32), cos, cos, jnp.ones((L, tail), F32)], axis=1)
    sin_rows = jnp.concatenate([jnp.zeros((L, MLA_NOPE), F32), -sin, sin, jnp.zeros((L, tail), F32)], axis=1)
    q, k, v = _mla_prep(u_mla, cos_rows, sin_rows, p)
    return _attention(q, k, v)


def _retention_branch(u_ret, p, rope):
    B, L, _ = u_ret.shape
    C = RET_CHUNK
    cos, sin = rope
    n_ctx = L - cos.shape[0]
    cos = jnp.concatenate([jnp.ones((n_ctx, cos.shape[1]), F32), cos], axis=0)
    sin = jnp.concatenate([jnp.zeros((n_ctx, sin.shape[1]), F32), sin], axis=0)
    cos_rows = jnp.tile(cos, (1, 2 * RET_HEADS))
    sin_rows = jnp.tile(jnp.concatenate([-sin, sin], axis=1), (1, RET_HEADS))
    log_gamma = jax.nn.log_sigmoid(p['ret_decay'])
    idx = jnp.arange(C, dtype=F32)
    rel = idx[:, None] - idx[None, :]
    lg_f, lg_b = log_gamma[0][:, None, None], log_gamma[1][:, None, None]
    inner = jnp.stack([jnp.where(rel[None] >= 0, jnp.exp(lg_f * jnp.maximum(rel, 0.0)[None]), 0.0),
                       jnp.where(rel[None] <= 0, jnp.exp(lg_b * jnp.maximum(-rel, 0.0)[None]), 0.0)])
    cross = jnp.stack([jnp.exp(lg_f * (idx + 1.0)[None, :, None]), jnp.exp(lg_b * (C - idx)[None, :, None])])
    cross = jnp.broadcast_to(cross, (2, RET_HEADS, C, RET_VAL))
    tailT = jnp.stack([jnp.exp(lg_f * (C - 1.0 - idx)[None, None, :]), jnp.exp(lg_b * idx[None, None, :])])
    tailT = jnp.broadcast_to(tailT, (2, RET_HEADS, RET_KEY, C))
    cdec = jnp.broadcast_to(jnp.exp(log_gamma * C)[:, :, None, None], (2, RET_HEADS, RET_KEY, RET_VAL))
    o_f, o_b = _retention_scan(u_ret, cos_rows, sin_rows, inner, cross, tailT, cdec)
    return o_f, o_b


def _moe(h2, routed, p):
    n_tok, D = h2.shape
    expert_id, weights = routed[:, :TOP_K].astype(jnp.int32), routed[:, TOP_K:2 * TOP_K]
    n_assign = n_tok * TOP_K
    flat_e = expert_id.reshape(n_assign).astype(jnp.int32)
    flat_w = weights.reshape(n_assign)
    sorted_e, order = lax.sort((flat_e, jnp.arange(n_assign, dtype=jnp.int32)), num_keys=1, is_stable=True)
    bounds = jnp.searchsorted(sorted_e, jnp.arange(N_EXPERTS + 1, dtype=jnp.int32),
                              method='scan_unrolled').astype(jnp.int32)
    raw_start, counts = bounds[:-1], bounds[1:] - bounds[:-1]
    padded = ((counts + MOE_TILE - 1) // MOE_TILE) * MOE_TILE
    pad_end = jnp.cumsum(padded)
    pad_start = pad_end - padded
    rank = jnp.argsort(order).astype(jnp.int32)
    n_tiles = n_assign // MOE_TILE + N_EXPERTS
    tile_start = jnp.arange(n_tiles, dtype=jnp.int32) * MOE_TILE
    tile_expert = jnp.minimum(
        jnp.sum((tile_start[:, None] >= pad_end[None, :]).astype(jnp.int32), axis=1), N_EXPERTS - 1)
    row_e = jnp.repeat(tile_expert, MOE_TILE)
    within = jnp.arange(n_tiles * MOE_TILE, dtype=jnp.int32) - pad_start[row_e]
    valid = within < counts[row_e]
    src_assign = order[jnp.clip(raw_start[row_e] + within, 0, n_assign - 1)]
    src_tok = jnp.where(valid, src_assign // TOP_K, 0)
    cw = jnp.where(valid, flat_w[src_assign], 0.0)
    pos = (pad_start[flat_e] + rank - raw_start[flat_e]).reshape(n_tok, TOP_K)
    out = _moe_experts(tile_expert, h2[src_tok], cw[:, None], p['moe_w_gate'], p['moe_w_up'], p['moe_w_down'])
    return out[pos[:, 0]] + out[pos[:, 1]]


def _permute_w_in(w_in):
    w_a = w_in[:, :U_RWKV]
    lo = U_RWKV + MLA_Q_RANK + MLA_KV_RANK
    w_b = jnp.concatenate([w_in[:, U_RWKV:lo], jnp.pad(w_in[:, lo:lo + MLA_ROPE], ((0, 0), (MLA_NOPE, MLA_SLOT - MLA_NOPE - MLA_ROPE)))], axis=1)
    w_c = w_in[:, U_RWKV + U_MLA_RAW:]
    return w_a.astype(BF16), w_b.astype(BF16), w_c.astype(BF16)


def kernel(x, c, ctx, c_ctx, w_mod, b_mod, norm1_w, w_in, rwkv_conv, rwkv_w0, rwkv_w_up, rwkv_a0, rwkv_a_up, rwkv_g_up, rwkv_k_k, rwkv_k_a, rwkv_r_k, rwkv_ln_w, rwkv_ln_b, mla_q_norm, mla_w_uq, mla_kv_norm, mla_w_ukv, ret_decay, w_branch_a, w_branch_b, w_branch_c, w_branch_gate, b_branch_gate, w_out, norm2_w, moe_w_group, moe_b_group, moe_w_expert, moe_b_expert, moe_w_gate, moe_w_up, moe_w_down, final_norm_w):
    B, n_lat, D = x.shape
    depth = w_mod.shape[0]
    ropes = (_rope_table(n_lat, MLA_ROPE), _rope_table(n_lat, RET_KEY))
    x_all = jnp.concatenate([ctx, x], axis=1)
    L = x_all.shape[1]
    for l in range(depth):
        with_ctx = l < depth - 1
        p = {
            'rwkv_conv': rwkv_conv[l], 'rwkv_w0': rwkv_w0[l], 'rwkv_w_up': rwkv_w_up[l],
            'rwkv_a0': rwkv_a0[l], 'rwkv_a_up': rwkv_a_up[l], 'rwkv_g_up': rwkv_g_up[l],
            'rwkv_k_k': rwkv_k_k[l], 'rwkv_k_a': rwkv_k_a[l], 'rwkv_r_k': rwkv_r_k[l].reshape(RWKV_HEADS, HEAD_DIM),
            'rwkv_ln_w': rwkv_ln_w[l], 'rwkv_ln_b': rwkv_ln_b[l],
            'mla_q_norm': mla_q_norm[l], 'mla_w_uq': mla_w_uq[l], 'mla_kv_norm': mla_kv_norm[l],
            'mla_w_ukv': mla_w_ukv[l], 'ret_decay': ret_decay[l],
            'moe_w_gate': moe_w_gate[l].astype(BF16), 'moe_w_up': moe_w_up[l].astype(BF16),
            'moe_w_down': moe_w_down[l].astype(BF16),
        }
        mod_lat = jax.nn.silu(c) @ w_mod[l] + b_mod[l]
        mod_ctx = jax.nn.silu(c_ctx) @ w_mod[l] + b_mod[l]
        mods = jnp.stack([jnp.broadcast_to(mod_ctx, mod_lat.shape), mod_lat], axis=1).reshape(B, 2, N_MOD, D)
        mods = jnp.pad(mods, ((0, 0), (0, 0), (0, 8 - N_MOD), (0, 0)))

        w_a, w_b, w_c = _permute_w_in(w_in[l])
        h, u_rwkv, u_mla, u_ret = _inproj(x_all, mods, norm1_w[l][None, :], w_a, w_b, w_c)

        rwkv_parts = _rwkv_branch(u_rwkv, p)
        yb = _mla_branch(u_mla, p, ropes[0])
        ret_parts = _retention_branch(u_ret, p, ropes[1])
        head_idx = jnp.arange(RWKV_DIM) // HEAD_DIM
        rwkv_consts = [p['rwkv_r_k'].reshape(1, RWKV_DIM), p['rwkv_ln_w'][None, :], p['rwkv_ln_b'][None, :],
                       (head_idx[:, None] == head_idx[None, :]).astype(BF16)]

        w_router = jnp.pad(jnp.concatenate([moe_w_group[l], moe_w_expert[l]], axis=1),
                           ((0, 0), (0, ROUTER_LANES - N_GROUPS - N_EXPERTS)))
        w_router_hi = w_router.astype(BF16)
        w_router = jnp.stack([w_router_hi, (w_router - w_router_hi.astype(F32)).astype(BF16)])
        b_router = jnp.pad(jnp.concatenate([moe_b_group[l], moe_b_expert[l]]),
                           (0, ROUTER_LANES - N_GROUPS - N_EXPERTS))[None, :]
        tile_off = 0 if with_ctx else CTX_LEN // ROW_TILE
        weights = [w_branch_gate[l].astype(BF16), b_branch_gate[l][None, :],
                   w_branch_a[l].astype(BF16), w_branch_b[l].astype(BF16), w_branch_c[l].astype(BF16),
                   w_out[l].astype(BF16), norm2_w[l][None, :], w_router, b_router]
        x_mid, h2, logits = _merge(x_all, h, yb, mods, rwkv_parts, rwkv_consts, ret_parts, u_ret, weights, tile_off)
        Lm = x_mid.shape[1]
        f = _moe(h2.reshape(B * Lm, D), logits.reshape(B * Lm, ROUTER_LANES), p).reshape(B, Lm, D)
        if with_ctx:
            g2 = jnp.concatenate([jnp.broadcast_to(mods[:, 0:1, 5, :], (B, CTX_LEN, D)),
                                  jnp.broadcast_to(mods[:, 1:2, 5, :], (B, L - CTX_LEN, D))], axis=1)
            x_all = x_mid + g2 * f
        else:
            x_lat = x_mid + mods[:, 1:2, 5, :] * f
    return _final_norm(x_lat.reshape(B * n_lat, D), final_norm_w[None, :]).reshape(B, n_lat, D)
```

```python
import functools

import jax
import jax.numpy as jnp
from jax import lax
from jax.experimental import pallas as pl
from jax.experimental.pallas import tpu as pltpu

D_MODEL = 1024
CTX_LEN = 256
GRID_W = 64
N_MOD = 6
RMS_EPS = 1e-6
ROPE_BASE = 10000.0
HEAD_DIM = 64
F32 = jnp.float32
BF16 = jnp.bfloat16

RWKV_HEADS = 4
RWKV_DIM = RWKV_HEADS * HEAD_DIM
DECAY_LORA = 64
AAA_LORA = 64
GATE_LORA = 128
RWKV_GN_EPS = 64e-5

MLA_HEADS = 8
MLA_NOPE = 64
MLA_ROPE = 32
MLA_QK = MLA_NOPE + MLA_ROPE
MLA_V = 64
MLA_Q_RANK = 256
MLA_KV_RANK = 128
MLA_DIM = MLA_HEADS * MLA_V

RET_HEADS = 4
RET_KEY = 64
RET_VAL = 64
RET_DIM = RET_HEADS * RET_VAL

N_GROUPS = 4
EXPERTS_PER_GROUP = 8
N_EXPERTS = N_GROUPS * EXPERTS_PER_GROUP
TOP_K = 2
EXPERT_HIDDEN = 512

U_RWKV = 3 * RWKV_DIM + 2 * DECAY_LORA + 2 * AAA_LORA + GATE_LORA
U_MLA_RAW = MLA_Q_RANK + MLA_KV_RANK + MLA_ROPE
U_MLA = 512
U_RET = 4 * RET_DIM

ROW_TILE = 256
RWKV_CHUNK = 64
RET_CHUNK = 256
ROUTER_LANES = 128
MOE_TILE = 256
VMEM_LIMIT = 56 * 1024 * 1024


def _dot(a, b):
    return jnp.dot(a, b, preferred_element_type=F32)


def _dot_nt(a, b):
    return lax.dot_general(a, b, (((1,), (1,)), ((), ())), preferred_element_type=F32)


def _inproj_kernel(x_ref, mod_ref, nw_ref, wa_ref, wb_ref, wc_ref, h_ref, ua_ref, ub_ref, uc_ref):
    x = x_ref[0]
    y = x * lax.rsqrt(jnp.mean(x * x, axis=-1, keepdims=True) + RMS_EPS) * nw_ref[...]
    h = y * (1.0 + mod_ref[0, 0, 1:2, :]) + mod_ref[0, 0, 0:1, :]
    hb = h.astype(BF16)
    h_ref[0] = hb
    ua_ref[0] = _dot(hb, wa_ref[...])
    ub_ref[0] = _dot(hb, wb_ref[...])
    uc_ref[0] = _dot(hb, wc_ref[...])


def _inproj(x_all, mods, norm_w, w_a, w_b, w_c):
    B, T, D = x_all.shape
    nt = T // ROW_TILE
    tok = lambda w: pl.BlockSpec((1, ROW_TILE, w), lambda b, j: (b, j, 0))
    full = lambda a: pl.BlockSpec(a.shape, lambda b, j: (0,) * a.ndim)
    return pl.pallas_call(
        _inproj_kernel,
        grid=(B, nt),
        in_specs=[tok(D),
                  pl.BlockSpec((1, 1, 8, D), lambda b, j: (b, jnp.minimum(j, 1), 0, 0)),
                  full(norm_w), full(w_a), full(w_b), full(w_c)],
        out_specs=[tok(D), tok(U_RWKV), tok(U_MLA), tok(U_RET)],
        out_shape=[jax.ShapeDtypeStruct((B, T, D), BF16),
                   jax.ShapeDtypeStruct((B, T, U_RWKV), F32),
                   jax.ShapeDtypeStruct((B, T, U_MLA), F32),
                   jax.ShapeDtypeStruct((B, T, U_RET), F32)],
        compiler_params=pltpu.CompilerParams(
            dimension_semantics=("parallel", "parallel"), vmem_limit_bytes=VMEM_LIMIT),
        name="inproj",
    )(x_all, mods, norm_w, w_a, w_b, w_c)


def _split3(x):
    hi = x.astype(BF16)
    rem = x - hi.astype(F32)
    mid = rem.astype(BF16)
    return hi, mid, (rem - mid.astype(F32)).astype(BF16)


def _sums_right(x, mask01):
    hi, mid, lo = _split3(x)
    return _dot(hi, mask01) + (_dot(mid, mask01) + _dot(lo, mask01))


def _softplus(z):
    return jnp.maximum(z, 0.0) + jnp.log(1.0 + jnp.exp(-jnp.abs(z)))


def _rwkv_prep_kernel(u_ref, up_ref, un_ref, conv_ref, kk_ref, ka_ref, w0_ref, a0_ref, wup_ref, aup_ref,
                      gup_ref, ones_ref, tril_ref, triu_ref,
                      r_ref, k_ref, v_ref, vTp_ref, kap_ref, g_ref,
                      w_ref, beta_ref, kaph_ref, kch_ref, bch_ref, rh_ref, kend_ref):
    R = RWKV_DIM
    j = pl.program_id(1)
    u = u_ref[0]
    x = u[:, :3 * R]
    prev_row = jnp.where(j >= 2, up_ref[0, 7:8, :3 * R], 0.0)
    next_row = jnp.where((j >= 1) & (j < pl.num_programs(1) - 1), un_ref[0, 0:1, :3 * R], 0.0)
    rid = lax.broadcasted_iota(jnp.int32, (ROW_TILE, 3 * R), 0)
    x_prev = jnp.where(rid == 0, prev_row, pltpu.roll(x, 1, 0))
    x_next = jnp.where(rid == ROW_TILE - 1, next_row, pltpu.roll(x, ROW_TILE - 1, 0))
    rkv = x_prev * conv_ref[0:1, :] + x * conv_ref[1:2, :] + x_next * conv_ref[2:3, :]
    r, k, v = rkv[:, :R], rkv[:, R:2 * R], rkv[:, 2 * R:]
    ones_blk = ones_ref[...]
    kk = k * kk_ref[...]
    kap = kk / jnp.maximum(jnp.sqrt(_sums_right(kk * kk, ones_blk)), 1e-12)
    r_ref[0] = r
    k_ref[0] = k
    v_ref[0] = v
    vT = v.T
    for c in range(ROW_TILE // RWKV_CHUNK):
        for hp in range(RWKV_HEADS // 2):
            vTp_ref[0, c, hp] = jnp.concatenate(
                [vT[HEAD_DIM * (2 * hp + s):HEAD_DIM * (2 * hp + s + 1), RWKV_CHUNK * c:RWKV_CHUNK * (c + 1)]
                 for s in range(2)], axis=1)
    kap_ref[0] = kap
    lo = 3 * R
    tw = jnp.tanh(u[:, lo:lo + 2 * DECAY_LORA]).astype(BF16)
    ad = u[:, lo + 2 * DECAY_LORA:lo + 2 * DECAY_LORA + 2 * AAA_LORA].astype(BF16)
    g_ref[0] = _dot(jax.nn.sigmoid(u[:, lo + 2 * DECAY_LORA + 2 * AAA_LORA:]).astype(BF16), gup_ref[...])
    for d in range(2):
        log_w = -_softplus(-(w0_ref[d:d + 1, :] + _dot(tw, wup_ref[d]))) - 0.5
        logw = -jnp.exp(log_w)
        a = jax.nn.sigmoid(a0_ref[d:d + 1, :] + _dot(ad, aup_ref[d]))
        kmod = k * (1.0 + (a - 1.0) * ka_ref[...])
        beta = kap * a
        hi, mid, lo = _split3(logw)
        tri = tril_ref[...] if d == 0 else triu_ref[...]
        cs = _dot(tri, hi) + (_dot(tri, mid) + _dot(tri, lo))
        tot = _dot(ones_blk, hi) + (_dot(ones_blk, mid) + _dot(ones_blk, lo))
        inv = jnp.exp(-cs)
        w_ref[d, 0] = jnp.exp(logw)
        beta_ref[d, 0] = beta
        kaph_ref[d, 0] = kap * jnp.exp(cs - logw)
        kch_ref[d, 0] = kmod * inv
        bch_ref[d, 0] = beta * inv
        rh_ref[d, 0] = r * jnp.exp(cs)
        kend_ref[d, 0] = kmod * jnp.exp(tot - cs)


def _rwkv_prep(u_rwkv, p):
    B, L, _ = u_rwkv.shape
    R = RWKV_DIM
    nt = L // ROW_TILE
    halo = ROW_TILE // 8
    n_halo = L // 8
    idx = jnp.arange(ROW_TILE)
    same_chunk = (idx[:, None] // RWKV_CHUNK) == (idx[None, :] // RWKV_CHUNK)
    ones_blk = same_chunk.astype(BF16)
    tril = (same_chunk & (idx[None, :] <= idx[:, None])).astype(BF16)
    triu = (same_chunk & (idx[None, :] >= idx[:, None])).astype(BF16)
    pad_rows = lambda w, d, n: jnp.pad(w, ((d * n, n - d * n), (0, 0)))
    wup = jnp.stack([pad_rows(p['rwkv_w_up'][d], d, DECAY_LORA) for d in range(2)]).astype(BF16)
    aup = jnp.stack([pad_rows(p['rwkv_a_up'][d], d, AAA_LORA) for d in range(2)]).astype(BF16)
    consts = [p['rwkv_conv'], p['rwkv_k_k'][None, :], p['rwkv_k_a'][None, :], p['rwkv_w0'], p['rwkv_a0'],
              wup, aup, p['rwkv_g_up'].astype(BF16), ones_blk, tril, triu]
    full = lambda a: pl.BlockSpec(a.shape, lambda b, j: (0,) * a.ndim)
    tok = pl.BlockSpec((1, ROW_TILE, R), lambda b, j: (b, j, 0))
    per_dir = pl.BlockSpec((2, 1, ROW_TILE, R), lambda b, j: (0, b, j, 0))
    tok_shape = jax.ShapeDtypeStruct((B, L, R), F32)
    dir_shape = jax.ShapeDtypeStruct((2, B, L, R), F32)
    vtp_block = (ROW_TILE // RWKV_CHUNK, RWKV_HEADS // 2, HEAD_DIM, 2 * RWKV_CHUNK)
    return pl.pallas_call(
        _rwkv_prep_kernel,
        grid=(B, nt),
        in_specs=[pl.BlockSpec((1, ROW_TILE, U_RWKV), lambda b, j: (b, j, 0)),
                  pl.BlockSpec((1, 8, U_RWKV), lambda b, j: (b, jnp.maximum(j * halo - 1, 0), 0)),
                  pl.BlockSpec((1, 8, U_RWKV), lambda b, j: (b, jnp.minimum((j + 1) * halo, n_halo - 1), 0))]
                 + [full(a) for a in consts],
        out_specs=[tok, tok, tok, pl.BlockSpec((1,) + vtp_block, lambda b, j: (b, j, 0, 0, 0)), tok, tok]
                  + [per_dir] * 7,
        out_shape=[tok_shape, tok_shape, tok_shape,
                   jax.ShapeDtypeStruct((B, L // RWKV_CHUNK) + vtp_block[1:], F32), tok_shape, tok_shape]
                  + [dir_shape] * 7,
        compiler_params=pltpu.CompilerParams(
            dimension_semantics=("parallel", "parallel"), vmem_limit_bytes=VMEM_LIMIT),
        name="rwkv_prep",
    )(u_rwkv, u_rwkv, u_rwkv, *consts)


RWKV_BATCH = 2
RWKV_ROWS = 2 * RWKV_CHUNK
SUBLANES = 8
RWKV_UNROLL = 8


def _rwkv_scan_kernel(*refs):
    T, N = RWKV_CHUNK, HEAD_DIM
    W = 2 * N
    ins, outs = refs[:20], refs[20:22]
    p_ref, mix_ref, seq_ref, y0_ref, cm_ref = refs[22:]
    n_pair = RWKV_HEADS // 2
    pairs = [(bi, d, hp) for bi in range(RWKV_BATCH) for d in range(2) for hp in range(n_pair)]

    @pl.when(pl.program_id(1) == 0)
    def _():
        p_ref[...] = jnp.zeros_like(p_ref)

    row_t = lax.broadcasted_iota(jnp.int32, (T, W), 0)
    col = lax.broadcasted_iota(jnp.int32, (T, W), 1)
    first = col < N
    col_j = jnp.where(first, col, col - N)
    before = (col_j < row_t, col_j > row_t)
    upto = (col_j <= row_t, col_j >= row_t)
    lane = lax.broadcasted_iota(jnp.int32, (SUBLANES, W), 1)
    low = lane < N
    lane_t = jnp.where(low, lane, lane - N)

    def blockdiag(x):
        return jnp.concatenate([jnp.where(first, x, 0.0), jnp.where(first, 0.0, x)], axis=0)

    for slot in range(2):
        for pc, (bi, d, hp) in enumerate(pairs):
            kap_r, vTp_r, v_r, w_r, beta_r, kaph_r, kch_r, bch_r, rh_r, kend_r = ins[10 * d:10 * d + 10]
            q = slot if d == 0 else 1 - slot
            rows, cols = slice(T * q, T * q + T), slice(W * hp, W * hp + W)
            kch2 = blockdiag(kch_r[0, bi, rows, cols])
            rh = rh_r[0, bi, rows, cols]
            b_m = jnp.where(before[d], _dot_nt(kaph_r[0, bi, rows, cols], kch2), 0.0)
            e_m = jnp.where(upto[d], _dot_nt(rh, kch2), 0.0)
            cm_ref[pc] = jnp.where(upto[d], _dot_nt(rh, blockdiag(bch_r[0, bi, rows, cols])), 0.0)
            mix_ref[pc] = _dot_nt(vTp_r[bi, q, hp], blockdiag(b_m))
            y0_ref[pc] = _dot_nt(rh, blockdiag(p_ref[pc])) + _dot(e_m, blockdiag(v_r[bi, rows, cols]))
            seq_ref[0, pc] = kap_r[bi, rows, cols]
            seq_ref[1, pc] = w_r[0, bi, rows, cols]
            seq_ref[2, pc] = beta_r[0, bi, rows, cols]

        def step(i, carry):
            for pc, (bi, d, hp) in enumerate(pairs):
                t = i if d == 0 else T - 1 - i
                at_t = lane_t == t
                kap_t, w_t, beta_t = (jnp.broadcast_to(seq_ref[op, pc, pl.ds(t, 1), :], (SUBLANES, W))
                                      for op in range(3))
                for s8 in range(N // SUBLANES):
                    sub = slice(SUBLANES * s8, SUBLANES * s8 + SUBLANES)
                    p, mix = p_ref[pc, sub, :], mix_ref[pc, sub, :]
                    prod = p * kap_t + jnp.where(at_t, mix, 0.0)
                    sa = jnp.where(low, jnp.sum(jnp.where(low, prod, 0.0), axis=-1, keepdims=True),
                                   jnp.sum(jnp.where(low, 0.0, prod), axis=-1, keepdims=True))
                    p_ref[pc, sub, :] = p * w_t - sa * beta_t
                    mix_ref[pc, sub, :] = jnp.where(at_t, sa, mix)
            return carry

        lax.fori_loop(0, T, step, 0, unroll=RWKV_UNROLL)

        for bi in range(RWKV_BATCH):
            for d in range(2):
                vTp_r, kend_r = ins[10 * d + 1], ins[10 * d + 9]
                q = slot if d == 0 else 1 - slot
                rows = slice(T * q, T * q + T)
                ys = []
                for hp in range(n_pair):
                    pc = (bi * 2 + d) * n_pair + hp
                    cols = slice(W * hp, W * hp + W)
                    ys.append(y0_ref[pc] - _dot_nt(cm_ref[pc], blockdiag(mix_ref[pc])))
                    p_ref[pc] = p_ref[pc] + _dot(vTp_r[bi, q, hp], blockdiag(kend_r[0, bi, rows, cols]))
                outs[d][bi, rows, :] = jnp.concatenate(ys, axis=-1)


def _rwkv_scan(kap, vTp, v, per_dir):
    B, L, R = kap.shape
    nb = L // RWKV_ROWS
    n_ctx = CTX_LEN // RWKV_ROWS
    chunks = RWKV_ROWS // RWKV_CHUNK
    fwd = lambda c: c
    bwd = lambda c: jnp.where(c < n_ctx, n_ctx - 1 - c, nb + n_ctx - 1 - c)
    in_specs, args = [], []
    for d, blk in enumerate((fwd, bwd)):
        tok = pl.BlockSpec((RWKV_BATCH, RWKV_ROWS, R), lambda b, c, blk=blk: (b, blk(c), 0))
        tr = pl.BlockSpec((RWKV_BATCH, chunks) + vTp.shape[2:], lambda b, c, blk=blk: (b, blk(c), 0, 0, 0))
        dirs = pl.BlockSpec((1, RWKV_BATCH, RWKV_ROWS, R), lambda b, c, blk=blk, d=d: (d, b, blk(c), 0))
        in_specs += [tok, tr, tok] + [dirs] * 7
        args += [kap, vTp, v] + list(per_dir)
    out_specs = [pl.BlockSpec((RWKV_BATCH, RWKV_ROWS, R), lambda b, c, blk=blk: (b, blk(c), 0))
                 for blk in (fwd, bwd)]
    pair_tile = (RWKV_BATCH * RWKV_HEADS, HEAD_DIM, 2 * HEAD_DIM)
    return pl.pallas_call(
        _rwkv_scan_kernel,
        grid=(B // RWKV_BATCH, nb),
        in_specs=in_specs,
        out_specs=out_specs,
        out_shape=[jax.ShapeDtypeStruct((B, L, R), F32)] * 2,
        scratch_shapes=[pltpu.VMEM(pair_tile, F32), pltpu.VMEM(pair_tile, F32), pltpu.VMEM((3,) + pair_tile, F32),
                        pltpu.VMEM(pair_tile, F32), pltpu.VMEM(pair_tile, F32)],
        compiler_params=pltpu.CompilerParams(
            dimension_semantics=("parallel", "arbitrary"), vmem_limit_bytes=VMEM_LIMIT),
        name="rwkv_scan",
    )(*args)


def _rwkv_out_rows(y, r, k, v, g, rk, lnw, lnb, ones_blk):
    mu = _sums_right(y, ones_blk) * (1.0 / HEAD_DIM)
    yc = y - mu
    var = _sums_right(yc * yc, ones_blk) * (1.0 / HEAD_DIM)
    yn = yc * lax.rsqrt(var + RWKV_GN_EPS) * lnw + lnb
    bonus = _sums_right(r * k * rk, ones_blk) * v
    return ((yn + bonus) * g).astype(BF16)


MLA_SLOT = 128
MLA_WIDE = MLA_HEADS * MLA_SLOT


def _rope_mla(x, cos, sin_signed):
    half = MLA_ROPE // 2
    lane = lax.broadcasted_iota(jnp.int32, x.shape, 1) % MLA_SLOT
    partner = jnp.where(lane < MLA_NOPE + half, pltpu.roll(x, x.shape[1] - half, 1), pltpu.roll(x, half, 1))
    return x * cos + partner * sin_signed


def _mla_prep_kernel(u_ref, cos_ref, sin_ref, qn_ref, kvn_ref, wq_ref, wk_ref, wv_ref, q_ref, k_ref, v_ref):
    u = u_ref[0]
    q_dn = u[:, :MLA_Q_RANK]
    kv_dn = u[:, MLA_Q_RANK:MLA_Q_RANK + MLA_KV_RANK]
    k_rot = u[:, MLA_Q_RANK + MLA_KV_RANK:]
    norm = lambda t, w: (t * lax.rsqrt(jnp.mean(t * t, axis=-1, keepdims=True) + RMS_EPS) * w).astype(BF16)
    qn, kvn = norm(q_dn, qn_ref[...]), norm(kv_dn, kvn_ref[...])
    cos, sin = cos_ref[...], sin_ref[...]
    wide = lambda t: jnp.concatenate([t] * MLA_HEADS, axis=1)
    q_ref[0] = (_rope_mla(_dot(qn, wq_ref[...]), wide(cos), wide(sin)) * (MLA_QK ** -0.5)).astype(BF16)
    k_ref[0] = (_dot(kvn, wk_ref[...]) + wide(_rope_mla(k_rot, cos, sin))).astype(BF16)
    v = _dot(kvn, wv_ref[...])
    lane = lax.broadcasted_iota(jnp.int32, v.shape, 1) % MLA_SLOT
    v_ref[0] = jnp.where(lane == MLA_V, 1.0, v).astype(BF16)


def _mla_prep(u_mla, cos_rows, sin_rows, p):
    B, L, _ = u_mla.shape
    slots = lambda w, width: jnp.pad(w.reshape(w.shape[0], MLA_HEADS, width),
                                     ((0, 0), (0, 0), (0, MLA_SLOT - width))).reshape(w.shape[0], MLA_WIDE)
    w_ukv = p['mla_w_ukv'].reshape(MLA_KV_RANK, MLA_HEADS, MLA_NOPE + MLA_V)
    consts = [p['mla_q_norm'][None, :], p['mla_kv_norm'][None, :],
              slots(p['mla_w_uq'], MLA_QK).astype(BF16),
              slots(w_ukv[:, :, :MLA_NOPE].reshape(MLA_KV_RANK, -1), MLA_NOPE).astype(BF16),
              slots(w_ukv[:, :, MLA_NOPE:].reshape(MLA_KV_RANK, -1), MLA_V).astype(BF16)]
    tok = lambda w: pl.BlockSpec((1, ROW_TILE, w), lambda b, j: (b, j, 0))
    table = pl.BlockSpec((ROW_TILE, MLA_SLOT), lambda b, j: (j, 0))
    return pl.pallas_call(
        _mla_prep_kernel,
        grid=(B, L // ROW_TILE),
        in_specs=[tok(U_MLA), table, table] + [pl.BlockSpec(a.shape, lambda b, j: (0, 0)) for a in consts],
        out_specs=[tok(MLA_WIDE)] * 3,
        out_shape=[jax.ShapeDtypeStruct((B, L, MLA_WIDE), BF16)] * 3,
        compiler_params=pltpu.CompilerParams(
            dimension_semantics=("parallel", "parallel"), vmem_limit_bytes=VMEM_LIMIT),
        name="mla_prep",
    )(u_mla, cos_rows, sin_rows, *consts)


ATTN_Q_TILES = 2


def _attn_kernel(*refs):
    q_refs, (k_ref, v_ref, o_ref) = refs[:-3], refs[-3:]
    outs = []
    for h in range(MLA_HEADS):
        cols = slice(MLA_SLOT * h, MLA_SLOT * h + MLA_SLOT)
        q = jnp.concatenate([r[0, :, cols] for r in q_refs], axis=0)
        s = _dot_nt(q, k_ref[0, :, cols])
        p = jnp.exp((s - jnp.max(s, axis=-1, keepdims=True)).astype(BF16))
        o = _dot(p, v_ref[0, :, cols])
        outs.append(o[:, :MLA_V] / o[:, MLA_V:MLA_V + 1])
    o_ref[0] = jnp.concatenate(outs, axis=-1).astype(o_ref.dtype)


def _attention(q, k, v):
    B, L, W = q.shape
    params = pltpu.CompilerParams(dimension_semantics=("parallel", "parallel"), vmem_limit_bytes=VMEM_LIMIT)
    ctx_blk = lambda w: pl.BlockSpec((1, CTX_LEN, w), lambda b, i: (b, 0, 0))
    y_ctx = pl.pallas_call(
        _attn_kernel, grid=(B, 1), in_specs=[ctx_blk(W)] * 3, out_specs=ctx_blk(MLA_DIM),
        out_shape=jax.ShapeDtypeStruct((B, CTX_LEN, MLA_DIM), BF16), compiler_params=params,
        name="mla_attention_ctx",
    )(q, k, v)
    n_ctx = CTX_LEN // ROW_TILE
    rows = ATTN_Q_TILES * ROW_TILE
    resident = lambda: pl.BlockSpec((1, L, W), lambda b, i: (b, 0, 0), pipeline_mode=pl.Buffered(1))
    q_specs = [pl.BlockSpec((1, ROW_TILE, W), lambda b, i, t=t: (b, ATTN_Q_TILES * i + n_ctx + t, 0))
               for t in range(ATTN_Q_TILES)]
    y_lat = pl.pallas_call(
        _attn_kernel, grid=(B, (L - CTX_LEN) // rows), in_specs=q_specs + [resident(), resident()],
        out_specs=pl.BlockSpec((1, rows, MLA_DIM), lambda b, i: (b, i, 0)),
        out_shape=jax.ShapeDtypeStruct((B, L - CTX_LEN, MLA_DIM), BF16), compiler_params=params,
        name="mla_attention",
    )(*([q] * ATTN_Q_TILES), k, v)
    return jnp.concatenate([y_ctx, y_lat], axis=1)


def _rope_rows(x, cos, sin_signed):
    lane = lax.broadcasted_iota(jnp.int32, x.shape, 1)
    partner = jnp.where(lane % RET_KEY < RET_KEY // 2,
                        pltpu.roll(x, x.shape[1] - RET_KEY // 2, 1), pltpu.roll(x, RET_KEY // 2, 1))
    return x * cos + partner * sin_signed


def _ret_kernel(*refs):
    ins, (inner_ref, cross_ref, tailT_ref, cdec_ref), outs, r_ref = refs[:10], refs[10:14], refs[14:16], refs[16]

    @pl.when(pl.program_id(1) == 0)
    def _():
        r_ref[...] = jnp.zeros_like(r_ref)

    for d in range(2):
        q_ref, k_ref, v_ref, cos_ref, sin_ref = ins[5 * d:5 * d + 5]
        cos, sin = cos_ref[...], sin_ref[...]
        q_all = _rope_rows(q_ref[0], cos, sin)
        k_all = _rope_rows(k_ref[0], cos, sin) * (RET_KEY ** -0.5)
        kT_all = k_all.T
        v_all = v_ref[0]
        os = []
        for h in range(RET_HEADS):
            cols = slice(RET_KEY * h, RET_KEY * h + RET_KEY)
            q = q_all[:, cols].astype(BF16)
            v = v_all[:, cols].astype(BF16)
            state = r_ref[d, h]
            s = _dot_nt(q, k_all[:, cols].astype(BF16)) * inner_ref[d, h]
            os.append(_dot(s.astype(BF16), v) + _dot(q, state.astype(BF16)) * cross_ref[d, h])
            r_ref[d, h] = state * cdec_ref[d, h] + _dot((kT_all[cols, :] * tailT_ref[d, h]).astype(BF16), v)
        outs[d][0] = jnp.concatenate(os, axis=-1)


def _retention_scan(u_ret, cos, sin_signed, inner, cross, tailT, cdec):
    B, L, _ = u_ret.shape
    C = RET_CHUNK
    nb = L // C
    n_ctx = CTX_LEN // C
    fwd = lambda c: c
    bwd = lambda c: jnp.where(c < n_ctx, n_ctx - 1 - c, nb + n_ctx - 1 - c)
    in_specs, args = [], []
    for blk in (fwd, bwd):
        in_specs += [pl.BlockSpec((1, C, RET_DIM), lambda b, c, blk=blk, i=i: (b, blk(c), i)) for i in range(3)]
        in_specs += [pl.BlockSpec((C, RET_DIM), lambda b, c, blk=blk: (blk(c), 0))] * 2
        args += [u_ret, u_ret, u_ret, cos, sin_signed]
    consts = [inner, cross, tailT, cdec]
    in_specs += [pl.BlockSpec(a.shape, lambda b, c: (0, 0, 0, 0)) for a in consts]
    return pl.pallas_call(
        _ret_kernel,
        grid=(B, nb),
        in_specs=in_specs,
        out_specs=[pl.BlockSpec((1, C, RET_DIM), lambda b, c, blk=blk: (b, blk(c), 0)) for blk in (fwd, bwd)],
        out_shape=[jax.ShapeDtypeStruct((B, L, RET_DIM), F32)] * 2,
        scratch_shapes=[pltpu.VMEM((2, RET_HEADS, RET_KEY, RET_VAL), F32)],
        compiler_params=pltpu.CompilerParams(
            dimension_semantics=("parallel", "arbitrary"), vmem_limit_bytes=VMEM_LIMIT),
        name="retention_scan",
    )(*args, *consts)


def _ret_out_rows(y, g, ones_blk):
    ms = _sums_right(y * y, ones_blk) * (1.0 / RET_VAL)
    return (jax.nn.silu(g) * (y * lax.rsqrt(ms + RMS_EPS))).astype(BF16)


def _route_rows(lg):
    lane = lax.broadcasted_iota(jnp.int32, lg.shape, 1).astype(F32)
    low = -3.0e38
    first_at = lambda t, m: jnp.min(jnp.where(t == m, lane, float(ROUTER_LANES)), axis=-1, keepdims=True)
    is_grp = lane < N_GROUPS
    gl = jnp.where(is_grp, lg, low)
    gmax = jnp.max(gl, axis=-1, keepdims=True)
    grp = first_at(gl, gmax)
    grp_p = 1.0 / jnp.sum(jnp.where(is_grp, jnp.exp(gl - gmax), 0.0), axis=-1, keepdims=True)
    lo = N_GROUPS + EXPERTS_PER_GROUP * grp
    in_grp = (lane >= lo) & (lane < lo + EXPERTS_PER_GROUP)
    el = jnp.where(in_grp, lg, low)
    emax = jnp.max(el, axis=-1, keepdims=True)
    e1 = first_at(el, emax)
    esum = jnp.sum(jnp.where(in_grp, jnp.exp(el - emax), 0.0), axis=-1, keepdims=True)
    rest = jnp.where(lane == e1, low, el)
    emax2 = jnp.max(rest, axis=-1, keepdims=True)
    e2 = first_at(rest, emax2)
    p1 = 1.0 / esum
    p2 = jnp.exp(emax2 - emax) / esum
    w1 = grp_p * p1 / (p1 + p2)
    w2 = grp_p * p2 / (p1 + p2)
    return jnp.where(lane == 0, e1 - N_GROUPS,
                     jnp.where(lane == 1, e2 - N_GROUPS, jnp.where(lane == 2, w1, jnp.where(lane == 3, w2, 0.0))))


def _merge_kernel(x_ref, h_ref, yb_ref, mod_ref,
                  rf_ref, rb_ref, rr_ref, rk_ref, rv_ref, rg_ref, rwk_ref, lnw_ref, lnb_ref, ones_ref,
                  of_ref, ob_ref, og_ref,
                  wg_ref, bg_ref, wa_ref, wb_ref, wc_ref, wo_ref, n2_ref, wr_ref, br_ref,
                  xo_ref, h2_ref, lg_ref):
    D = D_MODEL
    ones_blk = ones_ref[...]
    ya = _rwkv_out_rows(rf_ref[0] + rb_ref[0], rr_ref[0], rk_ref[0], rv_ref[0], rg_ref[0],
                        rwk_ref[...], lnw_ref[...], lnb_ref[...], ones_blk)
    yc = _ret_out_rows(of_ref[0] + ob_ref[0], og_ref[0], ones_blk)
    gates = jax.nn.sigmoid(_dot(h_ref[0], wg_ref[...]) + bg_ref[...])
    m = (gates[:, :D] * _dot(ya, wa_ref[...])
         + gates[:, D:2 * D] * _dot(yb_ref[0], wb_ref[...])
         + gates[:, 2 * D:] * _dot(yc, wc_ref[...]))
    xn = x_ref[0] + mod_ref[0, 0, 2:3, :] * _dot(m.astype(BF16), wo_ref[...])
    xo_ref[0] = xn
    y = xn * lax.rsqrt(jnp.mean(xn * xn, axis=-1, keepdims=True) + RMS_EPS) * n2_ref[...]
    h2 = y * (1.0 + mod_ref[0, 0, 4:5, :]) + mod_ref[0, 0, 3:4, :]
    h2_hi = h2.astype(BF16)
    h2_ref[0] = h2_hi
    h2_lo = (h2 - h2_hi.astype(F32)).astype(BF16)
    lg_ref[0] = _route_rows(
        (_dot(h2_hi, wr_ref[0]) + (_dot(h2_lo, wr_ref[0]) + _dot(h2_hi, wr_ref[1]))) + br_ref[...])


def _merge(x_all, h, yb, mods, rwkv_parts, rwkv_consts, ret_parts, u_ret, weights, tile_off):
    B, T, D = x_all.shape
    nt = T // ROW_TILE - tile_off
    tok = lambda w: pl.BlockSpec((1, ROW_TILE, w), lambda b, j: (b, j + tile_off, 0))
    out = lambda w: pl.BlockSpec((1, ROW_TILE, w), lambda b, j: (b, j, 0))
    full = lambda a: pl.BlockSpec(a.shape, lambda b, j: (0,) * a.ndim)
    ret_gate = pl.BlockSpec((1, ROW_TILE, RET_DIM), lambda b, j: (b, j + tile_off, 3))
    return pl.pallas_call(
        _merge_kernel,
        grid=(B, nt),
        in_specs=[tok(D), tok(D), tok(MLA_DIM),
                  pl.BlockSpec((1, 1, 8, D), lambda b, j: (b, jnp.minimum(j + tile_off, 1), 0, 0))]
                 + [tok(RWKV_DIM)] * 6 + [full(a) for a in rwkv_consts]
                 + [tok(RET_DIM)] * 2 + [ret_gate] + [full(a) for a in weights],
        out_specs=[out(D), out(D), out(ROUTER_LANES)],
        out_shape=[jax.ShapeDtypeStruct((B, nt * ROW_TILE, D), F32),
                   jax.ShapeDtypeStruct((B, nt * ROW_TILE, D), BF16),
                   jax.ShapeDtypeStruct((B, nt * ROW_TILE, ROUTER_LANES), F32)],
        compiler_params=pltpu.CompilerParams(
            dimension_semantics=("parallel", "parallel"), vmem_limit_bytes=VMEM_LIMIT),
        name="merge",
    )(x_all, h, yb, mods, *rwkv_parts, *rwkv_consts, *ret_parts, u_ret, *weights)


def _moe_kernel(te_ref, x_ref, wg_ref, wu_ref, wd_ref, o_ref):
    del te_ref
    x = x_ref[...]
    hid = jax.nn.silu(_dot(x, wg_ref[0, 0])) * _dot(x, wu_ref[0, 0])
    o_ref[...] = _dot(hid.astype(BF16), wd_ref[0, 0])


def _moe_experts(tile_expert, x_sorted, w_gate, w_up, w_down, layer):
    n_rows, D = x_sorted.shape
    n_tiles = n_rows // MOE_TILE
    grid_spec = pltpu.PrefetchScalarGridSpec(
        num_scalar_prefetch=1,
        grid=(n_tiles,),
        in_specs=[pl.BlockSpec((MOE_TILE, D), lambda i, te: (i, 0)),
                  pl.BlockSpec((1, 1, D, EXPERT_HIDDEN), lambda i, te: (layer, te[i], 0, 0)),
                  pl.BlockSpec((1, 1, D, EXPERT_HIDDEN), lambda i, te: (layer, te[i], 0, 0)),
                  pl.BlockSpec((1, 1, EXPERT_HIDDEN, D), lambda i, te: (layer, te[i], 0, 0))],
        out_specs=pl.BlockSpec((MOE_TILE, D), lambda i, te: (i, 0)),
    )
    return pl.pallas_call(
        _moe_kernel,
        grid_spec=grid_spec,
        out_shape=jax.ShapeDtypeStruct((n_rows, D), F32),
        compiler_params=pltpu.CompilerParams(
            dimension_semantics=("arbitrary",), vmem_limit_bytes=VMEM_LIMIT),
        name="moe_experts",
    )(tile_expert, x_sorted, w_gate, w_up, w_down)


def _combine_kernel(x_ref, a_ref, b_ref, rt_ref, mod_ref, nw_ref, o_ref, *, final):
    rt = rt_ref[0]
    f = rt[:, TOP_K:TOP_K + 1] * a_ref[0] + rt[:, TOP_K + 1:TOP_K + 2] * b_ref[0]
    x = x_ref[0] + mod_ref[0, 0, 5:6, :] * f
    if final:
        x = x * lax.rsqrt(jnp.mean(x * x, axis=-1, keepdims=True) + RMS_EPS) * nw_ref[...]
    o_ref[0] = x


def _combine(x_mid, out_a, out_b, routed, mods, norm_w, tile_off, final):
    B, Lm, D = x_mid.shape
    tok = lambda w: pl.BlockSpec((1, ROW_TILE, w), lambda b, j: (b, j, 0))
    return pl.pallas_call(
        functools.partial(_combine_kernel, final=final),
        grid=(B, Lm // ROW_TILE),
        in_specs=[tok(D), tok(D), tok(D), tok(ROUTER_LANES),
                  pl.BlockSpec((1, 1, 8, D), lambda b, j: (b, jnp.minimum(j + tile_off, 1), 0, 0)),
                  pl.BlockSpec((1, D), lambda b, j: (0, 0))],
        out_specs=tok(D),
        out_shape=jax.ShapeDtypeStruct((B, Lm, D), F32),
        compiler_params=pltpu.CompilerParams(dimension_semantics=("parallel", "parallel")),
        name="combine",
    )(x_mid, out_a, out_b, routed, mods, norm_w)


def _rope_table(n_tokens, rot_dim):
    rows = n_tokens // GRID_W
    row = jnp.repeat(jnp.arange(rows, dtype=F32), GRID_W)
    col = jnp.tile(jnp.arange(GRID_W, dtype=F32), rows)
    n_freq = rot_dim // 4
    inv_freq = ROPE_BASE ** (-jnp.arange(n_freq, dtype=F32) / n_freq)
    ang = jnp.concatenate([row[:, None] * inv_freq, col[:, None] * inv_freq], axis=-1)
    return jnp.cos(ang), jnp.sin(ang)


def _rwkv_branch(u_rwkv, p):
    r, k, v, vTp, kap, g, *per_dir = _rwkv_prep(u_rwkv, p)
    y_f, y_b = _rwkv_scan(kap, vTp, v, per_dir)
    return y_f, y_b, r, k, v, g


def _mla_branch(u_mla, p, rope):
    L = u_mla.shape[1]
    cos, sin = rope
    n_ctx = L - cos.shape[0]
    cos = jnp.concatenate([jnp.ones((n_ctx, cos.shape[1]), F32), cos], axis=0)
    sin = jnp.concatenate([jnp.zeros((n_ctx, sin.shape[1]), F32), sin], axis=0)
    tail = MLA_SLOT - MLA_NOPE - MLA_ROPE
    cos_rows = jnp.concatenate([jnp.ones((L, MLA_NOPE), F32), cos, cos, jnp.ones((L, tail), F32)], axis=1)
    sin_rows = jnp.concatenate([jnp.zeros((L, MLA_NOPE), F32), -sin, sin, jnp.zeros((L, tail), F32)], axis=1)
    q, k, v = _mla_prep(u_mla, cos_rows, sin_rows, p)
    return _attention(q, k, v)


def _retention_branch(u_ret, p, rope):
    B, L, _ = u_ret.shape
    C = RET_CHUNK
    cos, sin = rope
    n_ctx = L - cos.shape[0]
    cos = jnp.concatenate([jnp.ones((n_ctx, cos.shape[1]), F32), cos], axis=0)
    sin = jnp.concatenate([jnp.zeros((n_ctx, sin.shape[1]), F32), sin], axis=0)
    cos_rows = jnp.tile(cos, (1, 2 * RET_HEADS))
    sin_rows = jnp.tile(jnp.concatenate([-sin, sin], axis=1), (1, RET_HEADS))
    log_gamma = jax.nn.log_sigmoid(p['ret_decay'])
    idx = jnp.arange(C, dtype=F32)
    rel = idx[:, None] - idx[None, :]
    lg_f, lg_b = log_gamma[0][:, None, None], log_gamma[1][:, None, None]
    inner = jnp.stack([jnp.where(rel[None] >= 0, jnp.exp(lg_f * jnp.maximum(rel, 0.0)[None]), 0.0),
                       jnp.where(rel[None] <= 0, jnp.exp(lg_b * jnp.maximum(-rel, 0.0)[None]), 0.0)])
    cross = jnp.stack([jnp.exp(lg_f * (idx + 1.0)[None, :, None]), jnp.exp(lg_b * (C - idx)[None, :, None])])
    cross = jnp.broadcast_to(cross, (2, RET_HEADS, C, RET_VAL))
    tailT = jnp.stack([jnp.exp(lg_f * (C - 1.0 - idx)[None, None, :]), jnp.exp(lg_b * idx[None, None, :])])
    tailT = jnp.broadcast_to(tailT, (2, RET_HEADS, RET_KEY, C))
    cdec = jnp.broadcast_to(jnp.exp(log_gamma * C)[:, :, None, None], (2, RET_HEADS, RET_KEY, RET_VAL))
    o_f, o_b = _retention_scan(u_ret, cos_rows, sin_rows, inner, cross, tailT, cdec)
    return o_f, o_b


def _moe(h2, routed, w_gate, w_up, w_down, layer):
    n_tok, D = h2.shape
    n_assign = n_tok * TOP_K
    flat_e = routed[:, :TOP_K].astype(jnp.int32).reshape(n_assign)
    sorted_e, order = lax.sort((flat_e, jnp.arange(n_assign, dtype=jnp.int32)), num_keys=1, is_stable=True)
    bounds = jnp.searchsorted(sorted_e, jnp.arange(N_EXPERTS + 1, dtype=jnp.int32),
                              method='scan_unrolled').astype(jnp.int32)
    raw_start, counts = bounds[:-1], bounds[1:] - bounds[:-1]
    padded = ((counts + MOE_TILE - 1) // MOE_TILE) * MOE_TILE
    pad_end = jnp.cumsum(padded)
    pad_start = pad_end - padded
    rank = jnp.argsort(order).astype(jnp.int32)
    n_tiles = n_assign // MOE_TILE + N_EXPERTS
    tile_start = jnp.arange(n_tiles, dtype=jnp.int32) * MOE_TILE
    tile_expert = jnp.minimum(
        jnp.sum((tile_start[:, None] >= pad_end[None, :]).astype(jnp.int32), axis=1), N_EXPERTS - 1)
    first = (tile_start - pad_start[tile_expert] + raw_start[tile_expert])[:, None]
    last = (raw_start[tile_expert] + counts[tile_expert] - 1)[:, None]
    src_sorted = jnp.clip(jnp.minimum(first + jnp.arange(MOE_TILE, dtype=jnp.int32)[None, :], last), 0, n_assign - 1)
    src_tok = order[src_sorted.reshape(-1)] // TOP_K
    pos = (rank + (pad_start - raw_start)[flat_e]).reshape(n_tok, TOP_K)
    out = _moe_experts(tile_expert, h2[src_tok], w_gate, w_up, w_down, layer)
    return out[pos[:, 0]], out[pos[:, 1]]


def _permute_w_in(w_in):
    w_a = w_in[:, :U_RWKV]
    lo = U_RWKV + MLA_Q_RANK + MLA_KV_RANK
    w_b = jnp.concatenate([w_in[:, U_RWKV:lo], jnp.pad(w_in[:, lo:lo + MLA_ROPE], ((0, 0), (MLA_NOPE, MLA_SLOT - MLA_NOPE - MLA_ROPE)))], axis=1)
    w_c = w_in[:, U_RWKV + U_MLA_RAW:]
    return w_a.astype(BF16), w_b.astype(BF16), w_c.astype(BF16)


def kernel(x, c, ctx, c_ctx, w_mod, b_mod, norm1_w, w_in, rwkv_conv, rwkv_w0, rwkv_w_up, rwkv_a0, rwkv_a_up, rwkv_g_up, rwkv_k_k, rwkv_k_a, rwkv_r_k, rwkv_ln_w, rwkv_ln_b, mla_q_norm, mla_w_uq, mla_kv_norm, mla_w_ukv, ret_decay, w_branch_a, w_branch_b, w_branch_c, w_branch_gate, b_branch_gate, w_out, norm2_w, moe_w_group, moe_b_group, moe_w_expert, moe_b_expert, moe_w_gate, moe_w_up, moe_w_down, final_norm_w):
    B, n_lat, D = x.shape
    depth = w_mod.shape[0]
    ropes = (_rope_table(n_lat, MLA_ROPE), _rope_table(n_lat, RET_KEY))
    x_all = jnp.concatenate([ctx, x], axis=1)
    moe_gate_bf16, moe_up_bf16, moe_down_bf16 = (w.astype(BF16) for w in (moe_w_gate, moe_w_up, moe_w_down))
    for l in range(depth):
        with_ctx = l < depth - 1
        p = {
            'rwkv_conv': rwkv_conv[l], 'rwkv_w0': rwkv_w0[l], 'rwkv_w_up': rwkv_w_up[l],
            'rwkv_a0': rwkv_a0[l], 'rwkv_a_up': rwkv_a_up[l], 'rwkv_g_up': rwkv_g_up[l],
            'rwkv_k_k': rwkv_k_k[l], 'rwkv_k_a': rwkv_k_a[l], 'rwkv_r_k': rwkv_r_k[l].reshape(RWKV_HEADS, HEAD_DIM),
            'rwkv_ln_w': rwkv_ln_w[l], 'rwkv_ln_b': rwkv_ln_b[l],
            'mla_q_norm': mla_q_norm[l], 'mla_w_uq': mla_w_uq[l], 'mla_kv_norm': mla_kv_norm[l],
            'mla_w_ukv': mla_w_ukv[l], 'ret_decay': ret_decay[l],
        }
        mod_lat = jax.nn.silu(c) @ w_mod[l] + b_mod[l]
        mod_ctx = jax.nn.silu(c_ctx) @ w_mod[l] + b_mod[l]
        mods = jnp.stack([jnp.broadcast_to(mod_ctx, mod_lat.shape), mod_lat], axis=1).reshape(B, 2, N_MOD, D)
        mods = jnp.pad(mods, ((0, 0), (0, 0), (0, 8 - N_MOD), (0, 0)))

        w_a, w_b, w_c = _permute_w_in(w_in[l])
        h, u_rwkv, u_mla, u_ret = _inproj(x_all, mods, norm1_w[l][None, :], w_a, w_b, w_c)

        rwkv_parts = _rwkv_branch(u_rwkv, p)
        yb = _mla_branch(u_mla, p, ropes[0])
        ret_parts = _retention_branch(u_ret, p, ropes[1])
        head_idx = jnp.arange(RWKV_DIM) // HEAD_DIM
        rwkv_consts = [p['rwkv_r_k'].reshape(1, RWKV_DIM), p['rwkv_ln_w'][None, :], p['rwkv_ln_b'][None, :],
                       (head_idx[:, None] == head_idx[None, :]).astype(BF16)]

        w_router = jnp.pad(jnp.concatenate([moe_w_group[l], moe_w_expert[l]], axis=1),
                           ((0, 0), (0, ROUTER_LANES - N_GROUPS - N_EXPERTS)))
        w_router_hi = w_router.astype(BF16)
        w_router = jnp.stack([w_router_hi, (w_router - w_router_hi.astype(F32)).astype(BF16)])
        b_router = jnp.pad(jnp.concatenate([moe_b_group[l], moe_b_expert[l]]),
                           (0, ROUTER_LANES - N_GROUPS - N_EXPERTS))[None, :]
        tile_off = 0 if with_ctx else CTX_LEN // ROW_TILE
        weights = [w_branch_gate[l].astype(BF16), b_branch_gate[l][None, :],
                   w_branch_a[l].astype(BF16), w_branch_b[l].astype(BF16), w_branch_c[l].astype(BF16),
                   w_out[l].astype(BF16), norm2_w[l][None, :], w_router, b_router]
        x_mid, h2, logits = _merge(x_all, h, yb, mods, rwkv_parts, rwkv_consts, ret_parts, u_ret, weights, tile_off)
        Lm = x_mid.shape[1]
        out_a, out_b = _moe(h2.reshape(B * Lm, D), logits.reshape(B * Lm, ROUTER_LANES),
                            moe_gate_bf16, moe_up_bf16, moe_down_bf16, l)
        x_all = _combine(x_mid, out_a.reshape(B, Lm, D), out_b.reshape(B, Lm, D), logits, mods,
                         final_norm_w[None, :], tile_off, final=not with_ctx)
    return x_all
```

```python
import functools

import jax
import jax.numpy as jnp
from jax import lax
from jax.experimental import pallas as pl
from jax.experimental.pallas import tpu as pltpu

D_MODEL = 1024
CTX_LEN = 256
GRID_W = 64
N_MOD = 6
RMS_EPS = 1e-6
ROPE_BASE = 10000.0
HEAD_DIM = 64
F32 = jnp.float32
BF16 = jnp.bfloat16

RWKV_HEADS = 4
RWKV_DIM = RWKV_HEADS * HEAD_DIM
DECAY_LORA = 64
AAA_LORA = 64
GATE_LORA = 128
RWKV_GN_EPS = 64e-5

MLA_HEADS = 8
MLA_NOPE = 64
MLA_ROPE = 32
MLA_QK = MLA_NOPE + MLA_ROPE
MLA_V = 64
MLA_Q_RANK = 256
MLA_KV_RANK = 128
MLA_DIM = MLA_HEADS * MLA_V

RET_HEADS = 4
RET_KEY = 64
RET_VAL = 64
RET_DIM = RET_HEADS * RET_VAL

N_GROUPS = 4
EXPERTS_PER_GROUP = 8
N_EXPERTS = N_GROUPS * EXPERTS_PER_GROUP
TOP_K = 2
EXPERT_HIDDEN = 512

U_RWKV = 3 * RWKV_DIM + 2 * DECAY_LORA + 2 * AAA_LORA + GATE_LORA
U_MLA_RAW = MLA_Q_RANK + MLA_KV_RANK + MLA_ROPE
U_MLA = 512
U_RET = 4 * RET_DIM

ROW_TILE = 256
RWKV_CHUNK = 64
RET_CHUNK = 256
ROUTER_LANES = 128
MOE_TILE = 256
VMEM_LIMIT = 56 * 1024 * 1024


def _dot(a, b):
    return jnp.dot(a, b, preferred_element_type=F32)


def _dot_nt(a, b):
    return lax.dot_general(a, b, (((1,), (1,)), ((), ())), preferred_element_type=F32)


def _inproj_kernel(x_ref, mod_ref, nw_ref, wa_ref, wb_ref, wc_ref, h_ref, ua_ref, ub_ref, uc_ref):
    x = x_ref[0]
    y = x * lax.rsqrt(jnp.mean(x * x, axis=-1, keepdims=True) + RMS_EPS) * nw_ref[...]
    h = y * (1.0 + mod_ref[0, 0, 1:2, :]) + mod_ref[0, 0, 0:1, :]
    hb = h.astype(BF16)
    h_ref[0] = hb
    ua_ref[0] = _dot(hb, wa_ref[...])
    ub_ref[0] = _dot(hb, wb_ref[...])
    uc_ref[0] = _dot(hb, wc_ref[...])


def _inproj(x_all, mods, norm_w, w_a, w_b, w_c):
    B, T, D = x_all.shape
    nt = T // ROW_TILE
    tok = lambda w: pl.BlockSpec((1, ROW_TILE, w), lambda b, j: (b, j, 0))
    full = lambda a: pl.BlockSpec(a.shape, lambda b, j: (0,) * a.ndim)
    return pl.pallas_call(
        _inproj_kernel,
        grid=(B, nt),
        in_specs=[tok(D),
                  pl.BlockSpec((1, 1, 8, D), lambda b, j: (b, jnp.minimum(j, 1), 0, 0)),
                  full(norm_w), full(w_a), full(w_b), full(w_c)],
        out_specs=[tok(D), tok(U_RWKV), tok(U_MLA), tok(U_RET)],
        out_shape=[jax.ShapeDtypeStruct((B, T, D), BF16),
                   jax.ShapeDtypeStruct((B, T, U_RWKV), F32),
                   jax.ShapeDtypeStruct((B, T, U_MLA), F32),
                   jax.ShapeDtypeStruct((B, T, U_RET), F32)],
        compiler_params=pltpu.CompilerParams(
            dimension_semantics=("parallel", "parallel"), vmem_limit_bytes=VMEM_LIMIT),
        name="inproj",
    )(x_all, mods, norm_w, w_a, w_b, w_c)


def _split3(x):
    hi = x.astype(BF16)
    rem = x - hi.astype(F32)
    mid = rem.astype(BF16)
    return hi, mid, (rem - mid.astype(F32)).astype(BF16)


def _sums_right(x, mask01):
    hi, mid, lo = _split3(x)
    return _dot(hi, mask01) + (_dot(mid, mask01) + _dot(lo, mask01))


def _softplus(z):
    return jnp.maximum(z, 0.0) + jnp.log(1.0 + jnp.exp(-jnp.abs(z)))


def _rwkv_prep_kernel(u_ref, up_ref, un_ref, conv_ref, kk_ref, ka_ref, w0_ref, a0_ref, wup_ref, aup_ref,
                      gup_ref, ones_ref, tril_ref, triu_ref,
                      r_ref, k_ref, v_ref, vTp_ref, kap_ref, g_ref,
                      w_ref, beta_ref, kaph_ref, kch_ref, bch_ref, rh_ref, kend_ref):
    R = RWKV_DIM
    j = pl.program_id(1)
    u = u_ref[0]
    x = u[:, :3 * R]
    prev_row = jnp.where(j >= 2, up_ref[0, 7:8, :3 * R], 0.0)
    next_row = jnp.where((j >= 1) & (j < pl.num_programs(1) - 1), un_ref[0, 0:1, :3 * R], 0.0)
    rid = lax.broadcasted_iota(jnp.int32, (ROW_TILE, 3 * R), 0)
    x_prev = jnp.where(rid == 0, prev_row, pltpu.roll(x, 1, 0))
    x_next = jnp.where(rid == ROW_TILE - 1, next_row, pltpu.roll(x, ROW_TILE - 1, 0))
    rkv = x_prev * conv_ref[0:1, :] + x * conv_ref[1:2, :] + x_next * conv_ref[2:3, :]
    r, k, v = rkv[:, :R], rkv[:, R:2 * R], rkv[:, 2 * R:]
    ones_blk = ones_ref[...]
    kk = k * kk_ref[...]
    kap = kk / jnp.maximum(jnp.sqrt(_sums_right(kk * kk, ones_blk)), 1e-12)
    r_ref[0] = r
    k_ref[0] = k
    v_ref[0] = v
    vT = v.T
    for c in range(ROW_TILE // RWKV_CHUNK):
        for hp in range(RWKV_HEADS // 2):
            vTp_ref[0, c, hp] = jnp.concatenate(
                [vT[HEAD_DIM * (2 * hp + s):HEAD_DIM * (2 * hp + s + 1), RWKV_CHUNK * c:RWKV_CHUNK * (c + 1)]
                 for s in range(2)], axis=1)
    kap_ref[0] = kap
    lo = 3 * R
    tw = jnp.tanh(u[:, lo:lo + 2 * DECAY_LORA]).astype(BF16)
    ad = u[:, lo + 2 * DECAY_LORA:lo + 2 * DECAY_LORA + 2 * AAA_LORA].astype(BF16)
    g_ref[0] = _dot(jax.nn.sigmoid(u[:, lo + 2 * DECAY_LORA + 2 * AAA_LORA:]).astype(BF16), gup_ref[...])
    for d in range(2):
        log_w = -_softplus(-(w0_ref[d:d + 1, :] + _dot(tw, wup_ref[d]))) - 0.5
        logw = -jnp.exp(log_w)
        a = jax.nn.sigmoid(a0_ref[d:d + 1, :] + _dot(ad, aup_ref[d]))
        kmod = k * (1.0 + (a - 1.0) * ka_ref[...])
        beta = kap * a
        hi, mid, lo = _split3(logw)
        tri = tril_ref[...] if d == 0 else triu_ref[...]
        cs = _dot(tri, hi) + (_dot(tri, mid) + _dot(tri, lo))
        tot = _dot(ones_blk, hi) + (_dot(ones_blk, mid) + _dot(ones_blk, lo))
        inv = jnp.exp(-cs)
        w_ref[d, 0] = jnp.exp(logw)
        beta_ref[d, 0] = beta
        kaph_ref[d, 0] = kap * jnp.exp(cs - logw)
        kch_ref[d, 0] = kmod * inv
        bch_ref[d, 0] = beta * inv
        rh_ref[d, 0] = r * jnp.exp(cs)
        kend_ref[d, 0] = kmod * jnp.exp(tot - cs)


def _rwkv_prep(u_rwkv, p):
    B, L, _ = u_rwkv.shape
    R = RWKV_DIM
    nt = L // ROW_TILE
    halo = ROW_TILE // 8
    n_halo = L // 8
    idx = jnp.arange(ROW_TILE)
    same_chunk = (idx[:, None] // RWKV_CHUNK) == (idx[None, :] // RWKV_CHUNK)
    ones_blk = same_chunk.astype(BF16)
    tril = (same_chunk & (idx[None, :] <= idx[:, None])).astype(BF16)
    triu = (same_chunk & (idx[None, :] >= idx[:, None])).astype(BF16)
    pad_rows = lambda w, d, n: jnp.pad(w, ((d * n, n - d * n), (0, 0)))
    wup = jnp.stack([pad_rows(p['rwkv_w_up'][d], d, DECAY_LORA) for d in range(2)]).astype(BF16)
    aup = jnp.stack([pad_rows(p['rwkv_a_up'][d], d, AAA_LORA) for d in range(2)]).astype(BF16)
    consts = [p['rwkv_conv'], p['rwkv_k_k'][None, :], p['rwkv_k_a'][None, :], p['rwkv_w0'], p['rwkv_a0'],
              wup, aup, p['rwkv_g_up'].astype(BF16), ones_blk, tril, triu]
    full = lambda a: pl.BlockSpec(a.shape, lambda b, j: (0,) * a.ndim)
    tok = pl.BlockSpec((1, ROW_TILE, R), lambda b, j: (b, j, 0))
    per_dir = pl.BlockSpec((2, 1, ROW_TILE, R), lambda b, j: (0, b, j, 0))
    tok_shape = jax.ShapeDtypeStruct((B, L, R), F32)
    dir_shape = jax.ShapeDtypeStruct((2, B, L, R), F32)
    vtp_block = (ROW_TILE // RWKV_CHUNK, RWKV_HEADS // 2, HEAD_DIM, 2 * RWKV_CHUNK)
    return pl.pallas_call(
        _rwkv_prep_kernel,
        grid=(B, nt),
        in_specs=[pl.BlockSpec((1, ROW_TILE, U_RWKV), lambda b, j: (b, j, 0)),
                  pl.BlockSpec((1, 8, U_RWKV), lambda b, j: (b, jnp.maximum(j * halo - 1, 0), 0)),
                  pl.BlockSpec((1, 8, U_RWKV), lambda b, j: (b, jnp.minimum((j + 1) * halo, n_halo - 1), 0))]
                 + [full(a) for a in consts],
        out_specs=[tok, tok, tok, pl.BlockSpec((1,) + vtp_block, lambda b, j: (b, j, 0, 0, 0)), tok, tok]
                  + [per_dir] * 7,
        out_shape=[tok_shape, tok_shape, tok_shape,
                   jax.ShapeDtypeStruct((B, L // RWKV_CHUNK) + vtp_block[1:], F32), tok_shape, tok_shape]
                  + [dir_shape] * 7,
        compiler_params=pltpu.CompilerParams(
            dimension_semantics=("parallel", "parallel"), vmem_limit_bytes=VMEM_LIMIT),
        name="rwkv_prep",
    )(u_rwkv, u_rwkv, u_rwkv, *consts)


RWKV_BATCH = 2
RWKV_ROWS = 2 * RWKV_CHUNK
SUBLANES = 8
RWKV_UNROLL = 8


def _rwkv_scan_kernel(*refs):
    T, N = RWKV_CHUNK, HEAD_DIM
    W = 2 * N
    ins, outs = refs[:20], refs[20:22]
    p_ref, mix_ref, seq_ref, y0_ref, cm_ref = refs[22:]
    n_pair = RWKV_HEADS // 2
    pairs = [(bi, d, hp) for bi in range(RWKV_BATCH) for d in range(2) for hp in range(n_pair)]

    @pl.when(pl.program_id(1) == 0)
    def _():
        p_ref[...] = jnp.zeros_like(p_ref)

    row_t = lax.broadcasted_iota(jnp.int32, (T, W), 0)
    col = lax.broadcasted_iota(jnp.int32, (T, W), 1)
    first = col < N
    col_j = jnp.where(first, col, col - N)
    before = (col_j < row_t, col_j > row_t)
    upto = (col_j <= row_t, col_j >= row_t)
    lane = lax.broadcasted_iota(jnp.int32, (SUBLANES, W), 1)
    low = lane < N
    lane_t = jnp.where(low, lane, lane - N)

    def blockdiag(x):
        return jnp.concatenate([jnp.where(first, x, 0.0), jnp.where(first, 0.0, x)], axis=0)

    for slot in range(2):
        for pc, (bi, d, hp) in enumerate(pairs):
            kap_r, vTp_r, v_r, w_r, beta_r, kaph_r, kch_r, bch_r, rh_r, kend_r = ins[10 * d:10 * d + 10]
            q = slot if d == 0 else 1 - slot
            rows, cols = slice(T * q, T * q + T), slice(W * hp, W * hp + W)
            kch2 = blockdiag(kch_r[0, bi, rows, cols])
            rh = rh_r[0, bi, rows, cols]
            b_m = jnp.where(before[d], _dot_nt(kaph_r[0, bi, rows, cols], kch2), 0.0)
            e_m = jnp.where(upto[d], _dot_nt(rh, kch2), 0.0)
            cm_ref[pc] = jnp.where(upto[d], _dot_nt(rh, blockdiag(bch_r[0, bi, rows, cols])), 0.0)
            mix_ref[pc] = _dot_nt(vTp_r[bi, q, hp], blockdiag(b_m))
            y0_ref[pc] = _dot_nt(rh, blockdiag(p_ref[pc])) + _dot(e_m, blockdiag(v_r[bi, rows, cols]))
            seq_ref[0, pc] = kap_r[bi, rows, cols]
            seq_ref[1, pc] = w_r[0, bi, rows, cols]
            seq_ref[2, pc] = beta_r[0, bi, rows, cols]

        def step(i, carry):
            for pc, (bi, d, hp) in enumerate(pairs):
                t = i if d == 0 else T - 1 - i
                at_t = lane_t == t
                kap_t, w_t, beta_t = (jnp.broadcast_to(seq_ref[op, pc, pl.ds(t, 1), :], (SUBLANES, W))
                                      for op in range(3))
                for s8 in range(N // SUBLANES):
                    sub = slice(SUBLANES * s8, SUBLANES * s8 + SUBLANES)
                    p, mix = p_ref[pc, sub, :], mix_ref[pc, sub, :]
                    prod = p * kap_t + jnp.where(at_t, mix, 0.0)
                    sa = jnp.where(low, jnp.sum(jnp.where(low, prod, 0.0), axis=-1, keepdims=True),
                                   jnp.sum(jnp.where(low, 0.0, prod), axis=-1, keepdims=True))
                    p_ref[pc, sub, :] = p * w_t - sa * beta_t
                    mix_ref[pc, sub, :] = jnp.where(at_t, sa, mix)
            return carry

        lax.fori_loop(0, T, step, 0, unroll=RWKV_UNROLL)

        for bi in range(RWKV_BATCH):
            for d in range(2):
                vTp_r, kend_r = ins[10 * d + 1], ins[10 * d + 9]
                q = slot if d == 0 else 1 - slot
                rows = slice(T * q, T * q + T)
                ys = []
                for hp in range(n_pair):
                    pc = (bi * 2 + d) * n_pair + hp
                    cols = slice(W * hp, W * hp + W)
                    ys.append(y0_ref[pc] - _dot_nt(cm_ref[pc], blockdiag(mix_ref[pc])))
                    p_ref[pc] = p_ref[pc] + _dot(vTp_r[bi, q, hp], blockdiag(kend_r[0, bi, rows, cols]))
                outs[d][bi, rows, :] = jnp.concatenate(ys, axis=-1)


def _rwkv_scan(kap, vTp, v, per_dir):
    B, L, R = kap.shape
    nb = L // RWKV_ROWS
    n_ctx = CTX_LEN // RWKV_ROWS
    chunks = RWKV_ROWS // RWKV_CHUNK
    fwd = lambda c: c
    bwd = lambda c: jnp.where(c < n_ctx, n_ctx - 1 - c, nb + n_ctx - 1 - c)
    in_specs, args = [], []
    for d, blk in enumerate((fwd, bwd)):
        tok = pl.BlockSpec((RWKV_BATCH, RWKV_ROWS, R), lambda b, c, blk=blk: (b, blk(c), 0))
        tr = pl.BlockSpec((RWKV_BATCH, chunks) + vTp.shape[2:], lambda b, c, blk=blk: (b, blk(c), 0, 0, 0))
        dirs = pl.BlockSpec((1, RWKV_BATCH, RWKV_ROWS, R), lambda b, c, blk=blk, d=d: (d, b, blk(c), 0))
        in_specs += [tok, tr, tok] + [dirs] * 7
        args += [kap, vTp, v] + list(per_dir)
    out_specs = [pl.BlockSpec((RWKV_BATCH, RWKV_ROWS, R), lambda b, c, blk=blk: (b, blk(c), 0))
                 for blk in (fwd, bwd)]
    pair_tile = (RWKV_BATCH * RWKV_HEADS, HEAD_DIM, 2 * HEAD_DIM)
    return pl.pallas_call(
        _rwkv_scan_kernel,
        grid=(B // RWKV_BATCH, nb),
        in_specs=in_specs,
        out_specs=out_specs,
        out_shape=[jax.ShapeDtypeStruct((B, L, R), F32)] * 2,
        scratch_shapes=[pltpu.VMEM(pair_tile, F32), pltpu.VMEM(pair_tile, F32), pltpu.VMEM((3,) + pair_tile, F32),
                        pltpu.VMEM(pair_tile, F32), pltpu.VMEM(pair_tile, F32)],
        compiler_params=pltpu.CompilerParams(
            dimension_semantics=("parallel", "arbitrary"), vmem_limit_bytes=VMEM_LIMIT),
        name="rwkv_scan",
    )(*args)


def _rwkv_out_rows(y, r, k, v, g, rk, lnw, lnb, ones_blk):
    mu = _sums_right(y, ones_blk) * (1.0 / HEAD_DIM)
    yc = y - mu
    var = _sums_right(yc * yc, ones_blk) * (1.0 / HEAD_DIM)
    yn = yc * lax.rsqrt(var + RWKV_GN_EPS) * lnw + lnb
    bonus = _sums_right(r * k * rk, ones_blk) * v
    return ((yn + bonus) * g).astype(BF16)


MLA_SLOT = 128
MLA_WIDE = MLA_HEADS * MLA_SLOT


def _rope_mla(x, cos, sin_signed):
    half = MLA_ROPE // 2
    lane = lax.broadcasted_iota(jnp.int32, x.shape, 1) % MLA_SLOT
    partner = jnp.where(lane < MLA_NOPE + half, pltpu.roll(x, x.shape[1] - half, 1), pltpu.roll(x, half, 1))
    return x * cos + partner * sin_signed


def _mla_prep_kernel(u_ref, cos_ref, sin_ref, qn_ref, kvn_ref, wq_ref, wk_ref, wv_ref, q_ref, k_ref, v_ref):
    u = u_ref[0]
    q_dn = u[:, :MLA_Q_RANK]
    kv_dn = u[:, MLA_Q_RANK:MLA_Q_RANK + MLA_KV_RANK]
    k_rot = u[:, MLA_Q_RANK + MLA_KV_RANK:]
    norm = lambda t, w: (t * lax.rsqrt(jnp.mean(t * t, axis=-1, keepdims=True) + RMS_EPS) * w).astype(BF16)
    qn, kvn = norm(q_dn, qn_ref[...]), norm(kv_dn, kvn_ref[...])
    cos, sin = cos_ref[...], sin_ref[...]
    wide = lambda t: jnp.concatenate([t] * MLA_HEADS, axis=1)
    q_ref[0] = (_rope_mla(_dot(qn, wq_ref[...]), wide(cos), wide(sin)) * (MLA_QK ** -0.5)).astype(BF16)
    k_ref[0] = (_dot(kvn, wk_ref[...]) + wide(_rope_mla(k_rot, cos, sin))).astype(BF16)
    v = _dot(kvn, wv_ref[...])
    lane = lax.broadcasted_iota(jnp.int32, v.shape, 1) % MLA_SLOT
    v_ref[0] = jnp.where(lane == MLA_V, 1.0, v).astype(BF16)


def _mla_prep(u_mla, cos_rows, sin_rows, p):
    B, L, _ = u_mla.shape
    slots = lambda w, width: jnp.pad(w.reshape(w.shape[0], MLA_HEADS, width),
                                     ((0, 0), (0, 0), (0, MLA_SLOT - width))).reshape(w.shape[0], MLA_WIDE)
    w_ukv = p['mla_w_ukv'].reshape(MLA_KV_RANK, MLA_HEADS, MLA_NOPE + MLA_V)
    consts = [p['mla_q_norm'][None, :], p['mla_kv_norm'][None, :],
              slots(p['mla_w_uq'], MLA_QK).astype(BF16),
              slots(w_ukv[:, :, :MLA_NOPE].reshape(MLA_KV_RANK, -1), MLA_NOPE).astype(BF16),
              slots(w_ukv[:, :, MLA_NOPE:].reshape(MLA_KV_RANK, -1), MLA_V).astype(BF16)]
    tok = lambda w: pl.BlockSpec((1, ROW_TILE, w), lambda b, j: (b, j, 0))
    table = pl.BlockSpec((ROW_TILE, MLA_SLOT), lambda b, j: (j, 0))
    return pl.pallas_call(
        _mla_prep_kernel,
        grid=(B, L // ROW_TILE),
        in_specs=[tok(U_MLA), table, table] + [pl.BlockSpec(a.shape, lambda b, j: (0, 0)) for a in consts],
        out_specs=[tok(MLA_WIDE)] * 3,
        out_shape=[jax.ShapeDtypeStruct((B, L, MLA_WIDE), BF16)] * 3,
        compiler_params=pltpu.CompilerParams(
            dimension_semantics=("parallel", "parallel"), vmem_limit_bytes=VMEM_LIMIT),
        name="mla_prep",
    )(u_mla, cos_rows, sin_rows, *consts)


ATTN_Q_TILES = 2


def _attn_kernel(*refs):
    q_refs, (k_ref, v_ref, o_ref) = refs[:-3], refs[-3:]
    outs = []
    for h in range(MLA_HEADS):
        cols = slice(MLA_SLOT * h, MLA_SLOT * h + MLA_SLOT)
        q = jnp.concatenate([r[0, :, cols] for r in q_refs], axis=0)
        s = _dot_nt(q, k_ref[0, :, cols])
        p = jnp.exp((s - jnp.max(s, axis=-1, keepdims=True)).astype(BF16))
        o = _dot(p, v_ref[0, :, cols])
        outs.append(o[:, :MLA_V] / o[:, MLA_V:MLA_V + 1])
    o_ref[0] = jnp.concatenate(outs, axis=-1).astype(o_ref.dtype)


def _attention(q, k, v):
    B, L, W = q.shape
    params = pltpu.CompilerParams(dimension_semantics=("parallel", "parallel"), vmem_limit_bytes=VMEM_LIMIT)
    ctx_blk = lambda w: pl.BlockSpec((1, CTX_LEN, w), lambda b, i: (b, 0, 0))
    y_ctx = pl.pallas_call(
        _attn_kernel, grid=(B, 1), in_specs=[ctx_blk(W)] * 3, out_specs=ctx_blk(MLA_DIM),
        out_shape=jax.ShapeDtypeStruct((B, CTX_LEN, MLA_DIM), BF16), compiler_params=params,
        name="mla_attention_ctx",
    )(q, k, v)
    n_ctx = CTX_LEN // ROW_TILE
    rows = ATTN_Q_TILES * ROW_TILE
    resident = lambda: pl.BlockSpec((1, L, W), lambda b, i: (b, 0, 0), pipeline_mode=pl.Buffered(1))
    q_specs = [pl.BlockSpec((1, ROW_TILE, W), lambda b, i, t=t: (b, ATTN_Q_TILES * i + n_ctx + t, 0))
               for t in range(ATTN_Q_TILES)]
    y_lat = pl.pallas_call(
        _attn_kernel, grid=(B, (L - CTX_LEN) // rows), in_specs=q_specs + [resident(), resident()],
        out_specs=pl.BlockSpec((1, rows, MLA_DIM), lambda b, i: (b, i, 0)),
        out_shape=jax.ShapeDtypeStruct((B, L - CTX_LEN, MLA_DIM), BF16), compiler_params=params,
        name="mla_attention",
    )(*([q] * ATTN_Q_TILES), k, v)
    return jnp.concatenate([y_ctx, y_lat], axis=1)


def _rope_rows(x, cos, sin_signed):
    lane = lax.broadcasted_iota(jnp.int32, x.shape, 1)
    partner = jnp.where(lane % RET_KEY < RET_KEY // 2,
                        pltpu.roll(x, x.shape[1] - RET_KEY // 2, 1), pltpu.roll(x, RET_KEY // 2, 1))
    return x * cos + partner * sin_signed


def _ret_kernel(*refs):
    ins, (inner_ref, cross_ref, tailT_ref, cdec_ref), outs, r_ref = refs[:10], refs[10:14], refs[14:16], refs[16]

    @pl.when(pl.program_id(1) == 0)
    def _():
        r_ref[...] = jnp.zeros_like(r_ref)

    for d in range(2):
        q_ref, k_ref, v_ref, cos_ref, sin_ref = ins[5 * d:5 * d + 5]
        cos, sin = cos_ref[...], sin_ref[...]
        q_all = _rope_rows(q_ref[0], cos, sin)
        k_all = _rope_rows(k_ref[0], cos, sin) * (RET_KEY ** -0.5)
        kT_all = k_all.T
        v_all = v_ref[0]
        os = []
        for h in range(RET_HEADS):
            cols = slice(RET_KEY * h, RET_KEY * h + RET_KEY)
            q = q_all[:, cols].astype(BF16)
            v = v_all[:, cols].astype(BF16)
            state = r_ref[d, h]
            s = _dot_nt(q, k_all[:, cols].astype(BF16)) * inner_ref[d, h]
            os.append(_dot(s.astype(BF16), v) + _dot(q, state.astype(BF16)) * cross_ref[d, h])
            r_ref[d, h] = state * cdec_ref[d, h] + _dot((kT_all[cols, :] * tailT_ref[d, h]).astype(BF16), v)
        outs[d][0] = jnp.concatenate(os, axis=-1)


def _retention_scan(u_ret, cos, sin_signed, inner, cross, tailT, cdec):
    B, L, _ = u_ret.shape
    C = RET_CHUNK
    nb = L // C
    n_ctx = CTX_LEN // C
    fwd = lambda c: c
    bwd = lambda c: jnp.where(c < n_ctx, n_ctx - 1 - c, nb + n_ctx - 1 - c)
    in_specs, args = [], []
    for blk in (fwd, bwd):
        in_specs += [pl.BlockSpec((1, C, RET_DIM), lambda b, c, blk=blk, i=i: (b, blk(c), i)) for i in range(3)]
        in_specs += [pl.BlockSpec((C, RET_DIM), lambda b, c, blk=blk: (blk(c), 0))] * 2
        args += [u_ret, u_ret, u_ret, cos, sin_signed]
    consts = [inner, cross, tailT, cdec]
    in_specs += [pl.BlockSpec(a.shape, lambda b, c: (0, 0, 0, 0)) for a in consts]
    return pl.pallas_call(
        _ret_kernel,
        grid=(B, nb),
        in_specs=in_specs,
        out_specs=[pl.BlockSpec((1, C, RET_DIM), lambda b, c, blk=blk: (b, blk(c), 0)) for blk in (fwd, bwd)],
        out_shape=[jax.ShapeDtypeStruct((B, L, RET_DIM), F32)] * 2,
        scratch_shapes=[pltpu.VMEM((2, RET_HEADS, RET_KEY, RET_VAL), F32)],
        compiler_params=pltpu.CompilerParams(
            dimension_semantics=("parallel", "arbitrary"), vmem_limit_bytes=VMEM_LIMIT),
        name="retention_scan",
    )(*args, *consts)


def _ret_out_rows(y, g, ones_blk):
    ms = _sums_right(y * y, ones_blk) * (1.0 / RET_VAL)
    return (jax.nn.silu(g) * (y * lax.rsqrt(ms + RMS_EPS))).astype(BF16)


def _route_rows(lg):
    lane = lax.broadcasted_iota(jnp.int32, lg.shape, 1).astype(F32)
    low = -3.0e38
    first_at = lambda t, m: jnp.min(jnp.where(t == m, lane, float(ROUTER_LANES)), axis=-1, keepdims=True)
    is_grp = lane < N_GROUPS
    gl = jnp.where(is_grp, lg, low)
    gmax = jnp.max(gl, axis=-1, keepdims=True)
    grp = first_at(gl, gmax)
    grp_p = 1.0 / jnp.sum(jnp.where(is_grp, jnp.exp(gl - gmax), 0.0), axis=-1, keepdims=True)
    lo = N_GROUPS + EXPERTS_PER_GROUP * grp
    in_grp = (lane >= lo) & (lane < lo + EXPERTS_PER_GROUP)
    el = jnp.where(in_grp, lg, low)
    emax = jnp.max(el, axis=-1, keepdims=True)
    e1 = first_at(el, emax)
    esum = jnp.sum(jnp.where(in_grp, jnp.exp(el - emax), 0.0), axis=-1, keepdims=True)
    rest = jnp.where(lane == e1, low, el)
    emax2 = jnp.max(rest, axis=-1, keepdims=True)
    e2 = first_at(rest, emax2)
    p1 = 1.0 / esum
    p2 = jnp.exp(emax2 - emax) / esum
    w1 = grp_p * p1 / (p1 + p2)
    w2 = grp_p * p2 / (p1 + p2)
    return jnp.where(lane == 0, e1 - N_GROUPS,
                     jnp.where(lane == 1, e2 - N_GROUPS, jnp.where(lane == 2, w1, jnp.where(lane == 3, w2, 0.0))))


def _merge_kernel(x_ref, h_ref, yb_ref, mod_ref,
                  rf_ref, rb_ref, rr_ref, rk_ref, rv_ref, rg_ref, rwk_ref, lnw_ref, lnb_ref, ones_ref,
                  of_ref, ob_ref, og_ref,
                  wg_ref, bg_ref, wa_ref, wb_ref, wc_ref, wo_ref, n2_ref, wr_ref, br_ref,
                  xo_ref, h2_ref, lg_ref):
    D = D_MODEL
    ones_blk = ones_ref[...]
    ya = _rwkv_out_rows(rf_ref[0] + rb_ref[0], rr_ref[0], rk_ref[0], rv_ref[0], rg_ref[0],
                        rwk_ref[...], lnw_ref[...], lnb_ref[...], ones_blk)
    yc = _ret_out_rows(of_ref[0] + ob_ref[0], og_ref[0], ones_blk)
    gates = jax.nn.sigmoid(_dot(h_ref[0], wg_ref[...]) + bg_ref[...])
    m = (gates[:, :D] * _dot(ya, wa_ref[...])
         + gates[:, D:2 * D] * _dot(yb_ref[0], wb_ref[...])
         + gates[:, 2 * D:] * _dot(yc, wc_ref[...]))
    xn = x_ref[0] + mod_ref[0, 0, 2:3, :] * _dot(m.astype(BF16), wo_ref[...])
    xo_ref[0] = xn
    y = xn * lax.rsqrt(jnp.mean(xn * xn, axis=-1, keepdims=True) + RMS_EPS) * n2_ref[...]
    h2 = y * (1.0 + mod_ref[0, 0, 4:5, :]) + mod_ref[0, 0, 3:4, :]
    h2_hi = h2.astype(BF16)
    h2_ref[0] = h2_hi
    h2_lo = (h2 - h2_hi.astype(F32)).astype(BF16)
    lg_ref[0] = _route_rows(
        (_dot(h2_hi, wr_ref[0]) + (_dot(h2_lo, wr_ref[0]) + _dot(h2_hi, wr_ref[1]))) + br_ref[...])


def _merge(x_all, h, yb, mods, rwkv_parts, rwkv_consts, ret_parts, u_ret, weights, tile_off):
    B, T, D = x_all.shape
    nt = T // ROW_TILE - tile_off
    tok = lambda w: pl.BlockSpec((1, ROW_TILE, w), lambda b, j: (b, j + tile_off, 0))
    out = lambda w: pl.BlockSpec((1, ROW_TILE, w), lambda b, j: (b, j, 0))
    full = lambda a: pl.BlockSpec(a.shape, lambda b, j: (0,) * a.ndim)
    ret_gate = pl.BlockSpec((1, ROW_TILE, RET_DIM), lambda b, j: (b, j + tile_off, 3))
    return pl.pallas_call(
        _merge_kernel,
        grid=(B, nt),
        in_specs=[tok(D), tok(D), tok(MLA_DIM),
                  pl.BlockSpec((1, 1, 8, D), lambda b, j: (b, jnp.minimum(j + tile_off, 1), 0, 0))]
                 + [tok(RWKV_DIM)] * 6 + [full(a) for a in rwkv_consts]
                 + [tok(RET_DIM)] * 2 + [ret_gate] + [full(a) for a in weights],
        out_specs=[out(D), out(D), out(ROUTER_LANES)],
        out_shape=[jax.ShapeDtypeStruct((B, nt * ROW_TILE, D), F32),
                   jax.ShapeDtypeStruct((B, nt * ROW_TILE, D), BF16),
                   jax.ShapeDtypeStruct((B, nt * ROW_TILE, ROUTER_LANES), F32)],
        compiler_params=pltpu.CompilerParams(
            dimension_semantics=("parallel", "parallel"), vmem_limit_bytes=VMEM_LIMIT),
        name="merge",
    )(x_all, h, yb, mods, *rwkv_parts, *rwkv_consts, *ret_parts, u_ret, *weights)


def _moe_kernel(te_ref, x_ref, wg_ref, wu_ref, wd_ref, o_ref, wg_s, wu_s, wd_s):
    i = pl.program_id(0)

    @pl.when(jnp.logical_or(i == 0, te_ref[i] != te_ref[jnp.maximum(i - 1, 0)]))
    def _():
        wg_s[...] = wg_ref[0, 0].astype(BF16)
        wu_s[...] = wu_ref[0, 0].astype(BF16)
        wd_s[...] = wd_ref[0, 0].astype(BF16)

    x = x_ref[...]
    hid = jax.nn.silu(_dot(x, wg_s[...])) * _dot(x, wu_s[...])
    o_ref[...] = _dot(hid.astype(BF16), wd_s[...])


def _moe_experts(tile_expert, x_sorted, w_gate, w_up, w_down, layer):
    n_rows, D = x_sorted.shape
    n_tiles = n_rows // MOE_TILE
    grid_spec = pltpu.PrefetchScalarGridSpec(
        num_scalar_prefetch=1,
        grid=(n_tiles,),
        in_specs=[pl.BlockSpec((MOE_TILE, D), lambda i, te: (i, 0)),
                  pl.BlockSpec((1, 1, D, EXPERT_HIDDEN), lambda i, te: (layer, te[i], 0, 0)),
                  pl.BlockSpec((1, 1, D, EXPERT_HIDDEN), lambda i, te: (layer, te[i], 0, 0)),
                  pl.BlockSpec((1, 1, EXPERT_HIDDEN, D), lambda i, te: (layer, te[i], 0, 0))],
        out_specs=pl.BlockSpec((MOE_TILE, D), lambda i, te: (i, 0)),
        scratch_shapes=[pltpu.VMEM((D, EXPERT_HIDDEN), BF16), pltpu.VMEM((D, EXPERT_HIDDEN), BF16),
                        pltpu.VMEM((EXPERT_HIDDEN, D), BF16)],
    )
    return pl.pallas_call(
        _moe_kernel,
        grid_spec=grid_spec,
        out_shape=jax.ShapeDtypeStruct((n_rows, D), F32),
        compiler_params=pltpu.CompilerParams(
            dimension_semantics=("arbitrary",), vmem_limit_bytes=VMEM_LIMIT),
        name="moe_experts",
    )(tile_expert, x_sorted, w_gate, w_up, w_down)


def _combine_kernel(x_ref, a_ref, b_ref, rt_ref, mod_ref, nw_ref, o_ref, *, final):
    rt = rt_ref[0]
    f = rt[:, TOP_K:TOP_K + 1] * a_ref[0] + rt[:, TOP_K + 1:TOP_K + 2] * b_ref[0]
    x = x_ref[0] + mod_ref[0, 0, 5:6, :] * f
    if final:
        x = x * lax.rsqrt(jnp.mean(x * x, axis=-1, keepdims=True) + RMS_EPS) * nw_ref[...]
    o_ref[0] = x


def _combine(x_mid, out_a, out_b, routed, mods, norm_w, tile_off, final):
    B, Lm, D = x_mid.shape
    tok = lambda w: pl.BlockSpec((1, ROW_TILE, w), lambda b, j: (b, j, 0))
    return pl.pallas_call(
        functools.partial(_combine_kernel, final=final),
        grid=(B, Lm // ROW_TILE),
        in_specs=[tok(D), tok(D), tok(D), tok(ROUTER_LANES),
                  pl.BlockSpec((1, 1, 8, D), lambda b, j: (b, jnp.minimum(j + tile_off, 1), 0, 0)),
                  pl.BlockSpec((1, D), lambda b, j: (0, 0))],
        out_specs=tok(D),
        out_shape=jax.ShapeDtypeStruct((B, Lm, D), F32),
        compiler_params=pltpu.CompilerParams(dimension_semantics=("parallel", "parallel")),
        name="combine",
    )(x_mid, out_a, out_b, routed, mods, norm_w)


def _rope_table(n_tokens, rot_dim):
    rows = n_tokens // GRID_W
    row = jnp.repeat(jnp.arange(rows, dtype=F32), GRID_W)
    col = jnp.tile(jnp.arange(GRID_W, dtype=F32), rows)
    n_freq = rot_dim // 4
    inv_freq = ROPE_BASE ** (-jnp.arange(n_freq, dtype=F32) / n_freq)
    ang = jnp.concatenate([row[:, None] * inv_freq, col[:, None] * inv_freq], axis=-1)
    return jnp.cos(ang), jnp.sin(ang)


def _rwkv_branch(u_rwkv, p):
    r, k, v, vTp, kap, g, *per_dir = _rwkv_prep(u_rwkv, p)
    y_f, y_b = _rwkv_scan(kap, vTp, v, per_dir)
    return y_f, y_b, r, k, v, g


def _mla_branch(u_mla, p, rope):
    L = u_mla.shape[1]
    cos, sin = rope
    n_ctx = L - cos.shape[0]
    cos = jnp.concatenate([jnp.ones((n_ctx, cos.shape[1]), F32), cos], axis=0)
    sin = jnp.concatenate([jnp.zeros((n_ctx, sin.shape[1]), F32), sin], axis=0)
    tail = MLA_SLOT - MLA_NOPE - MLA_ROPE
    cos_rows = jnp.concatenate([jnp.ones((L, MLA_NOPE), F32), cos, cos, jnp.ones((L, tail), F32)], axis=1)
    sin_rows = jnp.concatenate([jnp.zeros((L, MLA_NOPE), F32), -sin, sin, jnp.zeros((L, tail), F32)], axis=1)
    q, k, v = _mla_prep(u_mla, cos_rows, sin_rows, p)
    return _attention(q, k, v)


def _retention_branch(u_ret, p, rope):
    B, L, _ = u_ret.shape
    C = RET_CHUNK
    cos, sin = rope
    n_ctx = L - cos.shape[0]
    cos = jnp.concatenate([jnp.ones((n_ctx, cos.shape[1]), F32), cos], axis=0)
    sin = jnp.concatenate([jnp.zeros((n_ctx, sin.shape[1]), F32), sin], axis=0)
    cos_rows = jnp.tile(cos, (1, 2 * RET_HEADS))
    sin_rows = jnp.tile(jnp.concatenate([-sin, sin], axis=1), (1, RET_HEADS))
    log_gamma = jax.nn.log_sigmoid(p['ret_decay'])
    idx = jnp.arange(C, dtype=F32)
    rel = idx[:, None] - idx[None, :]
    lg_f, lg_b = log_gamma[0][:, None, None], log_gamma[1][:, None, None]
    inner = jnp.stack([jnp.where(rel[None] >= 0, jnp.exp(lg_f * jnp.maximum(rel, 0.0)[None]), 0.0),
                       jnp.where(rel[None] <= 0, jnp.exp(lg_b * jnp.maximum(-rel, 0.0)[None]), 0.0)])
    cross = jnp.stack([jnp.exp(lg_f * (idx + 1.0)[None, :, None]), jnp.exp(lg_b * (C - idx)[None, :, None])])
    cross = jnp.broadcast_to(cross, (2, RET_HEADS, C, RET_VAL))
    tailT = jnp.stack([jnp.exp(lg_f * (C - 1.0 - idx)[None, None, :]), jnp.exp(lg_b * idx[None, None, :])])
    tailT = jnp.broadcast_to(tailT, (2, RET_HEADS, RET_KEY, C))
    cdec = jnp.broadcast_to(jnp.exp(log_gamma * C)[:, :, None, None], (2, RET_HEADS, RET_KEY, RET_VAL))
    o_f, o_b = _retention_scan(u_ret, cos_rows, sin_rows, inner, cross, tailT, cdec)
    return o_f, o_b


def _moe(h2, routed, w_gate, w_up, w_down, layer):
    n_tok, D = h2.shape
    n_assign = n_tok * TOP_K
    flat_e = routed[:, :TOP_K].astype(jnp.int32).reshape(n_assign)
    sorted_e, order = lax.sort((flat_e, jnp.arange(n_assign, dtype=jnp.int32)), num_keys=1, is_stable=True)
    bounds = jnp.searchsorted(sorted_e, jnp.arange(N_EXPERTS + 1, dtype=jnp.int32),
                              method='scan_unrolled').astype(jnp.int32)
    raw_start, counts = bounds[:-1], bounds[1:] - bounds[:-1]
    padded = ((counts + MOE_TILE - 1) // MOE_TILE) * MOE_TILE
    pad_end = jnp.cumsum(padded)
    pad_start = pad_end - padded
    rank = jnp.argsort(order).astype(jnp.int32)
    n_tiles = n_assign // MOE_TILE + N_EXPERTS
    tile_start = jnp.arange(n_tiles, dtype=jnp.int32) * MOE_TILE
    tile_expert = jnp.minimum(
        jnp.sum((tile_start[:, None] >= pad_end[None, :]).astype(jnp.int32), axis=1), N_EXPERTS - 1)
    first = (tile_start - pad_start[tile_expert] + raw_start[tile_expert])[:, None]
    last = (raw_start[tile_expert] + counts[tile_expert] - 1)[:, None]
    src_sorted = jnp.clip(jnp.minimum(first + jnp.arange(MOE_TILE, dtype=jnp.int32)[None, :], last), 0, n_assign - 1)
    src_tok = order[src_sorted.reshape(-1)] // TOP_K
    pos = (rank + (pad_start - raw_start)[flat_e]).reshape(n_tok, TOP_K)
    out = _moe_experts(tile_expert, h2[src_tok], w_gate, w_up, w_down, layer)
    return out[pos[:, 0]], out[pos[:, 1]]


def _permute_w_in(w_in):
    w_a = w_in[:, :U_RWKV]
    lo = U_RWKV + MLA_Q_RANK + MLA_KV_RANK
    w_b = jnp.concatenate([w_in[:, U_RWKV:lo], jnp.pad(w_in[:, lo:lo + MLA_ROPE], ((0, 0), (MLA_NOPE, MLA_SLOT - MLA_NOPE - MLA_ROPE)))], axis=1)
    w_c = w_in[:, U_RWKV + U_MLA_RAW:]
    return w_a.astype(BF16), w_b.astype(BF16), w_c.astype(BF16)


def kernel(x, c, ctx, c_ctx, w_mod, b_mod, norm1_w, w_in, rwkv_conv, rwkv_w0, rwkv_w_up, rwkv_a0, rwkv_a_up, rwkv_g_up, rwkv_k_k, rwkv_k_a, rwkv_r_k, rwkv_ln_w, rwkv_ln_b, mla_q_norm, mla_w_uq, mla_kv_norm, mla_w_ukv, ret_decay, w_branch_a, w_branch_b, w_branch_c, w_branch_gate, b_branch_gate, w_out, norm2_w, moe_w_group, moe_b_group, moe_w_expert, moe_b_expert, moe_w_gate, moe_w_up, moe_w_down, final_norm_w):
    B, n_lat, D = x.shape
    depth = w_mod.shape[0]
    ropes = (_rope_table(n_lat, MLA_ROPE), _rope_table(n_lat, RET_KEY))
    x_all = jnp.concatenate([ctx, x], axis=1)
    for l in range(depth):
        with_ctx = l < depth - 1
        p = {
            'rwkv_conv': rwkv_conv[l], 'rwkv_w0': rwkv_w0[l], 'rwkv_w_up': rwkv_w_up[l],
            'rwkv_a0': rwkv_a0[l], 'rwkv_a_up': rwkv_a_up[l], 'rwkv_g_up': rwkv_g_up[l],
            'rwkv_k_k': rwkv_k_k[l], 'rwkv_k_a': rwkv_k_a[l], 'rwkv_r_k': rwkv_r_k[l].reshape(RWKV_HEADS, HEAD_DIM),
            'rwkv_ln_w': rwkv_ln_w[l], 'rwkv_ln_b': rwkv_ln_b[l],
            'mla_q_norm': mla_q_norm[l], 'mla_w_uq': mla_w_uq[l], 'mla_kv_norm': mla_kv_norm[l],
            'mla_w_ukv': mla_w_ukv[l], 'ret_decay': ret_decay[l],
        }
        mod_lat = jax.nn.silu(c) @ w_mod[l] + b_mod[l]
        mod_ctx = jax.nn.silu(c_ctx) @ w_mod[l] + b_mod[l]
        mods = jnp.stack([jnp.broadcast_to(mod_ctx, mod_lat.shape), mod_lat], axis=1).reshape(B, 2, N_MOD, D)
        mods = jnp.pad(mods, ((0, 0), (0, 0), (0, 8 - N_MOD), (0, 0)))

        w_a, w_b, w_c = _permute_w_in(w_in[l])
        h, u_rwkv, u_mla, u_ret = _inproj(x_all, mods, norm1_w[l][None, :], w_a, w_b, w_c)

        rwkv_parts = _rwkv_branch(u_rwkv, p)
        yb = _mla_branch(u_mla, p, ropes[0])
        ret_parts = _retention_branch(u_ret, p, ropes[1])
        head_idx = jnp.arange(RWKV_DIM) // HEAD_DIM
        rwkv_consts = [p['rwkv_r_k'].reshape(1, RWKV_DIM), p['rwkv_ln_w'][None, :], p['rwkv_ln_b'][None, :],
                       (head_idx[:, None] == head_idx[None, :]).astype(BF16)]

        w_router = jnp.pad(jnp.concatenate([moe_w_group[l], moe_w_expert[l]], axis=1),
                           ((0, 0), (0, ROUTER_LANES - N_GROUPS - N_EXPERTS)))
        w_router_hi = w_router.astype(BF16)
        w_router = jnp.stack([w_router_hi, (w_router - w_router_hi.astype(F32)).astype(BF16)])
        b_router = jnp.pad(jnp.concatenate([moe_b_group[l], moe_b_expert[l]]),
                           (0, ROUTER_LANES - N_GROUPS - N_EXPERTS))[None, :]
        tile_off = 0 if with_ctx else CTX_LEN // ROW_TILE
        weights = [w_branch_gate[l].astype(BF16), b_branch_gate[l][None, :],
                   w_branch_a[l].astype(BF16), w_branch_b[l].astype(BF16), w_branch_c[l].astype(BF16),
                   w_out[l].astype(BF16), norm2_w[l][None, :], w_router, b_router]
        x_mid, h2, logits = _merge(x_all, h, yb, mods, rwkv_parts, rwkv_consts, ret_parts, u_ret, weights, tile_off)
        Lm = x_mid.shape[1]
        out_a, out_b = _moe(h2.reshape(B * Lm, D), logits.reshape(B * Lm, ROUTER_LANES),
                            moe_w_gate, moe_w_up, moe_w_down, l)
        x_all = _combine(x_mid, out_a.reshape(B, Lm, D), out_b.reshape(B, Lm, D), logits, mods,
                         final_norm_w[None, :], tile_off, final=not with_ctx)
    return x_all
```

```python
import functools

import jax
import jax.numpy as jnp
from jax import lax
from jax.experimental import pallas as pl
from jax.experimental.pallas import tpu as pltpu

D_MODEL = 1024
CTX_LEN = 256
GRID_W = 64
N_MOD = 6
RMS_EPS = 1e-6
ROPE_BASE = 10000.0
HEAD_DIM = 64
F32 = jnp.float32
BF16 = jnp.bfloat16

RWKV_HEADS = 4
RWKV_DIM = RWKV_HEADS * HEAD_DIM
DECAY_LORA = 64
AAA_LORA = 64
GATE_LORA = 128
RWKV_GN_EPS = 64e-5

MLA_HEADS = 8
MLA_NOPE = 64
MLA_ROPE = 32
MLA_QK = MLA_NOPE + MLA_ROPE
MLA_V = 64
MLA_Q_RANK = 256
MLA_KV_RANK = 128
MLA_DIM = MLA_HEADS * MLA_V

RET_HEADS = 4
RET_KEY = 64
RET_VAL = 64
RET_DIM = RET_HEADS * RET_VAL

N_GROUPS = 4
EXPERTS_PER_GROUP = 8
N_EXPERTS = N_GROUPS * EXPERTS_PER_GROUP
TOP_K = 2
EXPERT_HIDDEN = 512

U_RWKV = 3 * RWKV_DIM + 2 * DECAY_LORA + 2 * AAA_LORA + GATE_LORA
U_MLA_RAW = MLA_Q_RANK + MLA_KV_RANK + MLA_ROPE
U_MLA = 512
U_RET = 4 * RET_DIM

ROW_TILE = 256
RWKV_CHUNK = 64
RET_CHUNK = 256
ROUTER_LANES = 128
MOE_TILE = 256
VMEM_LIMIT = 56 * 1024 * 1024


def _dot(a, b):
    return jnp.dot(a, b, preferred_element_type=F32)


def _dot_nt(a, b):
    return lax.dot_general(a, b, (((1,), (1,)), ((), ())), preferred_element_type=F32)


def _inproj_kernel(x_ref, mod_ref, nw_ref, wa_ref, wb_ref, wc_ref, h_ref, ua_ref, ub_ref, uc_ref):
    x = x_ref[0]
    y = x * lax.rsqrt(jnp.mean(x * x, axis=-1, keepdims=True) + RMS_EPS) * nw_ref[...]
    h = y * (1.0 + mod_ref[0, 0, 1:2, :]) + mod_ref[0, 0, 0:1, :]
    hb = h.astype(BF16)
    h_ref[0] = hb
    ua_ref[0] = _dot(hb, wa_ref[...])
    ub_ref[0] = _dot(hb, wb_ref[...])
    uc_ref[0] = _dot(hb, wc_ref[...])


def _inproj(x_all, mods, norm_w, w_a, w_b, w_c):
    B, T, D = x_all.shape
    nt = T // ROW_TILE
    tok = lambda w: pl.BlockSpec((1, ROW_TILE, w), lambda b, j: (b, j, 0))
    full = lambda a: pl.BlockSpec(a.shape, lambda b, j: (0,) * a.ndim)
    return pl.pallas_call(
        _inproj_kernel,
        grid=(B, nt),
        in_specs=[tok(D),
                  pl.BlockSpec((1, 1, 8, D), lambda b, j: (b, jnp.minimum(j, 1), 0, 0)),
                  full(norm_w), full(w_a), full(w_b), full(w_c)],
        out_specs=[tok(D), tok(U_RWKV), tok(U_MLA), tok(U_RET)],
        out_shape=[jax.ShapeDtypeStruct((B, T, D), BF16),
                   jax.ShapeDtypeStruct((B, T, U_RWKV), F32),
                   jax.ShapeDtypeStruct((B, T, U_MLA), F32),
                   jax.ShapeDtypeStruct((B, T, U_RET), F32)],
        compiler_params=pltpu.CompilerParams(
            dimension_semantics=("parallel", "parallel"), vmem_limit_bytes=VMEM_LIMIT),
        name="inproj",
    )(x_all, mods, norm_w, w_a, w_b, w_c)


def _split3(x):
    hi = x.astype(BF16)
    rem = x - hi.astype(F32)
    mid = rem.astype(BF16)
    return hi, mid, (rem - mid.astype(F32)).astype(BF16)


def _sums_right(x, mask01):
    hi, mid, lo = _split3(x)
    return _dot(hi, mask01) + (_dot(mid, mask01) + _dot(lo, mask01))


def _softplus(z):
    return jnp.maximum(z, 0.0) + jnp.log(1.0 + jnp.exp(-jnp.abs(z)))


def _rwkv_prep_kernel(u_ref, up_ref, un_ref, conv_ref, kk_ref, ka_ref, w0_ref, a0_ref, wup_ref, aup_ref,
                      gup_ref, ones_ref, tril_ref, triu_ref,
                      r_ref, k_ref, v_ref, vTp_ref, kap_ref, g_ref,
                      w_ref, beta_ref, kaph_ref, kch_ref, bch_ref, rh_ref, kend_ref):
    R = RWKV_DIM
    j = pl.program_id(1)
    u = u_ref[0]
    x = u[:, :3 * R]
    prev_row = jnp.where(j >= 2, up_ref[0, 7:8, :3 * R], 0.0)
    next_row = jnp.where((j >= 1) & (j < pl.num_programs(1) - 1), un_ref[0, 0:1, :3 * R], 0.0)
    rid = lax.broadcasted_iota(jnp.int32, (ROW_TILE, 3 * R), 0)
    x_prev = jnp.where(rid == 0, prev_row, pltpu.roll(x, 1, 0))
    x_next = jnp.where(rid == ROW_TILE - 1, next_row, pltpu.roll(x, ROW_TILE - 1, 0))
    rkv = x_prev * conv_ref[0:1, :] + x * conv_ref[1:2, :] + x_next * conv_ref[2:3, :]
    r, k, v = rkv[:, :R], rkv[:, R:2 * R], rkv[:, 2 * R:]
    ones_blk = ones_ref[...]
    kk = k * kk_ref[...]
    kap = kk / jnp.maximum(jnp.sqrt(_sums_right(kk * kk, ones_blk)), 1e-12)
    r_ref[0] = r
    k_ref[0] = k
    v_ref[0] = v
    vT = v.T
    for c in range(ROW_TILE // RWKV_CHUNK):
        for hp in range(RWKV_HEADS // 2):
            vTp_ref[0, c, hp] = jnp.concatenate(
                [vT[HEAD_DIM * (2 * hp + s):HEAD_DIM * (2 * hp + s + 1), RWKV_CHUNK * c:RWKV_CHUNK * (c + 1)]
                 for s in range(2)], axis=1)
    kap_ref[0] = kap
    lo = 3 * R
    tw = jnp.tanh(u[:, lo:lo + 2 * DECAY_LORA]).astype(BF16)
    ad = u[:, lo + 2 * DECAY_LORA:lo + 2 * DECAY_LORA + 2 * AAA_LORA].astype(BF16)
    g_ref[0] = _dot(jax.nn.sigmoid(u[:, lo + 2 * DECAY_LORA + 2 * AAA_LORA:]).astype(BF16), gup_ref[...])
    for d in range(2):
        log_w = -_softplus(-(w0_ref[d:d + 1, :] + _dot(tw, wup_ref[d]))) - 0.5
        logw = -jnp.exp(log_w)
        a = jax.nn.sigmoid(a0_ref[d:d + 1, :] + _dot(ad, aup_ref[d]))
        kmod = k * (1.0 + (a - 1.0) * ka_ref[...])
        beta = kap * a
        hi, mid, lo = _split3(logw)
        tri = tril_ref[...] if d == 0 else triu_ref[...]
        cs = _dot(tri, hi) + (_dot(tri, mid) + _dot(tri, lo))
        tot = _dot(ones_blk, hi) + (_dot(ones_blk, mid) + _dot(ones_blk, lo))
        inv = jnp.exp(-cs)
        w_ref[d, 0] = jnp.exp(logw)
        beta_ref[d, 0] = beta
        kaph_ref[d, 0] = kap * jnp.exp(cs - logw)
        kch_ref[d, 0] = kmod * inv
        bch_ref[d, 0] = beta * inv
        rh_ref[d, 0] = r * jnp.exp(cs)
        kend_ref[d, 0] = kmod * jnp.exp(tot - cs)


def _rwkv_prep(u_rwkv, p):
    B, L, _ = u_rwkv.shape
    R = RWKV_DIM
    nt = L // ROW_TILE
    halo = ROW_TILE // 8
    n_halo = L // 8
    idx = jnp.arange(ROW_TILE)
    same_chunk = (idx[:, None] // RWKV_CHUNK) == (idx[None, :] // RWKV_CHUNK)
    ones_blk = same_chunk.astype(BF16)
    tril = (same_chunk & (idx[None, :] <= idx[:, None])).astype(BF16)
    triu = (same_chunk & (idx[None, :] >= idx[:, None])).astype(BF16)
    pad_rows = lambda w, d, n: jnp.pad(w, ((d * n, n - d * n), (0, 0)))
    wup = jnp.stack([pad_rows(p['rwkv_w_up'][d], d, DECAY_LORA) for d in range(2)]).astype(BF16)
    aup = jnp.stack([pad_rows(p['rwkv_a_up'][d], d, AAA_LORA) for d in range(2)]).astype(BF16)
    consts = [p['rwkv_conv'], p['rwkv_k_k'][None, :], p['rwkv_k_a'][None, :], p['rwkv_w0'], p['rwkv_a0'],
              wup, aup, p['rwkv_g_up'].astype(BF16), ones_blk, tril, triu]
    full = lambda a: pl.BlockSpec(a.shape, lambda b, j: (0,) * a.ndim)
    tok = pl.BlockSpec((1, ROW_TILE, R), lambda b, j: (b, j, 0))
    per_dir = pl.BlockSpec((2, 1, ROW_TILE, R), lambda b, j: (0, b, j, 0))
    tok_shape = jax.ShapeDtypeStruct((B, L, R), F32)
    dir_shape = jax.ShapeDtypeStruct((2, B, L, R), F32)
    vtp_block = (ROW_TILE // RWKV_CHUNK, RWKV_HEADS // 2, HEAD_DIM, 2 * RWKV_CHUNK)
    return pl.pallas_call(
        _rwkv_prep_kernel,
        grid=(B, nt),
        in_specs=[pl.BlockSpec((1, ROW_TILE, U_RWKV), lambda b, j: (b, j, 0)),
                  pl.BlockSpec((1, 8, U_RWKV), lambda b, j: (b, jnp.maximum(j * halo - 1, 0), 0)),
                  pl.BlockSpec((1, 8, U_RWKV), lambda b, j: (b, jnp.minimum((j + 1) * halo, n_halo - 1), 0))]
                 + [full(a) for a in consts],
        out_specs=[tok, tok, tok, pl.BlockSpec((1,) + vtp_block, lambda b, j: (b, j, 0, 0, 0)), tok, tok]
                  + [per_dir] * 7,
        out_shape=[tok_shape, tok_shape, tok_shape,
                   jax.ShapeDtypeStruct((B, L // RWKV_CHUNK) + vtp_block[1:], F32), tok_shape, tok_shape]
                  + [dir_shape] * 7,
        compiler_params=pltpu.CompilerParams(
            dimension_semantics=("parallel", "parallel"), vmem_limit_bytes=VMEM_LIMIT),
        name="rwkv_prep",
    )(u_rwkv, u_rwkv, u_rwkv, *consts)


RWKV_BATCH = 4
RWKV_ROWS = 2 * RWKV_CHUNK
SUBLANES = 8
RWKV_UNROLL = 8


def _rwkv_scan_kernel(*refs):
    T, N = RWKV_CHUNK, HEAD_DIM
    W = 2 * N
    ins, outs = refs[:20], refs[20:22]
    p_ref, mix_ref, seq_ref, y0_ref, cm_ref = refs[22:]
    n_pair = RWKV_HEADS // 2
    pairs = [(bi, d, hp) for bi in range(RWKV_BATCH) for d in range(2) for hp in range(n_pair)]

    @pl.when(pl.program_id(1) == 0)
    def _():
        p_ref[...] = jnp.zeros_like(p_ref)

    row_t = lax.broadcasted_iota(jnp.int32, (T, W), 0)
    col = lax.broadcasted_iota(jnp.int32, (T, W), 1)
    first = col < N
    col_j = jnp.where(first, col, col - N)
    before = (col_j < row_t, col_j > row_t)
    upto = (col_j <= row_t, col_j >= row_t)
    lane = lax.broadcasted_iota(jnp.int32, (SUBLANES, W), 1)
    low = lane < N
    lane_t = jnp.where(low, lane, lane - N)

    def blockdiag(x):
        return jnp.concatenate([jnp.where(first, x, 0.0), jnp.where(first, 0.0, x)], axis=0)

    for slot in range(2):
        for pc, (bi, d, hp) in enumerate(pairs):
            kap_r, vTp_r, v_r, w_r, beta_r, kaph_r, kch_r, bch_r, rh_r, kend_r = ins[10 * d:10 * d + 10]
            q = slot if d == 0 else 1 - slot
            rows, cols = slice(T * q, T * q + T), slice(W * hp, W * hp + W)
            kch2 = blockdiag(kch_r[0, bi, rows, cols])
            rh = rh_r[0, bi, rows, cols]
            b_m = jnp.where(before[d], _dot_nt(kaph_r[0, bi, rows, cols], kch2), 0.0)
            e_m = jnp.where(upto[d], _dot_nt(rh, kch2), 0.0)
            cm_ref[pc] = jnp.where(upto[d], _dot_nt(rh, blockdiag(bch_r[0, bi, rows, cols])), 0.0)
            mix_ref[pc] = _dot_nt(vTp_r[bi, q, hp], blockdiag(b_m))
            y0_ref[pc] = _dot_nt(rh, blockdiag(p_ref[pc])) + _dot(e_m, blockdiag(v_r[bi, rows, cols]))
            seq_ref[0, pc] = kap_r[bi, rows, cols]
            seq_ref[1, pc] = w_r[0, bi, rows, cols]
            seq_ref[2, pc] = beta_r[0, bi, rows, cols]

        def step(i, carry):
            for pc, (bi, d, hp) in enumerate(pairs):
                t = i if d == 0 else T - 1 - i
                at_t = lane_t == t
                kap_t, w_t, beta_t = (jnp.broadcast_to(seq_ref[op, pc, pl.ds(t, 1), :], (SUBLANES, W))
                                      for op in range(3))
                for s8 in range(N // SUBLANES):
                    sub = slice(SUBLANES * s8, SUBLANES * s8 + SUBLANES)
                    p, mix = p_ref[pc, sub, :], mix_ref[pc, sub, :]
                    prod = p * kap_t + jnp.where(at_t, mix, 0.0)
                    sa = jnp.where(low, jnp.sum(jnp.where(low, prod, 0.0), axis=-1, keepdims=True),
                                   jnp.sum(jnp.where(low, 0.0, prod), axis=-1, keepdims=True))
                    p_ref[pc, sub, :] = p * w_t - sa * beta_t
                    mix_ref[pc, sub, :] = jnp.where(at_t, sa, mix)
            return carry

        lax.fori_loop(0, T, step, 0, unroll=RWKV_UNROLL)

        for bi in range(RWKV_BATCH):
            for d in range(2):
                vTp_r, kend_r = ins[10 * d + 1], ins[10 * d + 9]
                q = slot if d == 0 else 1 - slot
                rows = slice(T * q, T * q + T)
                ys = []
                for hp in range(n_pair):
                    pc = (bi * 2 + d) * n_pair + hp
                    cols = slice(W * hp, W * hp + W)
                    ys.append(y0_ref[pc] - _dot_nt(cm_ref[pc], blockdiag(mix_ref[pc])))
                    p_ref[pc] = p_ref[pc] + _dot(vTp_r[bi, q, hp], blockdiag(kend_r[0, bi, rows, cols]))
                outs[d][bi, rows, :] = jnp.concatenate(ys, axis=-1)


def _rwkv_scan(kap, vTp, v, per_dir):
    B, L, R = kap.shape
    nb = L // RWKV_ROWS
    n_ctx = CTX_LEN // RWKV_ROWS
    chunks = RWKV_ROWS // RWKV_CHUNK
    fwd = lambda c: c
    bwd = lambda c: jnp.where(c < n_ctx, n_ctx - 1 - c, nb + n_ctx - 1 - c)
    in_specs, args = [], []
    for d, blk in enumerate((fwd, bwd)):
        tok = pl.BlockSpec((RWKV_BATCH, RWKV_ROWS, R), lambda b, c, blk=blk: (b, blk(c), 0))
        tr = pl.BlockSpec((RWKV_BATCH, chunks) + vTp.shape[2:], lambda b, c, blk=blk: (b, blk(c), 0, 0, 0))
        dirs = pl.BlockSpec((1, RWKV_BATCH, RWKV_ROWS, R), lambda b, c, blk=blk, d=d: (d, b, blk(c), 0))
        in_specs += [tok, tr, tok] + [dirs] * 7
        args += [kap, vTp, v] + list(per_dir)
    out_specs = [pl.BlockSpec((RWKV_BATCH, RWKV_ROWS, R), lambda b, c, blk=blk: (b, blk(c), 0))
                 for blk in (fwd, bwd)]
    pair_tile = (RWKV_BATCH * RWKV_HEADS, HEAD_DIM, 2 * HEAD_DIM)
    return pl.pallas_call(
        _rwkv_scan_kernel,
        grid=(B // RWKV_BATCH, nb),
        in_specs=in_specs,
        out_specs=out_specs,
        out_shape=[jax.ShapeDtypeStruct((B, L, R), F32)] * 2,
        scratch_shapes=[pltpu.VMEM(pair_tile, F32), pltpu.VMEM(pair_tile, F32), pltpu.VMEM((3,) + pair_tile, F32),
                        pltpu.VMEM(pair_tile, F32), pltpu.VMEM(pair_tile, F32)],
        compiler_params=pltpu.CompilerParams(
            dimension_semantics=("parallel", "arbitrary"), vmem_limit_bytes=VMEM_LIMIT),
        name="rwkv_scan",
    )(*args)


def _rwkv_out_rows(y, r, k, v, g, rk, lnw, lnb, ones_blk):
    mu = _sums_right(y, ones_blk) * (1.0 / HEAD_DIM)
    yc = y - mu
    var = _sums_right(yc * yc, ones_blk) * (1.0 / HEAD_DIM)
    yn = yc * lax.rsqrt(var + RWKV_GN_EPS) * lnw + lnb
    bonus = _sums_right(r * k * rk, ones_blk) * v
    return ((yn + bonus) * g).astype(BF16)


MLA_SLOT = 128
MLA_WIDE = MLA_HEADS * MLA_SLOT


def _rope_mla(x, cos, sin_signed):
    half = MLA_ROPE // 2
    lane = lax.broadcasted_iota(jnp.int32, x.shape, 1) % MLA_SLOT
    partner = jnp.where(lane < MLA_NOPE + half, pltpu.roll(x, x.shape[1] - half, 1), pltpu.roll(x, half, 1))
    return x * cos + partner * sin_signed


def _mla_prep_kernel(u_ref, cos_ref, sin_ref, qn_ref, kvn_ref, wq_ref, wk_ref, wv_ref, q_ref, k_ref, v_ref):
    u = u_ref[0]
    q_dn = u[:, :MLA_Q_RANK]
    kv_dn = u[:, MLA_Q_RANK:MLA_Q_RANK + MLA_KV_RANK]
    k_rot = u[:, MLA_Q_RANK + MLA_KV_RANK:]
    norm = lambda t, w: (t * lax.rsqrt(jnp.mean(t * t, axis=-1, keepdims=True) + RMS_EPS) * w).astype(BF16)
    qn, kvn = norm(q_dn, qn_ref[...]), norm(kv_dn, kvn_ref[...])
    cos, sin = cos_ref[...], sin_ref[...]
    wide = lambda t: jnp.concatenate([t] * MLA_HEADS, axis=1)
    q_ref[0] = (_rope_mla(_dot(qn, wq_ref[...]), wide(cos), wide(sin)) * (MLA_QK ** -0.5)).astype(BF16)
    k_ref[0] = (_dot(kvn, wk_ref[...]) + wide(_rope_mla(k_rot, cos, sin))).astype(BF16)
    v = _dot(kvn, wv_ref[...])
    lane = lax.broadcasted_iota(jnp.int32, v.shape, 1) % MLA_SLOT
    v_ref[0] = jnp.where(lane == MLA_V, 1.0, v).astype(BF16)


def _mla_prep(u_mla, cos_rows, sin_rows, p):
    B, L, _ = u_mla.shape
    slots = lambda w, width: jnp.pad(w.reshape(w.shape[0], MLA_HEADS, width),
                                     ((0, 0), (0, 0), (0, MLA_SLOT - width))).reshape(w.shape[0], MLA_WIDE)
    w_ukv = p['mla_w_ukv'].reshape(MLA_KV_RANK, MLA_HEADS, MLA_NOPE + MLA_V)
    consts = [p['mla_q_norm'][None, :], p['mla_kv_norm'][None, :],
              slots(p['mla_w_uq'], MLA_QK).astype(BF16),
              slots(w_ukv[:, :, :MLA_NOPE].reshape(MLA_KV_RANK, -1), MLA_NOPE).astype(BF16),
              slots(w_ukv[:, :, MLA_NOPE:].reshape(MLA_KV_RANK, -1), MLA_V).astype(BF16)]
    tok = lambda w: pl.BlockSpec((1, ROW_TILE, w), lambda b, j: (b, j, 0))
    table = pl.BlockSpec((ROW_TILE, MLA_SLOT), lambda b, j: (j, 0))
    return pl.pallas_call(
        _mla_prep_kernel,
        grid=(B, L // ROW_TILE),
        in_specs=[tok(U_MLA), table, table] + [pl.BlockSpec(a.shape, lambda b, j: (0, 0)) for a in consts],
        out_specs=[tok(MLA_WIDE)] * 3,
        out_shape=[jax.ShapeDtypeStruct((B, L, MLA_WIDE), BF16)] * 3,
        compiler_params=pltpu.CompilerParams(
            dimension_semantics=("parallel", "parallel"), vmem_limit_bytes=VMEM_LIMIT),
        name="mla_prep",
    )(u_mla, cos_rows, sin_rows, *consts)


ATTN_Q_TILES = 2


def _attn_kernel(*refs):
    q_refs, (k_ref, v_ref, o_ref) = refs[:-3], refs[-3:]
    outs = []
    for h in range(MLA_HEADS):
        cols = slice(MLA_SLOT * h, MLA_SLOT * h + MLA_SLOT)
        q = jnp.concatenate([r[0, :, cols] for r in q_refs], axis=0)
        s = _dot_nt(q, k_ref[0, :, cols])
        p = jnp.exp((s - jnp.max(s, axis=-1, keepdims=True)).astype(BF16))
        o = _dot(p, v_ref[0, :, cols])
        outs.append(o[:, :MLA_V] / o[:, MLA_V:MLA_V + 1])
    o_ref[0] = jnp.concatenate(outs, axis=-1).astype(o_ref.dtype)


def _attention(q, k, v):
    B, L, W = q.shape
    params = pltpu.CompilerParams(dimension_semantics=("parallel", "parallel"), vmem_limit_bytes=VMEM_LIMIT)
    ctx_blk = lambda w: pl.BlockSpec((1, CTX_LEN, w), lambda b, i: (b, 0, 0))
    y_ctx = pl.pallas_call(
        _attn_kernel, grid=(B, 1), in_specs=[ctx_blk(W)] * 3, out_specs=ctx_blk(MLA_DIM),
        out_shape=jax.ShapeDtypeStruct((B, CTX_LEN, MLA_DIM), BF16), compiler_params=params,
        name="mla_attention_ctx",
    )(q, k, v)
    n_ctx = CTX_LEN // ROW_TILE
    rows = ATTN_Q_TILES * ROW_TILE
    resident = lambda: pl.BlockSpec((1, L, W), lambda b, i: (b, 0, 0), pipeline_mode=pl.Buffered(1))
    q_specs = [pl.BlockSpec((1, ROW_TILE, W), lambda b, i, t=t: (b, ATTN_Q_TILES * i + n_ctx + t, 0))
               for t in range(ATTN_Q_TILES)]
    y_lat = pl.pallas_call(
        _attn_kernel, grid=(B, (L - CTX_LEN) // rows), in_specs=q_specs + [resident(), resident()],
        out_specs=pl.BlockSpec((1, rows, MLA_DIM), lambda b, i: (b, i, 0)),
        out_shape=jax.ShapeDtypeStruct((B, L - CTX_LEN, MLA_DIM), BF16), compiler_params=params,
        name="mla_attention",
    )(*([q] * ATTN_Q_TILES), k, v)
    return jnp.concatenate([y_ctx, y_lat], axis=1)


def _rope_rows(x, cos, sin_signed):
    lane = lax.broadcasted_iota(jnp.int32, x.shape, 1)
    partner = jnp.where(lane % RET_KEY < RET_KEY // 2,
                        pltpu.roll(x, x.shape[1] - RET_KEY // 2, 1), pltpu.roll(x, RET_KEY // 2, 1))
    return x * cos + partner * sin_signed


def _ret_kernel(*refs):
    ins, (inner_ref, cross_ref, tailT_ref, cdec_ref), outs, r_ref = refs[:10], refs[10:14], refs[14:16], refs[16]

    @pl.when(pl.program_id(1) == 0)
    def _():
        r_ref[...] = jnp.zeros_like(r_ref)

    for d in range(2):
        q_ref, k_ref, v_ref, cos_ref, sin_ref = ins[5 * d:5 * d + 5]
        cos, sin = cos_ref[...], sin_ref[...]
        q_all = _rope_rows(q_ref[0], cos, sin)
        k_all = _rope_rows(k_ref[0], cos, sin) * (RET_KEY ** -0.5)
        kT_all = k_all.T
        v_all = v_ref[0]
        os = []
        for h in range(RET_HEADS):
            cols = slice(RET_KEY * h, RET_KEY * h + RET_KEY)
            q = q_all[:, cols].astype(BF16)
            v = v_all[:, cols].astype(BF16)
            state = r_ref[d, h]
            s = _dot_nt(q, k_all[:, cols].astype(BF16)) * inner_ref[d, h]
            os.append(_dot(s.astype(BF16), v) + _dot(q, state.astype(BF16)) * cross_ref[d, h])
            r_ref[d, h] = state * cdec_ref[d, h] + _dot((kT_all[cols, :] * tailT_ref[d, h]).astype(BF16), v)
        outs[d][0] = jnp.concatenate(os, axis=-1)


def _retention_scan(u_ret, cos, sin_signed, inner, cross, tailT, cdec):
    B, L, _ = u_ret.shape
    C = RET_CHUNK
    nb = L // C
    n_ctx = CTX_LEN // C
    fwd = lambda c: c
    bwd = lambda c: jnp.where(c < n_ctx, n_ctx - 1 - c, nb + n_ctx - 1 - c)
    in_specs, args = [], []
    for blk in (fwd, bwd):
        in_specs += [pl.BlockSpec((1, C, RET_DIM), lambda b, c, blk=blk, i=i: (b, blk(c), i)) for i in range(3)]
        in_specs += [pl.BlockSpec((C, RET_DIM), lambda b, c, blk=blk: (blk(c), 0))] * 2
        args += [u_ret, u_ret, u_ret, cos, sin_signed]
    consts = [inner, cross, tailT, cdec]
    in_specs += [pl.BlockSpec(a.shape, lambda b, c: (0, 0, 0, 0)) for a in consts]
    return pl.pallas_call(
        _ret_kernel,
        grid=(B, nb),
        in_specs=in_specs,
        out_specs=[pl.BlockSpec((1, C, RET_DIM), lambda b, c, blk=blk: (b, blk(c), 0)) for blk in (fwd, bwd)],
        out_shape=[jax.ShapeDtypeStruct((B, L, RET_DIM), F32)] * 2,
        scratch_shapes=[pltpu.VMEM((2, RET_HEADS, RET_KEY, RET_VAL), F32)],
        compiler_params=pltpu.CompilerParams(
            dimension_semantics=("parallel", "arbitrary"), vmem_limit_bytes=VMEM_LIMIT),
        name="retention_scan",
    )(*args, *consts)


def _ret_out_rows(y, g, ones_blk):
    ms = _sums_right(y * y, ones_blk) * (1.0 / RET_VAL)
    return (jax.nn.silu(g) * (y * lax.rsqrt(ms + RMS_EPS))).astype(BF16)


def _route_rows(lg):
    lane = lax.broadcasted_iota(jnp.int32, lg.shape, 1).astype(F32)
    low = -3.0e38
    first_at = lambda t, m: jnp.min(jnp.where(t == m, lane, float(ROUTER_LANES)), axis=-1, keepdims=True)
    is_grp = lane < N_GROUPS
    gl = jnp.where(is_grp, lg, low)
    gmax = jnp.max(gl, axis=-1, keepdims=True)
    grp = first_at(gl, gmax)
    grp_p = 1.0 / jnp.sum(jnp.where(is_grp, jnp.exp(gl - gmax), 0.0), axis=-1, keepdims=True)
    lo = N_GROUPS + EXPERTS_PER_GROUP * grp
    in_grp = (lane >= lo) & (lane < lo + EXPERTS_PER_GROUP)
    el = jnp.where(in_grp, lg, low)
    emax = jnp.max(el, axis=-1, keepdims=True)
    e1 = first_at(el, emax)
    esum = jnp.sum(jnp.where(in_grp, jnp.exp(el - emax), 0.0), axis=-1, keepdims=True)
    rest = jnp.where(lane == e1, low, el)
    emax2 = jnp.max(rest, axis=-1, keepdims=True)
    e2 = first_at(rest, emax2)
    p1 = 1.0 / esum
    p2 = jnp.exp(emax2 - emax) / esum
    w1 = grp_p * p1 / (p1 + p2)
    w2 = grp_p * p2 / (p1 + p2)
    return jnp.where(lane == 0, e1 - N_GROUPS,
                     jnp.where(lane == 1, e2 - N_GROUPS, jnp.where(lane == 2, w1, jnp.where(lane == 3, w2, 0.0))))


def _merge_kernel(x_ref, h_ref, yb_ref, mod_ref,
                  rf_ref, rb_ref, rr_ref, rk_ref, rv_ref, rg_ref, rwk_ref, lnw_ref, lnb_ref, ones_ref,
                  of_ref, ob_ref, og_ref,
                  wg_ref, bg_ref, wa_ref, wb_ref, wc_ref, wo_ref, n2_ref, wr_ref, br_ref,
                  xo_ref, h2_ref, lg_ref):
    D = D_MODEL
    ones_blk = ones_ref[...]
    ya = _rwkv_out_rows(rf_ref[0] + rb_ref[0], rr_ref[0], rk_ref[0], rv_ref[0], rg_ref[0],
                        rwk_ref[...], lnw_ref[...], lnb_ref[...], ones_blk)
    yc = _ret_out_rows(of_ref[0] + ob_ref[0], og_ref[0], ones_blk)
    gates = jax.nn.sigmoid(_dot(h_ref[0], wg_ref[...]) + bg_ref[...])
    m = (gates[:, :D] * _dot(ya, wa_ref[...])
         + gates[:, D:2 * D] * _dot(yb_ref[0], wb_ref[...])
         + gates[:, 2 * D:] * _dot(yc, wc_ref[...]))
    xn = x_ref[0] + mod_ref[0, 0, 2:3, :] * _dot(m.astype(BF16), wo_ref[...])
    xo_ref[0] = xn
    y = xn * lax.rsqrt(jnp.mean(xn * xn, axis=-1, keepdims=True) + RMS_EPS) * n2_ref[...]
    h2 = y * (1.0 + mod_ref[0, 0, 4:5, :]) + mod_ref[0, 0, 3:4, :]
    h2_hi = h2.astype(BF16)
    h2_ref[0] = h2_hi
    h2_lo = (h2 - h2_hi.astype(F32)).astype(BF16)
    lg_ref[0] = _route_rows(
        (_dot(h2_hi, wr_ref[0]) + (_dot(h2_lo, wr_ref[0]) + _dot(h2_hi, wr_ref[1]))) + br_ref[...])


def _merge(x_all, h, yb, mods, rwkv_parts, rwkv_consts, ret_parts, u_ret, weights, tile_off):
    B, T, D = x_all.shape
    nt = T // ROW_TILE - tile_off
    tok = lambda w: pl.BlockSpec((1, ROW_TILE, w), lambda b, j: (b, j + tile_off, 0))
    out = lambda w: pl.BlockSpec((1, ROW_TILE, w), lambda b, j: (b, j, 0))
    full = lambda a: pl.BlockSpec(a.shape, lambda b, j: (0,) * a.ndim)
    ret_gate = pl.BlockSpec((1, ROW_TILE, RET_DIM), lambda b, j: (b, j + tile_off, 3))
    return pl.pallas_call(
        _merge_kernel,
        grid=(B, nt),
        in_specs=[tok(D), tok(D), tok(MLA_DIM),
                  pl.BlockSpec((1, 1, 8, D), lambda b, j: (b, jnp.minimum(j + tile_off, 1), 0, 0))]
                 + [tok(RWKV_DIM)] * 6 + [full(a) for a in rwkv_consts]
                 + [tok(RET_DIM)] * 2 + [ret_gate] + [full(a) for a in weights],
        out_specs=[out(D), out(D), out(ROUTER_LANES)],
        out_shape=[jax.ShapeDtypeStruct((B, nt * ROW_TILE, D), F32),
                   jax.ShapeDtypeStruct((B, nt * ROW_TILE, D), BF16),
                   jax.ShapeDtypeStruct((B, nt * ROW_TILE, ROUTER_LANES), F32)],
        compiler_params=pltpu.CompilerParams(
            dimension_semantics=("parallel", "parallel"), vmem_limit_bytes=VMEM_LIMIT),
        name="merge",
    )(x_all, h, yb, mods, *rwkv_parts, *rwkv_consts, *ret_parts, u_ret, *weights)


def _moe_kernel(te_ref, x_ref, wg_ref, wu_ref, wd_ref, o_ref, wg_s, wu_s, wd_s):
    i = pl.program_id(0)

    @pl.when(jnp.logical_or(i == 0, te_ref[i] != te_ref[jnp.maximum(i - 1, 0)]))
    def _():
        wg_s[...] = wg_ref[0, 0].astype(BF16)
        wu_s[...] = wu_ref[0, 0].astype(BF16)
        wd_s[...] = wd_ref[0, 0].astype(BF16)

    x = x_ref[...]
    hid = jax.nn.silu(_dot(x, wg_s[...])) * _dot(x, wu_s[...])
    o_ref[...] = _dot(hid.astype(BF16), wd_s[...])


def _moe_experts(tile_expert, x_sorted, w_gate, w_up, w_down, layer):
    n_rows, D = x_sorted.shape
    n_tiles = n_rows // MOE_TILE
    grid_spec = pltpu.PrefetchScalarGridSpec(
        num_scalar_prefetch=1,
        grid=(n_tiles,),
        in_specs=[pl.BlockSpec((MOE_TILE, D), lambda i, te: (i, 0)),
                  pl.BlockSpec((1, 1, D, EXPERT_HIDDEN), lambda i, te: (layer, te[i], 0, 0)),
                  pl.BlockSpec((1, 1, D, EXPERT_HIDDEN), lambda i, te: (layer, te[i], 0, 0)),
                  pl.BlockSpec((1, 1, EXPERT_HIDDEN, D), lambda i, te: (layer, te[i], 0, 0))],
        out_specs=pl.BlockSpec((MOE_TILE, D), lambda i, te: (i, 0)),
        scratch_shapes=[pltpu.VMEM((D, EXPERT_HIDDEN), BF16), pltpu.VMEM((D, EXPERT_HIDDEN), BF16),
                        pltpu.VMEM((EXPERT_HIDDEN, D), BF16)],
    )
    return pl.pallas_call(
        _moe_kernel,
        grid_spec=grid_spec,
        out_shape=jax.ShapeDtypeStruct((n_rows, D), F32),
        compiler_params=pltpu.CompilerParams(
            dimension_semantics=("arbitrary",), vmem_limit_bytes=VMEM_LIMIT),
        name="moe_experts",
    )(tile_expert, x_sorted, w_gate, w_up, w_down)


def _combine_kernel(x_ref, a_ref, b_ref, rt_ref, mod_ref, nw_ref, o_ref, *, final):
    rt = rt_ref[0]
    f = rt[:, TOP_K:TOP_K + 1] * a_ref[0] + rt[:, TOP_K + 1:TOP_K + 2] * b_ref[0]
    x = x_ref[0] + mod_ref[0, 0, 5:6, :] * f
    if final:
        x = x * lax.rsqrt(jnp.mean(x * x, axis=-1, keepdims=True) + RMS_EPS) * nw_ref[...]
    o_ref[0] = x


def _combine(x_mid, out_a, out_b, routed, mods, norm_w, tile_off, final):
    B, Lm, D = x_mid.shape
    tok = lambda w: pl.BlockSpec((1, ROW_TILE, w), lambda b, j: (b, j, 0))
    return pl.pallas_call(
        functools.partial(_combine_kernel, final=final),
        grid=(B, Lm // ROW_TILE),
        in_specs=[tok(D), tok(D), tok(D), tok(ROUTER_LANES),
                  pl.BlockSpec((1, 1, 8, D), lambda b, j: (b, jnp.minimum(j + tile_off, 1), 0, 0)),
                  pl.BlockSpec((1, D), lambda b, j: (0, 0))],
        out_specs=tok(D),
        out_shape=jax.ShapeDtypeStruct((B, Lm, D), F32),
        compiler_params=pltpu.CompilerParams(dimension_semantics=("parallel", "parallel")),
        name="combine",
    )(x_mid, out_a, out_b, routed, mods, norm_w)


def _rope_table(n_tokens, rot_dim):
    rows = n_tokens // GRID_W
    row = jnp.repeat(jnp.arange(rows, dtype=F32), GRID_W)
    col = jnp.tile(jnp.arange(GRID_W, dtype=F32), rows)
    n_freq = rot_dim // 4
    inv_freq = ROPE_BASE ** (-jnp.arange(n_freq, dtype=F32) / n_freq)
    ang = jnp.concatenate([row[:, None] * inv_freq, col[:, None] * inv_freq], axis=-1)
    return jnp.cos(ang), jnp.sin(ang)


def _rwkv_branch(u_rwkv, p):
    r, k, v, vTp, kap, g, *per_dir = _rwkv_prep(u_rwkv, p)
    y_f, y_b = _rwkv_scan(kap, vTp, v, per_dir)
    return y_f, y_b, r, k, v, g


def _mla_branch(u_mla, p, rope):
    L = u_mla.shape[1]
    cos, sin = rope
    n_ctx = L - cos.shape[0]
    cos = jnp.concatenate([jnp.ones((n_ctx, cos.shape[1]), F32), cos], axis=0)
    sin = jnp.concatenate([jnp.zeros((n_ctx, sin.shape[1]), F32), sin], axis=0)
    tail = MLA_SLOT - MLA_NOPE - MLA_ROPE
    cos_rows = jnp.concatenate([jnp.ones((L, MLA_NOPE), F32), cos, cos, jnp.ones((L, tail), F32)], axis=1)
    sin_rows = jnp.concatenate([jnp.zeros((L, MLA_NOPE), F32), -sin, sin, jnp.zeros((L, tail), F32)], axis=1)
    q, k, v = _mla_prep(u_mla, cos_rows, sin_rows, p)
    return _attention(q, k, v)


def _retention_branch(u_ret, p, rope):
    B, L, _ = u_ret.shape
    C = RET_CHUNK
    cos, sin = rope
    n_ctx = L - cos.shape[0]
    cos = jnp.concatenate([jnp.ones((n_ctx, cos.shape[1]), F32), cos], axis=0)
    sin = jnp.concatenate([jnp.zeros((n_ctx, sin.shape[1]), F32), sin], axis=0)
    cos_rows = jnp.tile(cos, (1, 2 * RET_HEADS))
    sin_rows = jnp.tile(jnp.concatenate([-sin, sin], axis=1), (1, RET_HEADS))
    log_gamma = jax.nn.log_sigmoid(p['ret_decay'])
    idx = jnp.arange(C, dtype=F32)
    rel = idx[:, None] - idx[None, :]
    lg_f, lg_b = log_gamma[0][:, None, None], log_gamma[1][:, None, None]
    inner = jnp.stack([jnp.where(rel[None] >= 0, jnp.exp(lg_f * jnp.maximum(rel, 0.0)[None]), 0.0),
                       jnp.where(rel[None] <= 0, jnp.exp(lg_b * jnp.maximum(-rel, 0.0)[None]), 0.0)])
    cross = jnp.stack([jnp.exp(lg_f * (idx + 1.0)[None, :, None]), jnp.exp(lg_b * (C - idx)[None, :, None])])
    cross = jnp.broadcast_to(cross, (2, RET_HEADS, C, RET_VAL))
    tailT = jnp.stack([jnp.exp(lg_f * (C - 1.0 - idx)[None, None, :]), jnp.exp(lg_b * idx[None, None, :])])
    tailT = jnp.broadcast_to(tailT, (2, RET_HEADS, RET_KEY, C))
    cdec = jnp.broadcast_to(jnp.exp(log_gamma * C)[:, :, None, None], (2, RET_HEADS, RET_KEY, RET_VAL))
    o_f, o_b = _retention_scan(u_ret, cos_rows, sin_rows, inner, cross, tailT, cdec)
    return o_f, o_b


def _moe(h2, routed, w_gate, w_up, w_down, layer):
    n_tok, D = h2.shape
    n_assign = n_tok * TOP_K
    flat_e = routed[:, :TOP_K].astype(jnp.int32).reshape(n_assign)
    sorted_e, order = lax.sort((flat_e, jnp.arange(n_assign, dtype=jnp.int32)), num_keys=1, is_stable=True)
    bounds = jnp.searchsorted(sorted_e, jnp.arange(N_EXPERTS + 1, dtype=jnp.int32),
                              method='scan_unrolled').astype(jnp.int32)
    raw_start, counts = bounds[:-1], bounds[1:] - bounds[:-1]
    padded = ((counts + MOE_TILE - 1) // MOE_TILE) * MOE_TILE
    pad_end = jnp.cumsum(padded)
    pad_start = pad_end - padded
    rank = jnp.argsort(order).astype(jnp.int32)
    n_tiles = n_assign // MOE_TILE + N_EXPERTS
    tile_start = jnp.arange(n_tiles, dtype=jnp.int32) * MOE_TILE
    tile_expert = jnp.minimum(
        jnp.sum((tile_start[:, None] >= pad_end[None, :]).astype(jnp.int32), axis=1), N_EXPERTS - 1)
    first = (tile_start - pad_start[tile_expert] + raw_start[tile_expert])[:, None]
    last = (raw_start[tile_expert] + counts[tile_expert] - 1)[:, None]
    src_sorted = jnp.clip(jnp.minimum(first + jnp.arange(MOE_TILE, dtype=jnp.int32)[None, :], last), 0, n_assign - 1)
    src_tok = order[src_sorted.reshape(-1)] // TOP_K
    pos = (rank + (pad_start - raw_start)[flat_e]).reshape(n_tok, TOP_K)
    out = _moe_experts(tile_expert, h2[src_tok], w_gate, w_up, w_down, layer)
    return out[pos[:, 0]], out[pos[:, 1]]


def _permute_w_in(w_in):
    w_a = w_in[:, :U_RWKV]
    lo = U_RWKV + MLA_Q_RANK + MLA_KV_RANK
    w_b = jnp.concatenate([w_in[:, U_RWKV:lo], jnp.pad(w_in[:, lo:lo + MLA_ROPE], ((0, 0), (MLA_NOPE, MLA_SLOT - MLA_NOPE - MLA_ROPE)))], axis=1)
    w_c = w_in[:, U_RWKV + U_MLA_RAW:]
    return w_a.astype(BF16), w_b.astype(BF16), w_c.astype(BF16)


def kernel(x, c, ctx, c_ctx, w_mod, b_mod, norm1_w, w_in, rwkv_conv, rwkv_w0, rwkv_w_up, rwkv_a0, rwkv_a_up, rwkv_g_up, rwkv_k_k, rwkv_k_a, rwkv_r_k, rwkv_ln_w, rwkv_ln_b, mla_q_norm, mla_w_uq, mla_kv_norm, mla_w_ukv, ret_decay, w_branch_a, w_branch_b, w_branch_c, w_branch_gate, b_branch_gate, w_out, norm2_w, moe_w_group, moe_b_group, moe_w_expert, moe_b_expert, moe_w_gate, moe_w_up, moe_w_down, final_norm_w):
    B, n_lat, D = x.shape
    depth = w_mod.shape[0]
    ropes = (_rope_table(n_lat, MLA_ROPE), _rope_table(n_lat, RET_KEY))
    x_all = jnp.concatenate([ctx, x], axis=1)
    for l in range(depth):
        with_ctx = l < depth - 1
        p = {
            'rwkv_conv': rwkv_conv[l], 'rwkv_w0': rwkv_w0[l], 'rwkv_w_up': rwkv_w_up[l],
            'rwkv_a0': rwkv_a0[l], 'rwkv_a_up': rwkv_a_up[l], 'rwkv_g_up': rwkv_g_up[l],
            'rwkv_k_k': rwkv_k_k[l], 'rwkv_k_a': rwkv_k_a[l], 'rwkv_r_k': rwkv_r_k[l].reshape(RWKV_HEADS, HEAD_DIM),
            'rwkv_ln_w': rwkv_ln_w[l], 'rwkv_ln_b': rwkv_ln_b[l],
            'mla_q_norm': mla_q_norm[l], 'mla_w_uq': mla_w_uq[l], 'mla_kv_norm': mla_kv_norm[l],
            'mla_w_ukv': mla_w_ukv[l], 'ret_decay': ret_decay[l],
        }
        mod_lat = jax.nn.silu(c) @ w_mod[l] + b_mod[l]
        mod_ctx = jax.nn.silu(c_ctx) @ w_mod[l] + b_mod[l]
        mods = jnp.stack([jnp.broadcast_to(mod_ctx, mod_lat.shape), mod_lat], axis=1).reshape(B, 2, N_MOD, D)
        mods = jnp.pad(mods, ((0, 0), (0, 0), (0, 8 - N_MOD), (0, 0)))

        w_a, w_b, w_c = _permute_w_in(w_in[l])
        h, u_rwkv, u_mla, u_ret = _inproj(x_all, mods, norm1_w[l][None, :], w_a, w_b, w_c)

        rwkv_parts = _rwkv_branch(u_rwkv, p)
        yb = _mla_branch(u_mla, p, ropes[0])
        ret_parts = _retention_branch(u_ret, p, ropes[1])
        head_idx = jnp.arange(RWKV_DIM) // HEAD_DIM
        rwkv_consts = [p['rwkv_r_k'].reshape(1, RWKV_DIM), p['rwkv_ln_w'][None, :], p['rwkv_ln_b'][None, :],
                       (head_idx[:, None] == head_idx[None, :]).astype(BF16)]

        w_router = jnp.pad(jnp.concatenate([moe_w_group[l], moe_w_expert[l]], axis=1),
                           ((0, 0), (0, ROUTER_LANES - N_GROUPS - N_EXPERTS)))
        w_router_hi = w_router.astype(BF16)
        w_router = jnp.stack([w_router_hi, (w_router - w_router_hi.astype(F32)).astype(BF16)])
        b_router = jnp.pad(jnp.concatenate([moe_b_group[l], moe_b_expert[l]]),
                           (0, ROUTER_LANES - N_GROUPS - N_EXPERTS))[None, :]
        tile_off = 0 if with_ctx else CTX_LEN // ROW_TILE
        weights = [w_branch_gate[l].astype(BF16), b_branch_gate[l][None, :],
                   w_branch_a[l].astype(BF16), w_branch_b[l].astype(BF16), w_branch_c[l].astype(BF16),
                   w_out[l].astype(BF16), norm2_w[l][None, :], w_router, b_router]
        x_mid, h2, logits = _merge(x_all, h, yb, mods, rwkv_parts, rwkv_consts, ret_parts, u_ret, weights, tile_off)
        Lm = x_mid.shape[1]
        out_a, out_b = _moe(h2.reshape(B * Lm, D), logits.reshape(B * Lm, ROUTER_LANES),
                            moe_w_gate, moe_w_up, moe_w_down, l)
        x_all = _combine(x_mid, out_a.reshape(B, Lm, D), out_b.reshape(B, Lm, D), logits, mods,
                         final_norm_w[None, :], tile_off, final=not with_ctx)
    return x_all
```
